```python
import jax, jax.numpy as jnp
from jax import lax
import numpy as np

D_MODEL = 1024
BATCH = 2
SEQ = 8192
DEPTH = 4
DEC_BATCH = 128
DEC_SEQ = 8
PAST_LEN = 8192
PAGE_SIZE = 128

N_MIXERS = 2
N_HEADS = 16
N_KV_HEADS = 4
HEAD_DIM = 64
Q_PER_KV = N_HEADS // N_KV_HEADS
WINDOW = 128
BLOCK = 128
ROT_DIM = HEAD_DIM // 4
ROPE_THETA = 500000.0
CHUNK = 128
SGU_GROUPS = 4
D_SGU = 2 * D_MODEL
SGU_GROUP_DIM = D_SGU // SGU_GROUPS
D_FF = 2816
CONV_W = 3
N_ATTN_LAYERS = (DEPTH + 1) // 2
N_SGU_LAYERS = DEPTH // 2
QKV_DIM = (N_HEADS + 2 * N_KV_HEADS) * HEAD_DIM
EPS = 1e-6

kernel_name = 'hybrid_swa_sink_sgu_convffn_step'


def rmsnorm(x, g):
    xf = x.astype(jnp.float32)
    y = xf * lax.rsqrt(jnp.mean(xf * xf, -1, keepdims=True) + EPS)
    return (y * g.astype(jnp.float32)).astype(x.dtype)


def layernorm(x, g, b):
    xf = x.astype(jnp.float32)
    mu = jnp.mean(xf, -1, keepdims=True)
    xc = xf - mu
    y = xc * lax.rsqrt(jnp.mean(xc * xc, -1, keepdims=True) + EPS)
    return (y * g.astype(jnp.float32) + b.astype(jnp.float32)).astype(x.dtype)


def rope(x, pos):
    half = ROT_DIM // 2
    inv = ROPE_THETA ** (-jnp.arange(0, ROT_DIM, 2, dtype=jnp.float32) / ROT_DIM)
    ang = pos.astype(jnp.float32)[:, None] * inv[None, :]
    cos = jnp.cos(ang)[:, None, :]
    sin = jnp.sin(ang)[:, None, :]
    xf = x.astype(jnp.float32)
    x1 = xf[..., :half]
    x2 = xf[..., half:ROT_DIM]
    out = jnp.concatenate([x1 * cos - x2 * sin, x2 * cos + x1 * sin, xf[..., ROT_DIM:]], -1)
    return out.astype(x.dtype)


def qkv_rope(h, w, b, pos):
    B, T = h.shape[:2]
    y = h @ w + b
    nq = N_HEADS * HEAD_DIM
    nk = N_KV_HEADS * HEAD_DIM
    q = y[..., :nq].reshape(B, T, N_HEADS, HEAD_DIM)
    k = y[..., nq:nq + nk].reshape(B, T, N_KV_HEADS, HEAD_DIM)
    v = y[..., nq + nk:].reshape(B, T, N_KV_HEADS, HEAD_DIM)
    return rope(q, pos), rope(k, pos), v


def sink_softmax(s, sink, mask):
    s = jnp.where(mask, s, -jnp.inf)
    sk = sink.astype(jnp.float32)[:, :, None, None]
    m = jnp.maximum(s.max(-1, keepdims=True), sk)
    p = jnp.exp(s - m)
    return p / (p.sum(-1, keepdims=True) + jnp.exp(sk - m))


def window_attn_prompt(q, k, v, sink):
    B, S = q.shape[:2]
    nb = S // BLOCK
    qb = q.reshape(B, nb, BLOCK, N_KV_HEADS, Q_PER_KV, HEAD_DIM)
    kb = k.reshape(B, nb, BLOCK, N_KV_HEADS, HEAD_DIM)
    vb = v.reshape(B, nb, BLOCK, N_KV_HEADS, HEAD_DIM)

    def with_prev(xb):
        prev = jnp.pad(xb, ((0, 0), (1, 0), (0, 0), (0, 0), (0, 0)))[:, :-1]
        return jnp.concatenate([prev, xb], axis=2)

    kc, vc = with_prev(kb), with_prev(vb)
    s = jnp.einsum('bnqkgd,bnskd->bnkgqs', qb, kc, preferred_element_type=jnp.float32) * (HEAD_DIM ** -0.5)
    i = jnp.arange(BLOCK)[:, None]
    j = jnp.arange(2 * BLOCK)[None, :]
    diff = BLOCK + i - j
    band = (diff >= 0) & (diff <= WINDOW)
    valid = (jnp.arange(nb) > 0)[:, None, None] | (j >= BLOCK)[None]
    mask = (band[None] & valid)[None, :, None, None]
    p = sink_softmax(s, sink.reshape(N_KV_HEADS, Q_PER_KV), mask)
    o = jnp.einsum('bnkgqs,bnskd->bnqkgd', p.astype(vc.dtype), vc)
    return o.reshape(B, S, N_HEADS * HEAD_DIM)


def window_attn_sample(q, k_new, v_new, k_cache, v_cache, sink):
    DB, T = q.shape[:2]
    W = k_cache.shape[1]
    kc = jnp.concatenate([k_cache.astype(k_new.dtype), k_new], 1)
    vc = jnp.concatenate([v_cache.astype(v_new.dtype), v_new], 1)
    qg = q.reshape(DB, T, N_KV_HEADS, Q_PER_KV, HEAD_DIM)
    s = jnp.einsum('btkgd,bskd->bkgts', qg, kc, preferred_element_type=jnp.float32) * (HEAD_DIM ** -0.5)
    t = jnp.arange(T)[:, None]
    j = jnp.arange(W + T)[None, :]
    diff = W + t - j
    mask = (diff >= 0) & (diff <= WINDOW)
    p = sink_softmax(s, sink.reshape(N_KV_HEADS, Q_PER_KV), mask)
    o = jnp.einsum('bkgts,bskd->btkgd', p.astype(vc.dtype), vc)
    return o.reshape(DB, T, N_HEADS * HEAD_DIM), kc[:, -W:], vc[:, -W:]


def sgu_mix(h, w_in, b_in, ln_g, ln_b, w_sp, b_sp, w_out):
    B, T = h.shape[:2]
    tc = min(T, CHUNK)
    z = jax.nn.gelu(h @ w_in + b_in)
    u, v = jnp.split(z, 2, axis=-1)
    v = layernorm(v, ln_g, ln_b)
    tri = jnp.tril(jnp.ones((CHUNK, CHUNK), w_sp.dtype))
    wm = (w_sp * tri)[:, :tc, :tc]
    vb = v.reshape(B, T // tc, tc, SGU_GROUPS, SGU_GROUP_DIM)
    mixed = jnp.einsum('gts,bnsgc->bntgc', wm, vb) + b_sp[:, :tc].T[None, None, :, :, None]
    out = (u * mixed.reshape(B, T, D_SGU)) @ w_out
    return out, v


def conv_ffn(h, past, w_up, conv_w, conv_b, w_down):
    T = h.shape[1]
    a = h @ w_up
    ap = jnp.concatenate([past.astype(a.dtype), a], 1)
    conv = sum(conv_w[j] * ap[:, j:j + T] for j in range(CONV_W)) + conv_b
    g, u = jnp.split(conv, 2, axis=-1)
    y = (jax.nn.silu(g) * u) @ w_down
    return y, ap[:, T:]


def trunk(x, c, start, cache_k, cache_v, state_conv, w_ada, b_ada, norm_mix, norm_ffn,
          w_qkv, b_qkv, attn_sink, w_o, w_sgu_in, b_sgu_in, sgu_ln_g, sgu_ln_b,
          w_spatial, b_spatial, w_sgu_out, w_up, conv_w, conv_b, w_down, norm_final):
    B, T = x.shape[:2]
    pos = start + jnp.arange(T)
    new_k, new_v, new_conv, new_sgu = [], [], [], []
    for l in range(DEPTH):
        mod = (jax.nn.silu(c) @ w_ada[l] + b_ada[l]).astype(x.dtype)[:, None, :]
        sh1, sc1, g1, sh2, sc2, g2 = jnp.split(mod, 6, axis=-1)
        h = rmsnorm(x, norm_mix[l]) * (1 + sc1) + sh1
        idx = l // N_MIXERS
        if l % N_MIXERS == 0:
            q, k, v = qkv_rope(h, w_qkv[idx], b_qkv[idx], pos)
            if cache_k is None:
                o = window_attn_prompt(q, k, v, attn_sink[idx])
                wkeep = min(WINDOW, T)
                nk, nv = k[:, T - wkeep:], v[:, T - wkeep:]
            else:
                o, nk, nv = window_attn_sample(q, k, v, cache_k[idx], cache_v[idx], attn_sink[idx])
            new_k.append(nk)
            new_v.append(nv)
            mix = o @ w_o[idx]
        else:
            mix, vrows = sgu_mix(h, w_sgu_in[idx], b_sgu_in[idx], sgu_ln_g[idx], sgu_ln_b[idx],
                                 w_spatial[idx], b_spatial[idx], w_sgu_out[idx])
            if cache_k is not None:
                new_sgu.append(vrows)
        x = x + g1 * mix
        h = rmsnorm(x, norm_ffn[l]) * (1 + sc2) + sh2
        past = jnp.zeros((B, CONV_W - 1, 2 * D_FF), x.dtype) if state_conv is None else state_conv[l]
        f, st = conv_ffn(h, past, w_up[l], conv_w[l], conv_b[l], w_down[l])
        new_conv.append(st)
        x = x + g2 * f
    y = rmsnorm(x, norm_final)
    return y, jnp.stack(new_k), jnp.stack(new_v), jnp.stack(new_conv), new_sgu


def setup_inputs(seed: int = 0) -> dict:
    key = jax.random.key(seed)
    ks = jax.random.split(key, 32)
    f32 = jnp.float32

    def nrm(k, shape, scale):
        return jax.random.normal(k, shape, f32) * scale

    cache_win = min(WINDOW, PAST_LEN)
    return {
        'x_prompt': nrm(ks[0], (BATCH, SEQ, D_MODEL), 1.0),
        'x_sample': nrm(ks[1], (DEC_BATCH, DEC_SEQ, D_MODEL), 1.0),
        'c_prompt': nrm(ks[2], (BATCH, D_MODEL), 1.0),
        'c_sample': nrm(ks[3], (DEC_BATCH, D_MODEL), 1.0),
        'cache_k': nrm(ks[4], (N_ATTN_LAYERS, DEC_BATCH, cache_win, N_KV_HEADS, HEAD_DIM), 1.0),
        'cache_v': nrm(ks[5], (N_ATTN_LAYERS, DEC_BATCH, cache_win, N_KV_HEADS, HEAD_DIM), 1.0),
        'state_conv': nrm(ks[6], (DEPTH, DEC_BATCH, CONV_W - 1, 2 * D_FF), 1.0),
        'w_ada': nrm(ks[7], (DEPTH, D_MODEL, 6 * D_MODEL), 0.5 * D_MODEL ** -0.5),
        'b_ada': nrm(ks[8], (DEPTH, 6 * D_MODEL), 0.02),
        'norm_mix': 1.0 + nrm(ks[9], (DEPTH, D_MODEL), 0.05),
        'norm_ffn': 1.0 + nrm(ks[10], (DEPTH, D_MODEL), 0.05),
        'w_qkv': nrm(ks[11], (N_ATTN_LAYERS, D_MODEL, QKV_DIM), D_MODEL ** -0.5),
        'b_qkv': nrm(ks[12], (N_ATTN_LAYERS, QKV_DIM), 0.02),
        'attn_sink': nrm(ks[13], (N_ATTN_LAYERS, N_HEADS), 0.5),
        'w_o': nrm(ks[14], (N_ATTN_LAYERS, N_HEADS * HEAD_DIM, D_MODEL), (N_HEADS * HEAD_DIM) ** -0.5),
        'w_sgu_in': nrm(ks[15], (N_SGU_LAYERS, D_MODEL, 2 * D_SGU), D_MODEL ** -0.5),
        'b_sgu_in': nrm(ks[16], (N_SGU_LAYERS, 2 * D_SGU), 0.02),
        'sgu_ln_g': 1.0 + nrm(ks[17], (N_SGU_LAYERS, D_SGU), 0.05),
        'sgu_ln_b': nrm(ks[18], (N_SGU_LAYERS, D_SGU), 0.02),
        'w_spatial': nrm(ks[19], (N_SGU_LAYERS, SGU_GROUPS, CHUNK, CHUNK), CHUNK ** -0.5),
        'b_spatial': 1.0 + nrm(ks[20], (N_SGU_LAYERS, SGU_GROUPS, CHUNK), 0.1),
        'w_sgu_out': nrm(ks[21], (N_SGU_LAYERS, D_SGU, D_MODEL), D_SGU ** -0.5),
        'w_up': nrm(ks[22], (DEPTH, D_MODEL, 2 * D_FF), D_MODEL ** -0.5),
        'conv_w': nrm(ks[23], (DEPTH, CONV_W, 2 * D_FF), CONV_W ** -0.5),
        'conv_b': nrm(ks[24], (DEPTH, 2 * D_FF), 0.02),
        'w_down': nrm(ks[25], (DEPTH, D_FF, D_MODEL), D_FF ** -0.5),
        'norm_final': 1.0 + nrm(ks[26], (D_MODEL,), 0.05),
    }


def reference(x_prompt, x_sample, c_prompt, c_sample, cache_k, cache_v, state_conv,
              w_ada, b_ada, norm_mix, norm_ffn, w_qkv, b_qkv, attn_sink, w_o,
              w_sgu_in, b_sgu_in, sgu_ln_g, sgu_ln_b, w_spatial, b_spatial, w_sgu_out,
              w_up, conv_w, conv_b, w_down, norm_final):
    y_prompt, k_p, v_p, conv_p, _ = trunk(
        x_prompt, c_prompt, 0, None, None, None, w_ada, b_ada, norm_mix, norm_ffn,
        w_qkv, b_qkv, attn_sink, w_o, w_sgu_in, b_sgu_in, sgu_ln_g, sgu_ln_b,
        w_spatial, b_spatial, w_sgu_out, w_up, conv_w, conv_b, w_down, norm_final)
    y_sample, k_s, v_s, conv_s, sgu_rows = trunk(
        x_sample, c_sample, PAST_LEN, cache_k, cache_v, state_conv, w_ada, b_ada, norm_mix, norm_ffn,
        w_qkv, b_qkv, attn_sink, w_o, w_sgu_in, b_sgu_in, sgu_ln_g, sgu_ln_b,
        w_spatial, b_spatial, w_sgu_out, w_up, conv_w, conv_b, w_down, norm_final)
    sgu_v_s = jnp.stack(sgu_rows)
    return (y_prompt, y_sample, k_p, v_p, conv_p, k_s, v_s, conv_s, sgu_v_s)
```

```python
import functools

import jax
import jax.numpy as jnp
from jax import lax
from jax.experimental import pallas as pl
from jax.experimental.pallas import tpu as pltpu

N_HEADS = 16
N_KV_HEADS = 4
HEAD_DIM = 64
Q_PER_KV = N_HEADS // N_KV_HEADS
WINDOW = 128
ROT_DIM = HEAD_DIM // 4
ROPE_THETA = 500000.0
CHUNK = 128
SGU_GROUPS = 4
CONV_W = 3
EPS = 1e-6
N_MIXERS = 2
PAST_LEN = 8192

LANES = 128
SUBLANES = 8
FF_CHUNK = 256
VMEM_LIMIT = 56 * 1024 * 1024
NEG_BIG = -1e30

F32 = jnp.float32
BF16 = jnp.bfloat16


def _params(n_axes=1, vmem=VMEM_LIMIT):
    return pltpu.CompilerParams(dimension_semantics=("arbitrary",) * n_axes, vmem_limit_bytes=vmem)


def _resident(shape):
    nd = len(shape)
    return pl.BlockSpec(shape, lambda *_: (0,) * nd, pipeline_mode=pl.Buffered(1))


def _modnorm(x, gw, scale, shift):
    ms = jnp.mean(x * x, axis=-1, keepdims=True)
    y = x * lax.rsqrt(ms + EPS) * gw
    return y * (1.0 + scale) + shift


def _silu(x):
    return x * jax.nn.sigmoid(x)


def _gelu_tanh(x):
    c = 0.7978845608028654
    return 0.5 * x * (1.0 + jnp.tanh(c * (x + 0.044715 * (x * x * x))))


def _ada_kernel(c_ref, w_ref, b_ref, o_ref):
    s = _silu(c_ref[...]).astype(BF16)
    o_ref[0] = jnp.dot(s, w_ref[0].astype(BF16), preferred_element_type=F32) + b_ref[0]


def _ada(c_all, w_ada, b_ada):
    depth, d, n6 = w_ada.shape
    rows = c_all.shape[0]
    tn = 1024
    return pl.pallas_call(
        _ada_kernel,
        grid=(depth, n6 // tn),
        in_specs=[
            pl.BlockSpec((rows, d), lambda l, n: (0, 0)),
            pl.BlockSpec((1, d, tn), lambda l, n: (l, 0, n)),
            pl.BlockSpec((1, 1, tn), lambda l, n: (l, 0, n)),
        ],
        out_specs=pl.BlockSpec((1, rows, tn), lambda l, n: (l, 0, n)),
        out_shape=jax.ShapeDtypeStruct((depth, rows, n6), F32),
        compiler_params=_params(2),
        name="ada",
    )(c_all, w_ada, b_ada.reshape(depth, 1, n6))


def _qkv_kernel(x_ref, sc_ref, sh_ref, gw_ref, w_ref, b_ref, cos_ref, sa_ref, sb_ref,
                q_ref, k_ref, v_ref):
    h = _modnorm(x_ref[...], gw_ref[...], sc_ref[0], sh_ref[0]).astype(BF16)
    y = jnp.dot(h, w_ref[...], preferred_element_type=F32) + b_ref[...]
    cos, sa, sb = cos_ref[...], sa_ref[...], sb_ref[...]
    nq = N_HEADS * HEAD_DIM
    nk = N_KV_HEADS * HEAD_DIM

    def rope(yb):
        return yb * cos + pltpu.roll(yb, LANES - ROT_DIM // 2, 1) * sa + pltpu.roll(yb, ROT_DIM // 2, 1) * sb

    for j in range(nq // LANES):
        q_ref[:, j * LANES:(j + 1) * LANES] = rope(y[:, j * LANES:(j + 1) * LANES]) * (HEAD_DIM ** -0.5)
    for j in range(nk // LANES):
        k_ref[:, j * LANES:(j + 1) * LANES] = rope(y[:, nq + j * LANES:nq + (j + 1) * LANES])
    v_ref[...] = y[:, nq + nk:]


def _rope_tables(pos):
    half = ROT_DIM // 2
    inv = ROPE_THETA ** (-jnp.arange(0, ROT_DIM, 2, dtype=F32) / ROT_DIM)
    ang = pos.astype(F32)[:, None] * inv[None, :]
    cos, sin = jnp.cos(ang), jnp.sin(ang)
    t = pos.shape[0]
    one = jnp.ones((t, HEAD_DIM - ROT_DIM), F32)
    zero = jnp.zeros((t, HEAD_DIM - ROT_DIM), F32)
    zh = jnp.zeros((t, half), F32)
    cos_t = jnp.concatenate([cos, cos, one], 1)
    sa_t = jnp.concatenate([-sin, zh, zero], 1)
    sb_t = jnp.concatenate([zh, sin, zero], 1)
    rep = LANES // HEAD_DIM
    return tuple(jnp.tile(a, (1, rep)) for a in (cos_t, sa_t, sb_t))


def _mod_spec(mod, tiles_per_group):
    return pl.BlockSpec((1,) + mod.shape[1:], lambda i: (i // tiles_per_group, 0, 0))


def _qkv(x, sc, sh, gw, w, b, tables, tm, tiles_per_group, table_tiles):
    m, d = x.shape
    n = w.shape[1]
    nq = N_HEADS * HEAD_DIM
    nk = N_KV_HEADS * HEAD_DIM
    row = lambda width: pl.BlockSpec((tm, width), lambda i: (i, 0))
    tab = pl.BlockSpec((tm, LANES), lambda i: (i % table_tiles, 0))
    return pl.pallas_call(
        _qkv_kernel,
        grid=(m // tm,),
        in_specs=[row(d), _mod_spec(sc, tiles_per_group), _mod_spec(sh, tiles_per_group),
                  _resident((1, d)), _resident((d, n)), _resident((1, n)), tab, tab, tab],
        out_specs=[row(nq), row(nk), row(nk)],
        out_shape=[jax.ShapeDtypeStruct((m, nq), F32), jax.ShapeDtypeStruct((m, nk), F32),
                   jax.ShapeDtypeStruct((m, nk), F32)],
        compiler_params=_params(1),
        name="qkv",
    )(x, sc, sh, gw, w, b, *tables)


def _attn_core(q, kwin, vwin, sink_ref, prev_valid):
    tq = q.shape[0]
    nwin = kwin.shape[0]
    lo = lax.broadcasted_iota(jnp.int32, (1, LANES), 1) < HEAD_DIM
    i = lax.broadcasted_iota(jnp.int32, (tq, nwin), 0)
    j = lax.broadcasted_iota(jnp.int32, (tq, nwin), 1)
    diff = WINDOW + i - j
    band = (diff >= 0) & (diff <= WINDOW)
    if prev_valid is not None:
        band = band & ((j >= WINDOW) | prev_valid)
    band = jnp.concatenate([band] * Q_PER_KV, axis=0)
    outs = []
    for kv in range(N_KV_HEADS):
        blk = kv // 2
        kp = kwin[:, blk * LANES:(blk + 1) * LANES]
        vp = vwin[:, blk * LANES:(blk + 1) * LANES]
        kr = pltpu.roll(kp, HEAD_DIM, 1)
        vr = pltpu.roll(vp, HEAD_DIM, 1)
        if kv % 2 == 0:
            kd, vd = jnp.where(lo, kp, kr), jnp.where(lo, vp, vr)
        else:
            kd, vd = jnp.where(lo, kr, kp), jnp.where(lo, vr, vp)
        q0 = q[:, (2 * kv) * LANES:(2 * kv + 1) * LANES]
        q1 = q[:, (2 * kv + 1) * LANES:(2 * kv + 2) * LANES]
        lhs = jnp.concatenate([jnp.where(lo, q0, 0.0), jnp.where(lo, 0.0, q0),
                               jnp.where(lo, q1, 0.0), jnp.where(lo, 0.0, q1)], axis=0)
        s = lax.dot_general(lhs.astype(BF16), kd.astype(BF16), (((1,), (1,)), ((), ())),
                            preferred_element_type=F32)
        sink = jnp.concatenate([jnp.full((tq, 1), sink_ref[Q_PER_KV * kv + g], F32)
                                for g in range(Q_PER_KV)], axis=0)
        s = jnp.where(band, s, NEG_BIG)
        mx = jnp.maximum(jnp.max(s, axis=-1, keepdims=True), sink)
        p = jnp.exp(s - mx)
        den = jnp.sum(p, axis=-1, keepdims=True) + jnp.exp(sink - mx)
        o = jnp.dot(p.astype(BF16), vd.astype(BF16), preferred_element_type=F32) / den
        outs.append(jnp.where(lo, o[0:tq], o[tq:2 * tq]))
        outs.append(jnp.where(lo, o[2 * tq:3 * tq], o[3 * tq:4 * tq]))
    return jnp.concatenate(outs, axis=1)


def _attn_prompt_kernel(sink_ref, q_ref, ko_ref, kp_ref, vo_ref, vp_ref, wo_ref, x_ref, g1_ref, o_ref):
    kwin = jnp.concatenate([kp_ref[...], ko_ref[...]], axis=0)
    vwin = jnp.concatenate([vp_ref[...], vo_ref[...]], axis=0)
    o = _attn_core(q_ref[...], kwin, vwin, sink_ref, pl.program_id(1) > 0)
    mix = jnp.dot(o.astype(BF16), wo_ref[...], preferred_element_type=F32)
    o_ref[...] = x_ref[...] + g1_ref[0] * mix


def _attn_prompt(sink, q, k, v, wo, x, g1, batch):
    m, d = x.shape
    nb = m // batch // WINDOW
    nk = k.shape[1]
    own = lambda width: pl.BlockSpec((WINDOW, width), lambda b, n: (b * nb + n, 0))
    prev = lambda width: pl.BlockSpec((WINDOW, width), lambda b, n: (b * nb + jnp.maximum(n - 1, 0), 0))
    return pl.pallas_call(
        _attn_prompt_kernel,
        grid=(batch, nb),
        in_specs=[pl.BlockSpec(memory_space=pltpu.SMEM), own(q.shape[1]), own(nk), prev(nk), own(nk), prev(nk),
                  pl.BlockSpec(wo.shape, lambda b, n: (0, 0), pipeline_mode=pl.Buffered(1)),
                  own(d), pl.BlockSpec((1, 1, d), lambda b, n: (b, 0, 0))],
        out_specs=own(d),
        out_shape=jax.ShapeDtypeStruct((m, d), F32),
        compiler_params=_params(2),
        name="attn_prompt",
    )(sink, q, k, k, v, v, wo, x, g1)


def _attn_sample_kernel(sink_ref, q_ref, kn_ref, vn_ref, kc_ref, vc_ref, wo_ref, x_ref, g1_ref,
                        o_ref, ko_ref, vo_ref, o_scr, *, t_new):
    nseq = kc_ref.shape[0]
    w = kc_ref.shape[1]
    pad = jnp.zeros((w - t_new, kc_ref.shape[2]), F32)

    def body(b, carry):
        r = pl.multiple_of(b * t_new, t_new)
        kn, vn = kn_ref[pl.ds(r, t_new), :], vn_ref[pl.ds(r, t_new), :]
        kc, vc = kc_ref[b], vc_ref[b]
        kwin = jnp.concatenate([kc, kn, pad], axis=0)
        vwin = jnp.concatenate([vc, vn, pad], axis=0)
        o_scr[pl.ds(r, t_new), :] = _attn_core(q_ref[pl.ds(r, t_new), :], kwin, vwin, sink_ref, None)
        ko_ref[b, 0:w - t_new, :] = kc[t_new:]
        ko_ref[b, w - t_new:w, :] = kn
        vo_ref[b, 0:w - t_new, :] = vc[t_new:]
        vo_ref[b, w - t_new:w, :] = vn
        return carry

    lax.fori_loop(0, nseq, body, 0)
    mix = jnp.dot(o_scr[...].astype(BF16), wo_ref[...], preferred_element_type=F32)
    o_ref[...] = x_ref[...] + g1_ref[...] * mix


def _attn_sample(sink, q, kn, vn, kc, vc, wo, x, g1, t_new):
    m, d = x.shape
    db, w, nk = kc.shape
    g = WINDOW // t_new
    tm = g * t_new
    row = lambda width: pl.BlockSpec((tm, width), lambda i: (i, 0))
    cache = pl.BlockSpec((g, w, nk), lambda i: (i, 0, 0))
    return pl.pallas_call(
        functools.partial(_attn_sample_kernel, t_new=t_new),
        grid=(db // g,),
        in_specs=[pl.BlockSpec(memory_space=pltpu.SMEM), row(q.shape[1]), row(nk), row(nk), cache, cache,
                  _resident(wo.shape), row(d), row(d)],
        out_specs=[row(d), cache, cache],
        out_shape=[jax.ShapeDtypeStruct((m, d), F32), jax.ShapeDtypeStruct(kc.shape, F32),
                   jax.ShapeDtypeStruct(vc.shape, F32)],
        scratch_shapes=[pltpu.VMEM((tm, q.shape[1]), F32)],
        compiler_params=_params(1),
        name="attn_sample",
    )(sink, q, kn, vn, kc, vc, wo, x, g1)


def _sgu_kernel(x_ref, sc_ref, sh_ref, g1_ref, gw_ref, win_ref, bin_ref, lng_ref, lnb_ref,
                wsp_ref, bsp_ref, wout_ref, o_ref, *v_out):
    x = x_ref[...]
    tm = x.shape[0]
    d_sgu = lng_ref.shape[1]
    gdim = d_sgu // SGU_GROUPS
    h = _modnorm(x, gw_ref[...], sc_ref[0], sh_ref[0]).astype(BF16)
    z = _gelu_tanh(jnp.dot(h, win_ref[...], preferred_element_type=F32) + bin_ref[...])
    u, v = z[:, :d_sgu], z[:, d_sgu:]
    mu = jnp.mean(v, axis=-1, keepdims=True)
    vc = v - mu
    var = jnp.mean(vc * vc, axis=-1, keepdims=True)
    vn = vc * lax.rsqrt(var + EPS) * lng_ref[...] + lnb_ref[...]
    if v_out:
        v_out[0][...] = vn
    vb = vn.astype(BF16)
    r = lax.broadcasted_iota(jnp.int32, (CHUNK, CHUNK), 0)
    c = lax.broadcasted_iota(jnp.int32, (CHUNK, CHUNK), 1)
    wm = [jnp.where(r >= c, wsp_ref[g], 0.0).astype(BF16) for g in range(SGU_GROUPS)]
    rows = []
    for ch in range(tm // CHUNK):
        cols = [jnp.dot(wm[g], vb[ch * CHUNK:(ch + 1) * CHUNK, g * gdim:(g + 1) * gdim],
                        preferred_element_type=F32) + bsp_ref[g]
                for g in range(SGU_GROUPS)]
        rows.append(jnp.concatenate(cols, axis=1))
    mixed = jnp.concatenate(rows, axis=0)
    out = jnp.dot((u * mixed).astype(BF16), wout_ref[...], preferred_element_type=F32)
    o_ref[...] = x + g1_ref[0] * out


def _sgu(x, sc, sh, g1, gw, win, b_in, lng, lnb, wsp, bsp, wout, tm, tiles_per_group, emit_v):
    m, d = x.shape
    d_sgu = wout.shape[0]
    row = lambda width: pl.BlockSpec((tm, width), lambda i: (i, 0))
    out_specs = [row(d)]
    out_shape = [jax.ShapeDtypeStruct((m, d), F32)]
    if emit_v:
        out_specs.append(row(d_sgu))
        out_shape.append(jax.ShapeDtypeStruct((m, d_sgu), F32))
    return pl.pallas_call(
        _sgu_kernel,
        grid=(m // tm,),
        in_specs=[row(d), _mod_spec(sc, tiles_per_group), _mod_spec(sh, tiles_per_group),
                  _mod_spec(g1, tiles_per_group), _resident((1, d)), _resident(win.shape),
                  _resident(b_in.shape), _resident(lng.shape), _resident(lnb.shape),
                  _resident(wsp.shape), _resident(bsp.shape), _resident(wout.shape)],
        out_specs=out_specs,
        out_shape=out_shape,
        compiler_params=_params(1),
        name="sgu",
    )(x, sc, sh, g1, gw, win, b_in, lng, lnb, wsp, bsp, wout)


def _ffn_gate_down(a, a1, a2, cw, cb, wdn):
    conv = cw[0:1] * a2 + cw[1:2] * a1 + cw[2:3] * a + cb
    half = conv.shape[1] // 2
    act = (_silu(conv[:, :half]) * conv[:, half:]).astype(BF16)
    return jnp.dot(act, wdn, preferred_element_type=F32)


def _final(xn, gf_ref):
    if gf_ref is None:
        return xn
    ms = jnp.mean(xn * xn, axis=-1, keepdims=True)
    return xn * lax.rsqrt(ms + EPS) * gf_ref[...]


def _ffn_prompt_kernel(x_ref, sc_ref, sh_ref, g2_ref, gw_ref, wup_ref, cw_ref, cb_ref, wdn_ref, *rest,
                       tiles_per_seq, final):
    gf_ref = rest[0] if final else None
    o_ref, tail_ref, a_scr = rest[-3:]
    nc = wup_ref.shape[0]
    x = x_ref[...]
    tm = x.shape[0]
    halo = SUBLANES

    @pl.when(pl.program_id(0) % tiles_per_seq == 0)
    def _():
        a_scr[:, 0:halo, :] = jnp.zeros((nc, halo, a_scr.shape[2]), F32)

    h = _modnorm(x, gw_ref[...], sc_ref[0], sh_ref[0]).astype(BF16)
    acc = None
    for j in range(nc):
        a = jnp.dot(h, wup_ref[j], preferred_element_type=F32)
        a_scr[j, halo:halo + tm, :] = a
        a1 = a_scr[j, halo - 1:halo - 1 + tm, :]
        a2 = a_scr[j, halo - 2:halo - 2 + tm, :]
        dn = _ffn_gate_down(a, a1, a2, cw_ref[j], cb_ref[j], wdn_ref[j])
        acc = dn if acc is None else acc + dn
        tail = a_scr[j, tm:tm + halo, :]
        a_scr[j, 0:halo, :] = tail
        tail_ref[0, j] = tail
    o_ref[...] = _final(x + g2_ref[0] * acc, gf_ref)


def _ffn_prompt(x, sc, sh, g2, gw, wup, cw, cb, wdn, gf, batch, tm):
    m, d = x.shape
    nc, _, c2 = wup.shape
    tiles_per_seq = m // batch // tm
    row = pl.BlockSpec((tm, d), lambda i: (i, 0))
    final = gf is not None
    in_specs = [row, _mod_spec(sc, tiles_per_seq), _mod_spec(sh, tiles_per_seq), _mod_spec(g2, tiles_per_seq),
                _resident((1, d)), _resident(wup.shape), _resident(cw.shape), _resident(cb.shape),
                _resident(wdn.shape)]
    args = [x, sc, sh, g2, gw, wup, cw, cb, wdn]
    if final:
        in_specs.append(_resident((1, d)))
        args.append(gf)
    return pl.pallas_call(
        functools.partial(_ffn_prompt_kernel, tiles_per_seq=tiles_per_seq, final=final),
        grid=(m // tm,),
        in_specs=in_specs,
        out_specs=[row, pl.BlockSpec((1, nc, SUBLANES, c2), lambda i: (i // tiles_per_seq, 0, 0, 0))],
        out_shape=[jax.ShapeDtypeStruct((m, d), F32), jax.ShapeDtypeStruct((batch, nc, SUBLANES, c2), F32)],
        scratch_shapes=[pltpu.VMEM((nc, tm + SUBLANES, c2), F32)],
        compiler_params=_params(1),
        name="ffn_prompt",
    )(*args)


def _ffn_sample_kernel(x_ref, sc_ref, sh_ref, g2_ref, gw_ref, st_ref, wup_ref, cw_ref, cb_ref, wdn_ref, *rest,
                       final):
    gf_ref = rest[0] if final else None
    o_ref, nst_ref, ring = rest[-3:]
    nc = wup_ref.shape[0]
    t = pl.program_id(0)

    @pl.when(t == 0)
    def _():
        ring[1] = st_ref[0]
        ring[2] = st_ref[1]

    s0, s1, s2 = t % 3, (t + 2) % 3, (t + 1) % 3
    x = x_ref[...]
    h = _modnorm(x, gw_ref[...], sc_ref[...], sh_ref[...]).astype(BF16)
    acc = None
    for j in range(nc):
        a = jnp.dot(h, wup_ref[j], preferred_element_type=F32)
        ring[s0, j] = a
        nst_ref[0, j] = a
        dn = _ffn_gate_down(a, ring[s1, j], ring[s2, j], cw_ref[j], cb_ref[j], wdn_ref[j])
        acc = dn if acc is None else acc + dn
    o_ref[...] = _final(x + g2_ref[...] * acc, gf_ref)


def _ffn_sample(x2, sc, sh, g2, gw, st, wup, cw, cb, wdn, gf):
    db, d = sc.shape
    t_new = x2.shape[1] // d
    nc, _, c2 = wup.shape
    step = pl.BlockSpec((db, d), lambda t: (0, t))
    final = gf is not None
    in_specs = [step, _resident((db, d)), _resident((db, d)), _resident((db, d)), _resident((1, d)),
                _resident(st.shape), _resident(wup.shape), _resident(cw.shape), _resident(cb.shape),
                _resident(wdn.shape)]
    args = [x2, sc, sh, g2, gw, st, wup, cw, cb, wdn]
    if final:
        in_specs.append(_resident((1, d)))
        args.append(gf)
    keep = CONV_W - 1
    return pl.pallas_call(
        functools.partial(_ffn_sample_kernel, final=final),
        grid=(t_new,),
        in_specs=in_specs,
        out_specs=[step, pl.BlockSpec((1, nc, db, c2),
                                      lambda t: (jnp.maximum(t - (t_new - keep), 0), 0, 0, 0))],
        out_shape=[jax.ShapeDtypeStruct((db, t_new * d), F32), jax.ShapeDtypeStruct((keep, nc, db, c2), F32)],
        scratch_shapes=[pltpu.VMEM((CONV_W, nc, db, c2), F32)],
        compiler_params=_params(1),
        name="ffn_sample",
    )(*args)


def _chunk_cols(a, nc):
    lead = a.shape[:-1]
    a = a.reshape(lead + (2, nc, FF_CHUNK))
    a = jnp.moveaxis(a, -2, 0)
    return a.reshape((nc,) + lead + (2 * FF_CHUNK,))


def _unchunk_cols(a):
    nc = a.shape[0]
    lead = a.shape[1:-1]
    a = a.reshape((nc,) + lead + (2, FF_CHUNK))
    a = jnp.moveaxis(a, 0, -2)
    return a.reshape(lead + (2 * nc * FF_CHUNK,))


def _tile_rows(size, want):
    return want if size % want == 0 else size


def kernel(x_prompt, x_sample, c_prompt, c_sample, cache_k, cache_v, state_conv, w_ada, b_ada, norm_mix,
           norm_ffn, w_qkv, b_qkv, attn_sink, w_o, w_sgu_in, b_sgu_in, sgu_ln_g, sgu_ln_b, w_spatial,
           b_spatial, w_sgu_out, w_up, conv_w, conv_b, w_down, norm_final):
    batch, seq, d = x_prompt.shape
    db, t_new, _ = x_sample.shape
    depth = w_ada.shape[0]
    d_ff = w_down.shape[1]
    nc = d_ff // FF_CHUNK

    pad = (-(batch + db)) % SUBLANES
    c_all = jnp.concatenate([c_prompt, c_sample, jnp.zeros((pad, d), F32)], axis=0)
    mod = _ada(c_all, w_ada, b_ada)

    def mod_part(l, k, lo, hi):
        return mod[l, lo:hi, k * d:(k + 1) * d]

    mp, ms = x_prompt.shape[0] * seq, db * t_new
    xp = x_prompt.reshape(mp, d)
    xs = x_sample.reshape(ms, d)

    tm_qkv_p, tm_sgu_p, tm_ffn_p = _tile_rows(seq, 512), _tile_rows(seq, 256), _tile_rows(seq, 512)
    tm_qkv_s, tm_sgu_s = _tile_rows(ms, 512), _tile_rows(ms, 256)

    tab_p = _rope_tables(jnp.arange(seq))
    tab_s = _rope_tables(PAST_LEN + (jnp.arange(ms) % t_new))

    new_k_p, new_v_p, new_conv_p = [], [], []
    new_k_s, new_v_s, new_conv_s, new_sgu_s = [], [], [], []

    for l in range(depth):
        idx = l // N_MIXERS
        gmix = norm_mix[l].reshape(1, d)
        gffn = norm_ffn[l].reshape(1, d)
        p_mod = [mod_part(l, k, 0, batch)[:, None, :] for k in range(6)]
        s_seq = [mod_part(l, k, batch, batch + db) for k in range(6)]
        s_row = [jnp.repeat(a, t_new, axis=0) for a in s_seq[:3]]

        if l % N_MIXERS == 0:
            wq = w_qkv[idx].astype(BF16)
            bq = b_qkv[idx].reshape(1, -1)
            wo = w_o[idx].astype(BF16)
            sink = attn_sink[idx]
            q, k, v = _qkv(xp, p_mod[1], p_mod[0], gmix, wq, bq, tab_p, tm_qkv_p, seq // tm_qkv_p,
                           seq // tm_qkv_p)
            xp = _attn_prompt(sink, q, k, v, wo, xp, p_mod[2], batch)
            keep = min(WINDOW, seq)
            new_k_p.append(k.reshape(batch, seq, N_KV_HEADS, HEAD_DIM)[:, seq - keep:])
            new_v_p.append(v.reshape(batch, seq, N_KV_HEADS, HEAD_DIM)[:, seq - keep:])

            tile3 = lambda a, tm: a.reshape(ms // tm, tm, d)
            q, k, v = _qkv(xs, tile3(s_row[1], tm_qkv_s), tile3(s_row[0], tm_qkv_s), gmix, wq, bq, tab_s,
                           tm_qkv_s, 1, ms // tm_qkv_s)
            w = cache_k.shape[2]
            xs, nk, nv = _attn_sample(sink, q, k, v, cache_k[idx].reshape(db, w, -1),
                                      cache_v[idx].reshape(db, w, -1), wo, xs, s_row[2], t_new)
            new_k_s.append(nk.reshape(db, w, N_KV_HEADS, HEAD_DIM))
            new_v_s.append(nv.reshape(db, w, N_KV_HEADS, HEAD_DIM))
        else:
            win = w_sgu_in[idx].astype(BF16)
            b_in = b_sgu_in[idx].reshape(1, -1)
            lng = sgu_ln_g[idx].reshape(1, -1)
            lnb = sgu_ln_b[idx].reshape(1, -1)
            wout = w_sgu_out[idx].astype(BF16)
            tc = min(seq, CHUNK)
            (xp,) = _sgu(xp, p_mod[1], p_mod[0], p_mod[2], gmix, win, b_in, lng, lnb,
                         w_spatial[idx][:, :tc, :tc], b_spatial[idx][:, :tc, None], wout,
                         tm_sgu_p, seq // tm_sgu_p, False)
            reps = CHUNK // t_new
            eye = jnp.eye(reps, dtype=F32)
            wsp_s = jnp.einsum("ab,gts->gatbs", eye, w_spatial[idx][:, :t_new, :t_new]).reshape(
                SGU_GROUPS, CHUNK, CHUNK)
            bsp_s = jnp.tile(b_spatial[idx][:, :t_new], (1, reps))[:, :, None]
            tile3 = lambda a: a.reshape(ms // tm_sgu_s, tm_sgu_s, d)
            xs, vrows = _sgu(xs, tile3(s_row[1]), tile3(s_row[0]), tile3(s_row[2]), gmix, win, b_in, lng, lnb,
                             wsp_s, bsp_s, wout, tm_sgu_s, 1, True)
            new_sgu_s.append(vrows.reshape(db, t_new, -1))

        wup = _chunk_cols(w_up[l].astype(BF16), nc)
        cw = _chunk_cols(conv_w[l], nc)
        cb = _chunk_cols(conv_b[l].reshape(1, -1), nc)
        wdn = w_down[l].astype(BF16).reshape(nc, FF_CHUNK, d)
        gf = norm_final.reshape(1, d) if l == depth - 1 else None
        xp, tail = _ffn_prompt(xp, p_mod[4], p_mod[3], p_mod[5], gffn, wup, cw, cb, wdn, gf, batch, tm_ffn_p)
        new_conv_p.append(_unchunk_cols(jnp.moveaxis(tail[:, :, SUBLANES - (CONV_W - 1):, :], 1, 0)))

        st = _chunk_cols(jnp.moveaxis(state_conv[l], 1, 0), nc)
        st = jnp.moveaxis(st, 0, 1)
        xs2, nst = _ffn_sample(xs.reshape(db, t_new * d), s_seq[4], s_seq[3], s_seq[5], gffn, st,
                               wup, cw, cb, wdn, gf)
        xs = xs2.reshape(ms, d)
        new_conv_s.append(jnp.moveaxis(_unchunk_cols(jnp.moveaxis(nst, 1, 0)), 0, 1))

    return (xp.reshape(batch, seq, d), xs.reshape(db, t_new, d),
            jnp.stack(new_k_p), jnp.stack(new_v_p), jnp.stack(new_conv_p),
            jnp.stack(new_k_s), jnp.stack(new_v_s), jnp.stack(new_conv_s), jnp.stack(new_sgu_s))
```

```python
import functools

import jax
import jax.numpy as jnp
from jax import lax
from jax.experimental import pallas as pl
from jax.experimental.pallas import tpu as pltpu

N_HEADS = 16
N_KV_HEADS = 4
HEAD_DIM = 64
Q_PER_KV = N_HEADS // N_KV_HEADS
WINDOW = 128
ROT_DIM = HEAD_DIM // 4
ROPE_THETA = 500000.0
CHUNK = 128
SGU_GROUPS = 4
CONV_W = 3
EPS = 1e-6
N_MIXERS = 2
PAST_LEN = 8192

LANES = 128
SUBLANES = 8
FF_CHUNK = 256
ATTN_BLOCKS_PER_STEP = 2
ATTN_SCORES_AHEAD = 3
SAMPLE_UNROLL = 4
LOG2E = 1.4426950408889634
Q_SCALE = HEAD_DIM ** -0.5 * LOG2E
VMEM_LIMIT = 56 * 1024 * 1024
NEG_BIG = -1e30

F32 = jnp.float32
BF16 = jnp.bfloat16


def _params(n_axes=1, vmem=VMEM_LIMIT):
    return pltpu.CompilerParams(dimension_semantics=("arbitrary",) * n_axes, vmem_limit_bytes=vmem)


def _resident(shape):
    nd = len(shape)
    return pl.BlockSpec(shape, lambda *_: (0,) * nd, pipeline_mode=pl.Buffered(1))


def _modnorm(x, gw, scale, shift):
    ms = jnp.mean(x * x, axis=-1, keepdims=True)
    y = x * lax.rsqrt(ms + EPS) * gw
    return y * (1.0 + scale) + shift


def _silu(x):
    return x * jax.nn.sigmoid(x)


def _gelu_tanh(x):
    c = 0.7978845608028654
    return 0.5 * x * (1.0 + jnp.tanh(c * (x + 0.044715 * (x * x * x))))


def _ada_kernel(c_ref, w_ref, b_ref, o_ref):
    s = _silu(c_ref[...]).astype(BF16)
    o_ref[0] = jnp.dot(s, w_ref[0].astype(BF16), preferred_element_type=F32) + b_ref[0]


def _ada(c_all, w_ada, b_ada):
    depth, d, n6 = w_ada.shape
    rows = c_all.shape[0]
    tn = 1024
    return pl.pallas_call(
        _ada_kernel,
        grid=(depth, n6 // tn),
        in_specs=[
            pl.BlockSpec((rows, d), lambda l, n: (0, 0)),
            pl.BlockSpec((1, d, tn), lambda l, n: (l, 0, n)),
            pl.BlockSpec((1, 1, tn), lambda l, n: (l, 0, n)),
        ],
        out_specs=pl.BlockSpec((1, rows, tn), lambda l, n: (l, 0, n)),
        out_shape=jax.ShapeDtypeStruct((depth, rows, n6), F32),
        compiler_params=_params(2),
        name="ada",
    )(c_all, w_ada, b_ada.reshape(depth, 1, n6))


def _qkv_kernel(x_ref, sc_ref, sh_ref, gw_ref, w_ref, b_ref, cos_ref, sa_ref, sb_ref,
                q_ref, k_ref, v_ref):
    h = _modnorm(x_ref[...], gw_ref[...], sc_ref[0], sh_ref[0]).astype(BF16)
    y = jnp.dot(h, w_ref[...], preferred_element_type=F32) + b_ref[...]
    cos, sa, sb = cos_ref[...], sa_ref[...], sb_ref[...]
    nq = N_HEADS * HEAD_DIM
    nk = N_KV_HEADS * HEAD_DIM

    def rope(yb):
        return yb * cos + pltpu.roll(yb, LANES - ROT_DIM // 2, 1) * sa + pltpu.roll(yb, ROT_DIM // 2, 1) * sb

    for j in range(nq // LANES):
        q_ref[:, j * LANES:(j + 1) * LANES] = rope(y[:, j * LANES:(j + 1) * LANES]) * Q_SCALE
    for j in range(nk // LANES):
        k_ref[:, j * LANES:(j + 1) * LANES] = rope(y[:, nq + j * LANES:nq + (j + 1) * LANES])
    v_ref[...] = y[:, nq + nk:]


def _rope_tables(pos):
    half = ROT_DIM // 2
    inv = ROPE_THETA ** (-jnp.arange(0, ROT_DIM, 2, dtype=F32) / ROT_DIM)
    ang = pos.astype(F32)[:, None] * inv[None, :]
    cos, sin = jnp.cos(ang), jnp.sin(ang)
    t = pos.shape[0]
    one = jnp.ones((t, HEAD_DIM - ROT_DIM), F32)
    zero = jnp.zeros((t, HEAD_DIM - ROT_DIM), F32)
    zh = jnp.zeros((t, half), F32)
    cos_t = jnp.concatenate([cos, cos, one], 1)
    sa_t = jnp.concatenate([-sin, zh, zero], 1)
    sb_t = jnp.concatenate([zh, sin, zero], 1)
    rep = LANES // HEAD_DIM
    return tuple(jnp.tile(a, (1, rep)) for a in (cos_t, sa_t, sb_t))


def _mod_spec(mod, tiles_per_group):
    return pl.BlockSpec((1,) + mod.shape[1:], lambda i: (i // tiles_per_group, 0, 0))


def _qkv(x, sc, sh, gw, w, b, tables, tm, tiles_per_group, table_tiles):
    m, d = x.shape
    n = w.shape[1]
    nq = N_HEADS * HEAD_DIM
    nk = N_KV_HEADS * HEAD_DIM
    row = lambda width: pl.BlockSpec((tm, width), lambda i: (i, 0))
    tab = pl.BlockSpec((tm, LANES), lambda i: (i % table_tiles, 0))
    return pl.pallas_call(
        _qkv_kernel,
        grid=(m // tm,),
        in_specs=[row(d), _mod_spec(sc, tiles_per_group), _mod_spec(sh, tiles_per_group),
                  _resident((1, d)), _resident((d, n)), _resident((1, n)), tab, tab, tab],
        out_specs=[row(nq), row(nk), row(nk)],
        out_shape=[jax.ShapeDtypeStruct((m, nq), F32), jax.ShapeDtypeStruct((m, nk), F32),
                   jax.ShapeDtypeStruct((m, nk), F32)],
        compiler_params=_params(1),
        name="qkv",
    )(x, sc, sh, gw, w, b, *tables)


def _dup_half(x, kv):
    lo = lax.broadcasted_iota(jnp.int32, (1, LANES), 1) < HEAD_DIM
    xr = pltpu.roll(x, HEAD_DIM, 1)
    return jnp.where(lo, x, xr) if kv % 2 == 0 else jnp.where(lo, xr, x)


def _masked_queries(q, kv):
    lo = lax.broadcasted_iota(jnp.int32, (1, LANES), 1) < HEAD_DIM
    q0 = q[:, (2 * kv) * LANES:(2 * kv + 1) * LANES]
    q1 = q[:, (2 * kv + 1) * LANES:(2 * kv + 2) * LANES]
    return jnp.concatenate([jnp.where(lo, q0, 0.0), jnp.where(lo, 0.0, q0),
                            jnp.where(lo, q1, 0.0), jnp.where(lo, 0.0, q1)], axis=0)


def _band(tq, nwin):
    i = lax.broadcasted_iota(jnp.int32, (tq, nwin), 0)
    j = lax.broadcasted_iota(jnp.int32, (tq, nwin), 1)
    diff = WINDOW + i - j
    return jnp.concatenate([(diff >= 0) & (diff <= WINDOW)] * Q_PER_KV, axis=0)


def _scores(q, kwin, kv):
    kd = _dup_half(kwin[:, (kv // 2) * LANES:(kv // 2 + 1) * LANES], kv)
    return lax.dot_general(_masked_queries(q, kv).astype(BF16), kd.astype(BF16), (((1,), (1,)), ((), ())),
                           preferred_element_type=F32)


def _softmax_pv(s, band, vwin, kv, sink_ref):
    tq = s.shape[0] // Q_PER_KV
    lo = lax.broadcasted_iota(jnp.int32, (1, LANES), 1) < HEAD_DIM
    vd = _dup_half(vwin[:, (kv // 2) * LANES:(kv // 2 + 1) * LANES], kv)
    sink = jnp.concatenate([jnp.full((tq, 1), sink_ref[Q_PER_KV * kv + g], F32)
                            for g in range(Q_PER_KV)], axis=0)
    s = jnp.where(band, s, NEG_BIG)
    mx = jnp.maximum(jnp.max(s, axis=-1, keepdims=True), sink)
    p = jnp.exp2(s - mx)
    den = jnp.sum(p, axis=-1, keepdims=True) + jnp.exp2(sink - mx)
    o = jnp.dot(p.astype(BF16), vd.astype(BF16), preferred_element_type=F32) / den
    return [jnp.where(lo, o[0:tq], o[tq:2 * tq]), jnp.where(lo, o[2 * tq:3 * tq], o[3 * tq:4 * tq])]


def _band_t(tq, nwin, prev_valid):
    j = lax.broadcasted_iota(jnp.int32, (nwin, tq), 0)
    i = lax.broadcasted_iota(jnp.int32, (nwin, tq), 1)
    diff = WINDOW + i - j
    band = (diff >= 0) & (diff <= WINDOW)
    if prev_valid is not None:
        band = band & ((j >= WINDOW) | prev_valid)
    return jnp.concatenate([band] * Q_PER_KV, axis=1)


def _scores_t(q, kwin, kv):
    kd = _dup_half(kwin[:, (kv // 2) * LANES:(kv // 2 + 1) * LANES], kv)
    return lax.dot_general(kd.astype(BF16), _masked_queries(q, kv).astype(BF16), (((1,), (1,)), ((), ())),
                           preferred_element_type=F32)


def _softmax_pv_t(s, band, vwin, kv, sink_ref):
    tq = s.shape[1] // Q_PER_KV
    nwin = s.shape[0]
    blk = kv // 2
    vt = vwin[:, blk * LANES:(blk + 1) * LANES].T
    vt = jnp.concatenate([vt[(kv % 2) * HEAD_DIM:(kv % 2 + 1) * HEAD_DIM],
                          jnp.ones((2 * SUBLANES, nwin), F32)], axis=0)
    sink = jnp.concatenate([jnp.full((1, tq), sink_ref[Q_PER_KV * kv + g], F32)
                            for g in range(Q_PER_KV)], axis=1)
    s = jnp.where(band, s, NEG_BIG)
    mx = jnp.maximum(jnp.max(s, axis=0, keepdims=True), sink)
    p = jnp.exp2(s - mx)
    o = jnp.dot(vt.astype(BF16), p.astype(BF16), preferred_element_type=F32)
    den = o[HEAD_DIM:HEAD_DIM + 1] + jnp.exp2(sink - mx)
    o = o[:HEAD_DIM] / den
    return [o[:, g * tq:(g + 1) * tq] for g in range(Q_PER_KV)]


def _attn_prompt_kernel(sink_ref, q_ref, ko_ref, kp_ref, vo_ref, vp_ref, wo_ref, x_ref, g1_ref, o_ref):
    nblk = q_ref.shape[0] // WINDOW
    kall = jnp.concatenate([kp_ref[...], ko_ref[...]], axis=0)
    vall = jnp.concatenate([vp_ref[...], vo_ref[...]], axis=0)
    band_first = _band_t(WINDOW, 2 * WINDOW, pl.program_id(1) > 0)
    band_rest = _band_t(WINDOW, 2 * WINDOW, None) if nblk > 1 else None
    heads = {c: [] for c in range(nblk)}

    def finish(c, kv, s):
        win = slice(c * WINDOW, (c + 2) * WINDOW)
        heads[c].extend(_softmax_pv_t(s, band_first if c == 0 else band_rest, vall[win], kv, sink_ref))
        if kv == N_KV_HEADS - 1:
            rows = slice(c * WINDOW, (c + 1) * WINDOW)
            ot = jnp.concatenate(heads[c], axis=0)
            mix = lax.dot_general(ot.astype(BF16), wo_ref[...], (((0,), (0,)), ((), ())),
                                  preferred_element_type=F32)
            o_ref[rows, :] = x_ref[rows, :] + g1_ref[0] * mix

    pending = []
    for c in range(nblk):
        for kv in range(N_KV_HEADS):
            s = _scores_t(q_ref[c * WINDOW:(c + 1) * WINDOW, :], kall[c * WINDOW:(c + 2) * WINDOW], kv)
            pending.append((c, kv, s))
            if len(pending) > ATTN_SCORES_AHEAD:
                finish(*pending.pop(0))
    for unit in pending:
        finish(*unit)


def _attn_prompt(sink, q, k, v, wo, x, g1, batch, nblk):
    m, d = x.shape
    tq = nblk * WINDOW
    nb = m // batch // tq
    nk = k.shape[1]
    own = lambda width: pl.BlockSpec((tq, width), lambda b, n: (b * nb + n, 0))
    prev = lambda width: pl.BlockSpec(
        (WINDOW, width), lambda b, n: (b * nb * nblk + jnp.maximum(n * nblk - 1, 0), 0))
    return pl.pallas_call(
        _attn_prompt_kernel,
        grid=(batch, nb),
        in_specs=[pl.BlockSpec(memory_space=pltpu.SMEM), own(q.shape[1]), own(nk), prev(nk), own(nk), prev(nk),
                  pl.BlockSpec(wo.shape, lambda b, n: (0, 0), pipeline_mode=pl.Buffered(1)),
                  own(d), pl.BlockSpec((1, 1, d), lambda b, n: (b, 0, 0))],
        out_specs=own(d),
        out_shape=jax.ShapeDtypeStruct((m, d), F32),
        compiler_params=_params(2),
        name="attn_prompt",
    )(sink, q, k, k, v, v, wo, x, g1)


def _attn_sample_kernel(sink_ref, q_ref, kn_ref, vn_ref, kc_ref, vc_ref, wo_ref, x_ref, g1_ref,
                        o_ref, ko_ref, vo_ref, o_scr, *, t_new):
    nseq = kc_ref.shape[0]
    w = kc_ref.shape[1]
    pad = jnp.zeros((w - t_new, kc_ref.shape[2]), F32)
    band = _band(t_new, 2 * w)

    def body(it, carry):
        units = []
        for u in range(SAMPLE_UNROLL):
            b = it * SAMPLE_UNROLL + u
            r = pl.multiple_of(b * t_new, t_new)
            kn, vn = kn_ref[pl.ds(r, t_new), :], vn_ref[pl.ds(r, t_new), :]
            kc, vc = kc_ref[b], vc_ref[b]
            ko_ref[b, 0:w - t_new, :] = kc[t_new:]
            ko_ref[b, w - t_new:w, :] = kn
            vo_ref[b, 0:w - t_new, :] = vc[t_new:]
            vo_ref[b, w - t_new:w, :] = vn
            kwin = jnp.concatenate([kc, kn, pad], axis=0)
            vwin = jnp.concatenate([vc, vn, pad], axis=0)
            q = q_ref[pl.ds(r, t_new), :]
            units.append((r, vwin, [_scores(q, kwin, kv) for kv in range(N_KV_HEADS)]))
        for r, vwin, scores in units:
            cols = []
            for kv in range(N_KV_HEADS):
                cols.extend(_softmax_pv(scores[kv], band, vwin, kv, sink_ref))
            o_scr[pl.ds(r, t_new), :] = jnp.concatenate(cols, axis=1)
        return carry

    lax.fori_loop(0, nseq // SAMPLE_UNROLL, body, 0)
    mix = jnp.dot(o_scr[...].astype(BF16), wo_ref[...], preferred_element_type=F32)
    o_ref[...] = x_ref[...] + g1_ref[...] * mix


def _attn_sample(sink, q, kn, vn, kc, vc, wo, x, g1, t_new):
    m, d = x.shape
    db, w, nk = kc.shape
    g = WINDOW // t_new
    tm = g * t_new
    row = lambda width: pl.BlockSpec((tm, width), lambda i: (i, 0))
    cache = pl.BlockSpec((g, w, nk), lambda i: (i, 0, 0))
    return pl.pallas_call(
        functools.partial(_attn_sample_kernel, t_new=t_new),
        grid=(db // g,),
        in_specs=[pl.BlockSpec(memory_space=pltpu.SMEM), row(q.shape[1]), row(nk), row(nk), cache, cache,
                  _resident(wo.shape), row(d), row(d)],
        out_specs=[row(d), cache, cache],
        out_shape=[jax.ShapeDtypeStruct((m, d), F32), jax.ShapeDtypeStruct(kc.shape, F32),
                   jax.ShapeDtypeStruct(vc.shape, F32)],
        scratch_shapes=[pltpu.VMEM((tm, q.shape[1]), F32)],
        compiler_params=_params(1),
        name="attn_sample",
    )(sink, q, kn, vn, kc, vc, wo, x, g1)


def _sgu_kernel(x_ref, sc_ref, sh_ref, g1_ref, gw_ref, win_ref, bin_ref, lng_ref, lnb_ref,
                wsp_ref, bsp_ref, wout_ref, o_ref, *v_out):
    x = x_ref[...]
    tm = x.shape[0]
    d_sgu = lng_ref.shape[1]
    gdim = d_sgu // SGU_GROUPS
    h = _modnorm(x, gw_ref[...], sc_ref[0], sh_ref[0]).astype(BF16)
    z = _gelu_tanh(jnp.dot(h, win_ref[...], preferred_element_type=F32) + bin_ref[...])
    u, v = z[:, :d_sgu], z[:, d_sgu:]
    mu = jnp.mean(v, axis=-1, keepdims=True)
    vc = v - mu
    var = jnp.mean(vc * vc, axis=-1, keepdims=True)
    vn = vc * lax.rsqrt(var + EPS) * lng_ref[...] + lnb_ref[...]
    if v_out:
        v_out[0][...] = vn
    vb = vn.astype(BF16)
    r = lax.broadcasted_iota(jnp.int32, (CHUNK, CHUNK), 0)
    c = lax.broadcasted_iota(jnp.int32, (CHUNK, CHUNK), 1)
    wm = [jnp.where(r >= c, wsp_ref[g], 0.0).astype(BF16) for g in range(SGU_GROUPS)]
    rows = []
    for ch in range(tm // CHUNK):
        cols = [jnp.dot(wm[g], vb[ch * CHUNK:(ch + 1) * CHUNK, g * gdim:(g + 1) * gdim],
                        preferred_element_type=F32) + bsp_ref[g]
                for g in range(SGU_GROUPS)]
        rows.append(jnp.concatenate(cols, axis=1))
    mixed = jnp.concatenate(rows, axis=0)
    out = jnp.dot((u * mixed).astype(BF16), wout_ref[...], preferred_element_type=F32)
    o_ref[...] = x + g1_ref[0] * out


def _sgu(x, sc, sh, g1, gw, win, b_in, lng, lnb, wsp, bsp, wout, tm, tiles_per_group, emit_v):
    m, d = x.shape
    d_sgu = wout.shape[0]
    row = lambda width: pl.BlockSpec((tm, width), lambda i: (i, 0))
    out_specs = [row(d)]
    out_shape = [jax.ShapeDtypeStruct((m, d), F32)]
    if emit_v:
        out_specs.append(row(d_sgu))
        out_shape.append(jax.ShapeDtypeStruct((m, d_sgu), F32))
    return pl.pallas_call(
        _sgu_kernel,
        grid=(m // tm,),
        in_specs=[row(d), _mod_spec(sc, tiles_per_group), _mod_spec(sh, tiles_per_group),
                  _mod_spec(g1, tiles_per_group), _resident((1, d)), _resident(win.shape),
                  _resident(b_in.shape), _resident(lng.shape), _resident(lnb.shape),
                  _resident(wsp.shape), _resident(bsp.shape), _resident(wout.shape)],
        out_specs=out_specs,
        out_shape=out_shape,
        compiler_params=_params(1),
        name="sgu",
    )(x, sc, sh, g1, gw, win, b_in, lng, lnb, wsp, bsp, wout)


def _ffn_chunks(d_ff):
    return [(c, d_ff + c) for c in range(0, d_ff, FF_CHUNK)]


def _conv3(a, a1, a2, cw_ref, cb_ref, c0):
    cols = slice(c0, c0 + FF_CHUNK)
    return cw_ref[0:1, cols] * a2 + cw_ref[1:2, cols] * a1 + cw_ref[2:3, cols] * a + cb_ref[:, cols]


def _gate_down(conv_g, conv_u, wdn_ref, c0):
    act = (_silu(conv_g) * conv_u).astype(BF16)
    return jnp.dot(act, wdn_ref[c0:c0 + FF_CHUNK, :], preferred_element_type=F32)


def _final(xn, gf_ref):
    if gf_ref is None:
        return xn
    ms = jnp.mean(xn * xn, axis=-1, keepdims=True)
    return xn * lax.rsqrt(ms + EPS) * gf_ref[...]


def _ffn_prompt_kernel(x_ref, sc_ref, sh_ref, g2_ref, gw_ref, wup_ref, cw_ref, cb_ref, wdn_ref, *rest,
                       tiles_per_seq, final):
    gf_ref = rest[0] if final else None
    o_ref, tail_ref, a_scr = rest[-3:]
    d_ff = wdn_ref.shape[0]
    x = x_ref[...]
    tm = x.shape[0]
    halo = SUBLANES

    @pl.when(pl.program_id(0) % tiles_per_seq == 0)
    def _():
        a_scr[0:halo, :] = jnp.zeros((halo, a_scr.shape[1]), F32)

    h = _modnorm(x, gw_ref[...], sc_ref[0], sh_ref[0]).astype(BF16)

    def conv_cols(c0):
        cols = slice(c0, c0 + FF_CHUNK)
        a = jnp.dot(h, wup_ref[:, cols], preferred_element_type=F32)
        a_scr[halo:halo + tm, cols] = a
        return _conv3(a, a_scr[halo - 1:halo - 1 + tm, cols], a_scr[halo - 2:halo - 2 + tm, cols],
                      cw_ref, cb_ref, c0)

    acc = None
    for cg, cu in _ffn_chunks(d_ff):
        dn = _gate_down(conv_cols(cg), conv_cols(cu), wdn_ref, cg)
        acc = dn if acc is None else acc + dn
    tail = a_scr[tm:tm + halo, :]
    a_scr[0:halo, :] = tail
    tail_ref[0] = tail
    o_ref[...] = _final(x + g2_ref[0] * acc, gf_ref)


def _ffn_prompt(x, sc, sh, g2, gw, wup, cw, cb, wdn, gf, batch, tm):
    m, d = x.shape
    c2 = wup.shape[1]
    tiles_per_seq = m // batch // tm
    row = pl.BlockSpec((tm, d), lambda i: (i, 0))
    final = gf is not None
    in_specs = [row, _mod_spec(sc, tiles_per_seq), _mod_spec(sh, tiles_per_seq), _mod_spec(g2, tiles_per_seq),
                _resident((1, d)), _resident(wup.shape), _resident(cw.shape), _resident(cb.shape),
                _resident(wdn.shape)]
    args = [x, sc, sh, g2, gw, wup, cw, cb, wdn]
    if final:
        in_specs.append(_resident((1, d)))
        args.append(gf)
    return pl.pallas_call(
        functools.partial(_ffn_prompt_kernel, tiles_per_seq=tiles_per_seq, final=final),
        grid=(m // tm,),
        in_specs=in_specs,
        out_specs=[row, pl.BlockSpec((1, SUBLANES, c2), lambda i: (i // tiles_per_seq, 0, 0))],
        out_shape=[jax.ShapeDtypeStruct((m, d), F32), jax.ShapeDtypeStruct((batch, SUBLANES, c2), F32)],
        scratch_shapes=[pltpu.VMEM((tm + SUBLANES, c2), F32)],
        compiler_params=_params(1),
        name="ffn_prompt",
    )(*args)


def _ffn_sample_kernel(x_ref, sc_ref, sh_ref, g2_ref, gw_ref, st0_ref, st1_ref, wup_ref, cw_ref, cb_ref,
                       wdn_ref, *rest, final):
    gf_ref = rest[0] if final else None
    o_ref, nst_ref, ring = rest[-3:]
    d_ff = wdn_ref.shape[0]
    t = pl.program_id(0)

    @pl.when(t == 0)
    def _():
        ring[1] = st0_ref[...]
        ring[2] = st1_ref[...]

    s0, s1, s2 = t % CONV_W, (t + 2) % CONV_W, (t + 1) % CONV_W
    x = x_ref[...]
    h = _modnorm(x, gw_ref[...], sc_ref[...], sh_ref[...]).astype(BF16)

    def conv_cols(c0):
        cols = slice(c0, c0 + FF_CHUNK)
        a = jnp.dot(h, wup_ref[:, cols], preferred_element_type=F32)
        ring[s0, :, cols] = a
        nst_ref[:, cols] = a
        return _conv3(a, ring[s1, :, cols], ring[s2, :, cols], cw_ref, cb_ref, c0)

    acc = None
    for cg, cu in _ffn_chunks(d_ff):
        dn = _gate_down(conv_cols(cg), conv_cols(cu), wdn_ref, cg)
        acc = dn if acc is None else acc + dn
    o_ref[...] = _final(x + g2_ref[...] * acc, gf_ref)


def _ffn_sample(x2, sc, sh, g2, gw, st2, wup, cw, cb, wdn, gf):
    db, d = sc.shape
    t_new = x2.shape[1] // d
    c2 = wup.shape[1]
    keep = CONV_W - 1
    step = pl.BlockSpec((db, d), lambda t: (0, t))
    final = gf is not None
    in_specs = [step, _resident((db, d)), _resident((db, d)), _resident((db, d)), _resident((1, d)),
                pl.BlockSpec((db, c2), lambda t: (0, 0), pipeline_mode=pl.Buffered(1)),
                pl.BlockSpec((db, c2), lambda t: (0, 1), pipeline_mode=pl.Buffered(1)),
                _resident(wup.shape), _resident(cw.shape), _resident(cb.shape), _resident(wdn.shape)]
    args = [x2, sc, sh, g2, gw, st2, st2, wup, cw, cb, wdn]
    if final:
        in_specs.append(_resident((1, d)))
        args.append(gf)
    return pl.pallas_call(
        functools.partial(_ffn_sample_kernel, final=final),
        grid=(t_new,),
        in_specs=in_specs,
        out_specs=[step, pl.BlockSpec((db, c2), lambda t: (0, jnp.maximum(t - (t_new - keep), 0)))],
        out_shape=[jax.ShapeDtypeStruct((db, t_new * d), F32), jax.ShapeDtypeStruct((db, keep * c2), F32)],
        scratch_shapes=[pltpu.VMEM((CONV_W, db, c2), F32)],
        compiler_params=_params(1),
        name="ffn_sample",
    )(*args)


def _tile_rows(size, want):
    return want if size % want == 0 else size


def kernel(x_prompt, x_sample, c_prompt, c_sample, cache_k, cache_v, state_conv, w_ada, b_ada, norm_mix,
           norm_ffn, w_qkv, b_qkv, attn_sink, w_o, w_sgu_in, b_sgu_in, sgu_ln_g, sgu_ln_b, w_spatial,
           b_spatial, w_sgu_out, w_up, conv_w, conv_b, w_down, norm_final):
    batch, seq, d = x_prompt.shape
    db, t_new, _ = x_sample.shape
    depth = w_ada.shape[0]
    d_ff = w_down.shape[1]
    nc = d_ff // FF_CHUNK

    pad = (-(batch + db)) % SUBLANES
    c_all = jnp.concatenate([c_prompt, c_sample, jnp.zeros((pad, d), F32)], axis=0)
    mod = _ada(c_all, w_ada, b_ada)

    def mod_part(l, k, lo, hi):
        return mod[l, lo:hi, k * d:(k + 1) * d]

    mp, ms = x_prompt.shape[0] * seq, db * t_new
    xp = x_prompt.reshape(mp, d)
    xs = x_sample.reshape(ms, d)

    tm_qkv_p, tm_sgu_p, tm_ffn_p = _tile_rows(seq, 512), _tile_rows(seq, 256), _tile_rows(seq, 512)
    tm_qkv_s, tm_sgu_s = _tile_rows(ms, 512), _tile_rows(ms, 256)

    tab_p = _rope_tables(jnp.arange(seq))
    tab_s = _rope_tables(PAST_LEN + (jnp.arange(ms) % t_new))

    new_k_p, new_v_p, new_conv_p = [], [], []
    new_k_s, new_v_s, new_conv_s, new_sgu_s = [], [], [], []

    for l in range(depth):
        idx = l // N_MIXERS
        gmix = norm_mix[l].reshape(1, d)
        gffn = norm_ffn[l].reshape(1, d)
        p_mod = [mod_part(l, k, 0, batch)[:, None, :] for k in range(6)]
        s_seq = [mod_part(l, k, batch, batch + db) for k in range(6)]
        s_row = [jnp.repeat(a, t_new, axis=0) for a in s_seq[:3]]

        if l % N_MIXERS == 0:
            wq = w_qkv[idx].astype(BF16)
            bq = b_qkv[idx].reshape(1, -1)
            wo = w_o[idx].astype(BF16)
            sink = attn_sink[idx] * LOG2E
            q, k, v = _qkv(xp, p_mod[1], p_mod[0], gmix, wq, bq, tab_p, tm_qkv_p, seq // tm_qkv_p,
                           seq // tm_qkv_p)
            nblk = ATTN_BLOCKS_PER_STEP if seq % (ATTN_BLOCKS_PER_STEP * WINDOW) == 0 else 1
            xp = _attn_prompt(sink, q, k, v, wo, xp, p_mod[2], batch, nblk)
            keep = min(WINDOW, seq)
            new_k_p.append(k.reshape(batch, seq, N_KV_HEADS, HEAD_DIM)[:, seq - keep:])
            new_v_p.append(v.reshape(batch, seq, N_KV_HEADS, HEAD_DIM)[:, seq - keep:])

            tile3 = lambda a, tm: a.reshape(ms // tm, tm, d)
            q, k, v = _qkv(xs, tile3(s_row[1], tm_qkv_s), tile3(s_row[0], tm_qkv_s), gmix, wq, bq, tab_s,
                           tm_qkv_s, 1, ms // tm_qkv_s)
            w = cache_k.shape[2]
            xs, nk, nv = _attn_sample(sink, q, k, v, cache_k[idx].reshape(db, w, -1),
                                      cache_v[idx].reshape(db, w, -1), wo, xs, s_row[2], t_new)
            new_k_s.append(nk.reshape(db, w, N_KV_HEADS, HEAD_DIM))
            new_v_s.append(nv.reshape(db, w, N_KV_HEADS, HEAD_DIM))
        else:
            win = w_sgu_in[idx].astype(BF16)
            b_in = b_sgu_in[idx].reshape(1, -1)
            lng = sgu_ln_g[idx].reshape(1, -1)
            lnb = sgu_ln_b[idx].reshape(1, -1)
            wout = w_sgu_out[idx].astype(BF16)
            tc = min(seq, CHUNK)
            (xp,) = _sgu(xp, p_mod[1], p_mod[0], p_mod[2], gmix, win, b_in, lng, lnb,
                         w_spatial[idx][:, :tc, :tc], b_spatial[idx][:, :tc, None], wout,
                         tm_sgu_p, seq // tm_sgu_p, False)
            reps = CHUNK // t_new
            eye = jnp.eye(reps, dtype=F32)
            wsp_s = jnp.einsum("ab,gts->gatbs", eye, w_spatial[idx][:, :t_new, :t_new]).reshape(
                SGU_GROUPS, CHUNK, CHUNK)
            bsp_s = jnp.tile(b_spatial[idx][:, :t_new], (1, reps))[:, :, None]
            tile3 = lambda a: a.reshape(ms // tm_sgu_s, tm_sgu_s, d)
            xs, vrows = _sgu(xs, tile3(s_row[1]), tile3(s_row[0]), tile3(s_row[2]), gmix, win, b_in, lng, lnb,
                             wsp_s, bsp_s, wout, tm_sgu_s, 1, True)
            new_sgu_s.append(vrows.reshape(db, t_new, -1))

        wup = w_up[l].astype(BF16)
        wdn = w_down[l].astype(BF16)
        cw = conv_w[l]
        cb = conv_b[l].reshape(1, -1)
        gf = norm_final.reshape(1, d) if l == depth - 1 else None
        xp, tail = _ffn_prompt(xp, p_mod[4], p_mod[3], p_mod[5], gffn, wup, cw, cb, wdn, gf, batch, tm_ffn_p)
        new_conv_p.append(tail[:, SUBLANES - (CONV_W - 1):, :])

        xs2, nst = _ffn_sample(xs.reshape(db, t_new * d), s_seq[4], s_seq[3], s_seq[5], gffn,
                               state_conv[l].reshape(db, -1), wup, cw, cb, wdn, gf)
        xs = xs2.reshape(ms, d)
        new_conv_s.append(nst.reshape(db, CONV_W - 1, -1))

    return (xp.reshape(batch, seq, d), xs.reshape(db, t_new, d),
            jnp.stack(new_k_p), jnp.stack(new_v_p), jnp.stack(new_conv_p),
            jnp.stack(new_k_s), jnp.stack(new_v_s), jnp.stack(new_conv_s), jnp.stack(new_sgu_s))
```

```python
import functools

import jax
import jax.numpy as jnp
from jax import lax
from jax.experimental import pallas as pl
from jax.experimental.pallas import tpu as pltpu

N_HEADS = 16
N_KV_HEADS = 4
HEAD_DIM = 64
Q_PER_KV = N_HEADS // N_KV_HEADS
WINDOW = 128
ROT_DIM = HEAD_DIM // 4
ROPE_THETA = 500000.0
CHUNK = 128
SGU_GROUPS = 4
CONV_W = 3
EPS = 1e-6
N_MIXERS = 2
PAST_LEN = 8192

LANES = 128
SUBLANES = 8
FF_CHUNK = 256
ATTN_BLOCKS_PER_STEP = 2
FF_ROWS = 64
FFN_UP_AHEAD = 3
FFN_DOWN_GROUP = 4
ATTN_SCORES_AHEAD = 3
SAMPLE_UNROLL = 4
LOG2E = 1.4426950408889634
Q_SCALE = HEAD_DIM ** -0.5 * LOG2E
VMEM_LIMIT = 56 * 1024 * 1024
NEG_BIG = -1e30

F32 = jnp.float32
BF16 = jnp.bfloat16


def _params(n_axes=1, vmem=VMEM_LIMIT):
    return pltpu.CompilerParams(dimension_semantics=("arbitrary",) * n_axes, vmem_limit_bytes=vmem)


def _resident(shape):
    nd = len(shape)
    return pl.BlockSpec(shape, lambda *_: (0,) * nd, pipeline_mode=pl.Buffered(1))


def _layer_spec(stacked, l):
    nd = stacked.ndim - 1
    return pl.BlockSpec((None,) + stacked.shape[1:], lambda *_: (l,) + (0,) * nd, pipeline_mode=pl.Buffered(1))


def _modnorm(x, gw, scale, shift):
    ms = jnp.mean(x * x, axis=-1, keepdims=True)
    y = x * lax.rsqrt(ms + EPS) * gw
    return y * (1.0 + scale) + shift


def _silu(x):
    return x * jax.nn.sigmoid(x)


def _gelu_tanh(x):
    c = 0.7978845608028654
    return 0.5 * x * (1.0 + jnp.tanh(c * (x + 0.044715 * (x * x * x))))


def _ada_kernel(c_ref, w_ref, b_ref, o_ref):
    s = _silu(c_ref[...]).astype(BF16)
    o_ref[0] = jnp.dot(s, w_ref[0].astype(BF16), preferred_element_type=F32) + b_ref[0]


def _ada(c_all, w_ada, b_ada):
    depth, d, n6 = w_ada.shape
    rows = c_all.shape[0]
    tn = 1024
    return pl.pallas_call(
        _ada_kernel,
        grid=(depth, n6 // tn),
        in_specs=[
            pl.BlockSpec((rows, d), lambda l, n: (0, 0)),
            pl.BlockSpec((1, d, tn), lambda l, n: (l, 0, n)),
            pl.BlockSpec((1, 1, tn), lambda l, n: (l, 0, n)),
        ],
        out_specs=pl.BlockSpec((1, rows, tn), lambda l, n: (l, 0, n)),
        out_shape=jax.ShapeDtypeStruct((depth, rows, n6), F32),
        compiler_params=_params(2),
        name="ada",
    )(c_all, w_ada, b_ada.reshape(depth, 1, n6))


def _qkv_kernel(x_ref, sc_ref, sh_ref, gw_ref, w_ref, b_ref, cos_ref, sa_ref, sb_ref,
                q_ref, k_ref, v_ref):
    h = _modnorm(x_ref[...], gw_ref[...], sc_ref[0], sh_ref[0]).astype(BF16)
    y = jnp.dot(h, w_ref[...], preferred_element_type=F32) + b_ref[...]
    cos, sa, sb = cos_ref[...], sa_ref[...], sb_ref[...]
    nq = N_HEADS * HEAD_DIM
    nk = N_KV_HEADS * HEAD_DIM

    def rope(yb):
        return yb * cos + pltpu.roll(yb, LANES - ROT_DIM // 2, 1) * sa + pltpu.roll(yb, ROT_DIM // 2, 1) * sb

    for j in range(nq // LANES):
        q_ref[:, j * LANES:(j + 1) * LANES] = rope(y[:, j * LANES:(j + 1) * LANES]) * Q_SCALE
    for j in range(nk // LANES):
        k_ref[:, j * LANES:(j + 1) * LANES] = rope(y[:, nq + j * LANES:nq + (j + 1) * LANES])
    v_ref[...] = y[:, nq + nk:]


def _rope_tables(pos):
    half = ROT_DIM // 2
    inv = ROPE_THETA ** (-jnp.arange(0, ROT_DIM, 2, dtype=F32) / ROT_DIM)
    ang = pos.astype(F32)[:, None] * inv[None, :]
    cos, sin = jnp.cos(ang), jnp.sin(ang)
    t = pos.shape[0]
    one = jnp.ones((t, HEAD_DIM - ROT_DIM), F32)
    zero = jnp.zeros((t, HEAD_DIM - ROT_DIM), F32)
    zh = jnp.zeros((t, half), F32)
    cos_t = jnp.concatenate([cos, cos, one], 1)
    sa_t = jnp.concatenate([-sin, zh, zero], 1)
    sb_t = jnp.concatenate([zh, sin, zero], 1)
    rep = LANES // HEAD_DIM
    return tuple(jnp.tile(a, (1, rep)) for a in (cos_t, sa_t, sb_t))


def _mod_spec(mod, tiles_per_group):
    return pl.BlockSpec((1,) + mod.shape[1:], lambda i: (i // tiles_per_group, 0, 0))


def _qkv(x, sc, sh, gw, w, l, b, tables, tm, tiles_per_group, table_tiles):
    m, d = x.shape
    n = w.shape[-1]
    nq = N_HEADS * HEAD_DIM
    nk = N_KV_HEADS * HEAD_DIM
    row = lambda width: pl.BlockSpec((tm, width), lambda i: (i, 0))
    tab = pl.BlockSpec((tm, LANES), lambda i: (i % table_tiles, 0))
    return pl.pallas_call(
        _qkv_kernel,
        grid=(m // tm,),
        in_specs=[row(d), _mod_spec(sc, tiles_per_group), _mod_spec(sh, tiles_per_group),
                  _resident((1, d)), _layer_spec(w, l), _resident((1, n)), tab, tab, tab],
        out_specs=[row(nq), row(nk), row(nk)],
        out_shape=[jax.ShapeDtypeStruct((m, nq), F32), jax.ShapeDtypeStruct((m, nk), F32),
                   jax.ShapeDtypeStruct((m, nk), F32)],
        compiler_params=_params(1),
        name="qkv",
    )(x, sc, sh, gw, w, b, *tables)


def _dup_half(x, kv):
    lo = lax.broadcasted_iota(jnp.int32, (1, LANES), 1) < HEAD_DIM
    xr = pltpu.roll(x, HEAD_DIM, 1)
    return jnp.where(lo, x, xr) if kv % 2 == 0 else jnp.where(lo, xr, x)


def _masked_queries(q, kv):
    lo = lax.broadcasted_iota(jnp.int32, (1, LANES), 1) < HEAD_DIM
    q0 = q[:, (2 * kv) * LANES:(2 * kv + 1) * LANES]
    q1 = q[:, (2 * kv + 1) * LANES:(2 * kv + 2) * LANES]
    return jnp.concatenate([jnp.where(lo, q0, 0.0), jnp.where(lo, 0.0, q0),
                            jnp.where(lo, q1, 0.0), jnp.where(lo, 0.0, q1)], axis=0)


def _band(tq, nwin):
    i = lax.broadcasted_iota(jnp.int32, (tq, nwin), 0)
    j = lax.broadcasted_iota(jnp.int32, (tq, nwin), 1)
    diff = WINDOW + i - j
    return jnp.concatenate([(diff >= 0) & (diff <= WINDOW)] * Q_PER_KV, axis=0)


def _scores(q, kwin, kv):
    kd = _dup_half(kwin[:, (kv // 2) * LANES:(kv // 2 + 1) * LANES], kv)
    return lax.dot_general(_masked_queries(q, kv).astype(BF16), kd.astype(BF16), (((1,), (1,)), ((), ())),
                           preferred_element_type=F32)


def _softmax_pv(s, band, vwin, kv, sink_ref):
    tq = s.shape[0] // Q_PER_KV
    lo = lax.broadcasted_iota(jnp.int32, (1, LANES), 1) < HEAD_DIM
    vd = _dup_half(vwin[:, (kv // 2) * LANES:(kv // 2 + 1) * LANES], kv)
    sink = jnp.concatenate([jnp.full((tq, 1), sink_ref[Q_PER_KV * kv + g], F32)
                            for g in range(Q_PER_KV)], axis=0)
    s = jnp.where(band, s, NEG_BIG)
    mx = jnp.maximum(jnp.max(s, axis=-1, keepdims=True), sink)
    p = jnp.exp2(s - mx)
    den = jnp.sum(p, axis=-1, keepdims=True) + jnp.exp2(sink - mx)
    o = jnp.dot(p.astype(BF16), vd.astype(BF16), preferred_element_type=F32) / den
    return [jnp.where(lo, o[0:tq], o[tq:2 * tq]), jnp.where(lo, o[2 * tq:3 * tq], o[3 * tq:4 * tq])]


def _band_t(tq, nwin, prev_valid):
    j = lax.broadcasted_iota(jnp.int32, (nwin, tq), 0)
    i = lax.broadcasted_iota(jnp.int32, (nwin, tq), 1)
    diff = WINDOW + i - j
    band = (diff >= 0) & (diff <= WINDOW)
    if prev_valid is not None:
        band = band & ((j >= WINDOW) | prev_valid)
    return jnp.concatenate([band] * Q_PER_KV, axis=1)


def _scores_t(q, kwin, kv):
    kd = _dup_half(kwin[:, (kv // 2) * LANES:(kv // 2 + 1) * LANES], kv)
    return lax.dot_general(kd.astype(BF16), _masked_queries(q, kv).astype(BF16), (((1,), (1,)), ((), ())),
                           preferred_element_type=F32)


def _softmax_pv_t(s, band, vwin, kv, sink_ref):
    tq = s.shape[1] // Q_PER_KV
    nwin = s.shape[0]
    blk = kv // 2
    vt = vwin[:, blk * LANES:(blk + 1) * LANES].T
    vt = jnp.concatenate([vt[(kv % 2) * HEAD_DIM:(kv % 2 + 1) * HEAD_DIM],
                          jnp.ones((2 * SUBLANES, nwin), F32)], axis=0)
    sink = jnp.concatenate([jnp.full((1, tq), sink_ref[Q_PER_KV * kv + g], F32)
                            for g in range(Q_PER_KV)], axis=1)
    s = jnp.where(band, s, NEG_BIG)
    mx = jnp.maximum(jnp.max(s, axis=0, keepdims=True), sink)
    p = jnp.exp2(s - mx)
    o = jnp.dot(vt.astype(BF16), p.astype(BF16), preferred_element_type=F32)
    den = o[HEAD_DIM:HEAD_DIM + 1] + jnp.exp2(sink - mx)
    o = o[:HEAD_DIM] / den
    return [o[:, g * tq:(g + 1) * tq] for g in range(Q_PER_KV)]


def _attn_prompt_kernel(sink_ref, q_ref, ko_ref, kp_ref, vo_ref, vp_ref, wo_ref, x_ref, g1_ref, o_ref):
    nblk = q_ref.shape[0] // WINDOW
    kall = jnp.concatenate([kp_ref[...], ko_ref[...]], axis=0)
    vall = jnp.concatenate([vp_ref[...], vo_ref[...]], axis=0)
    band_first = _band_t(WINDOW, 2 * WINDOW, pl.program_id(1) > 0)
    band_rest = _band_t(WINDOW, 2 * WINDOW, None) if nblk > 1 else None
    heads = {c: [] for c in range(nblk)}

    def finish(c, kv, s):
        win = slice(c * WINDOW, (c + 2) * WINDOW)
        heads[c].extend(_softmax_pv_t(s, band_first if c == 0 else band_rest, vall[win], kv, sink_ref))
        if kv == N_KV_HEADS - 1:
            rows = slice(c * WINDOW, (c + 1) * WINDOW)
            ot = jnp.concatenate(heads[c], axis=0)
            mix = lax.dot_general(ot.astype(BF16), wo_ref[...], (((0,), (0,)), ((), ())),
                                  preferred_element_type=F32)
            o_ref[rows, :] = x_ref[rows, :] + g1_ref[0] * mix

    pending = []
    for c in range(nblk):
        for kv in range(N_KV_HEADS):
            s = _scores_t(q_ref[c * WINDOW:(c + 1) * WINDOW, :], kall[c * WINDOW:(c + 2) * WINDOW], kv)
            pending.append((c, kv, s))
            if len(pending) > ATTN_SCORES_AHEAD:
                finish(*pending.pop(0))
    for unit in pending:
        finish(*unit)


def _attn_prompt(sink, q, k, v, wo, l, x, g1, batch, nblk):
    m, d = x.shape
    tq = nblk * WINDOW
    nb = m // batch // tq
    nk = k.shape[1]
    own = lambda width: pl.BlockSpec((tq, width), lambda b, n: (b * nb + n, 0))
    prev = lambda width: pl.BlockSpec(
        (WINDOW, width), lambda b, n: (b * nb * nblk + jnp.maximum(n * nblk - 1, 0), 0))
    return pl.pallas_call(
        _attn_prompt_kernel,
        grid=(batch, nb),
        in_specs=[pl.BlockSpec(memory_space=pltpu.SMEM), own(q.shape[1]), own(nk), prev(nk), own(nk), prev(nk),
                  _layer_spec(wo, l),
                  own(d), pl.BlockSpec((1, 1, d), lambda b, n: (b, 0, 0))],
        out_specs=own(d),
        out_shape=jax.ShapeDtypeStruct((m, d), F32),
        compiler_params=_params(2),
        name="attn_prompt",
    )(sink, q, k, k, v, v, wo, x, g1)


def _attn_sample_kernel(sink_ref, q_ref, kn_ref, vn_ref, kc_ref, vc_ref, wo_ref, x_ref, g1_ref,
                        o_ref, ko_ref, vo_ref, o_scr, *, t_new):
    nseq = kc_ref.shape[0]
    w = kc_ref.shape[1]
    pad = jnp.zeros((w - t_new, kc_ref.shape[2]), F32)
    band = _band(t_new, 2 * w)

    def body(it, carry):
        units = []
        for u in range(SAMPLE_UNROLL):
            b = it * SAMPLE_UNROLL + u
            r = pl.multiple_of(b * t_new, t_new)
            kn, vn = kn_ref[pl.ds(r, t_new), :], vn_ref[pl.ds(r, t_new), :]
            kc, vc = kc_ref[b], vc_ref[b]
            ko_ref[b, 0:w - t_new, :] = kc[t_new:]
            ko_ref[b, w - t_new:w, :] = kn
            vo_ref[b, 0:w - t_new, :] = vc[t_new:]
            vo_ref[b, w - t_new:w, :] = vn
            kwin = jnp.concatenate([kc, kn, pad], axis=0)
            vwin = jnp.concatenate([vc, vn, pad], axis=0)
            q = q_ref[pl.ds(r, t_new), :]
            units.append((r, vwin, [_scores(q, kwin, kv) for kv in range(N_KV_HEADS)]))
        for r, vwin, scores in units:
            cols = []
            for kv in range(N_KV_HEADS):
                cols.extend(_softmax_pv(scores[kv], band, vwin, kv, sink_ref))
            o_scr[pl.ds(r, t_new), :] = jnp.concatenate(cols, axis=1)
        return carry

    lax.fori_loop(0, nseq // SAMPLE_UNROLL, body, 0)
    mix = jnp.dot(o_scr[...].astype(BF16), wo_ref[...], preferred_element_type=F32)
    o_ref[...] = x_ref[...] + g1_ref[...] * mix


def _attn_sample(sink, q, kn, vn, kc, vc, wo, l, x, g1, t_new):
    m, d = x.shape
    db, w, nk = kc.shape
    g = WINDOW // t_new
    tm = g * t_new
    row = lambda width: pl.BlockSpec((tm, width), lambda i: (i, 0))
    cache = pl.BlockSpec((g, w, nk), lambda i: (i, 0, 0))
    return pl.pallas_call(
        functools.partial(_attn_sample_kernel, t_new=t_new),
        grid=(db // g,),
        in_specs=[pl.BlockSpec(memory_space=pltpu.SMEM), row(q.shape[1]), row(nk), row(nk), cache, cache,
                  _layer_spec(wo, l), row(d), row(d)],
        out_specs=[row(d), cache, cache],
        out_shape=[jax.ShapeDtypeStruct((m, d), F32), jax.ShapeDtypeStruct(kc.shape, F32),
                   jax.ShapeDtypeStruct(vc.shape, F32)],
        scratch_shapes=[pltpu.VMEM((tm, q.shape[1]), F32)],
        compiler_params=_params(1),
        name="attn_sample",
    )(sink, q, kn, vn, kc, vc, wo, x, g1)


def _sgu_kernel(x_ref, sc_ref, sh_ref, g1_ref, gw_ref, win_ref, bin_ref, lng_ref, lnb_ref,
                wsp_ref, bsp_ref, wout_ref, o_ref, *v_out):
    x = x_ref[...]
    tm = x.shape[0]
    d_sgu = lng_ref.shape[1]
    gdim = d_sgu // SGU_GROUPS
    h = _modnorm(x, gw_ref[...], sc_ref[0], sh_ref[0]).astype(BF16)
    z = _gelu_tanh(jnp.dot(h, win_ref[...], preferred_element_type=F32) + bin_ref[...])
    u, v = z[:, :d_sgu], z[:, d_sgu:]
    mu = jnp.mean(v, axis=-1, keepdims=True)
    vc = v - mu
    var = jnp.mean(vc * vc, axis=-1, keepdims=True)
    vn = vc * lax.rsqrt(var + EPS) * lng_ref[...] + lnb_ref[...]
    if v_out:
        v_out[0][...] = vn
    vb = vn.astype(BF16)
    r = lax.broadcasted_iota(jnp.int32, (CHUNK, CHUNK), 0)
    c = lax.broadcasted_iota(jnp.int32, (CHUNK, CHUNK), 1)
    wm = [jnp.where(r >= c, wsp_ref[g], 0.0).astype(BF16) for g in range(SGU_GROUPS)]
    rows = []
    for ch in range(tm // CHUNK):
        cols = [jnp.dot(wm[g], vb[ch * CHUNK:(ch + 1) * CHUNK, g * gdim:(g + 1) * gdim],
                        preferred_element_type=F32) + bsp_ref[g]
                for g in range(SGU_GROUPS)]
        rows.append(jnp.concatenate(cols, axis=1))
    mixed = jnp.concatenate(rows, axis=0)
    out = jnp.dot((u * mixed).astype(BF16), wout_ref[...], preferred_element_type=F32)
    o_ref[...] = x + g1_ref[0] * out


def _sgu(x, sc, sh, g1, gw, win, wout, l, b_in, lng, lnb, wsp, bsp, tm, tiles_per_group, emit_v):
    m, d = x.shape
    d_sgu = wout.shape[1]
    row = lambda width: pl.BlockSpec((tm, width), lambda i: (i, 0))
    out_specs = [row(d)]
    out_shape = [jax.ShapeDtypeStruct((m, d), F32)]
    if emit_v:
        out_specs.append(row(d_sgu))
        out_shape.append(jax.ShapeDtypeStruct((m, d_sgu), F32))
    return pl.pallas_call(
        _sgu_kernel,
        grid=(m // tm,),
        in_specs=[row(d), _mod_spec(sc, tiles_per_group), _mod_spec(sh, tiles_per_group),
                  _mod_spec(g1, tiles_per_group), _resident((1, d)), _layer_spec(win, l),
                  _resident(b_in.shape), _resident(lng.shape), _resident(lnb.shape),
                  _resident(wsp.shape), _resident(bsp.shape), _layer_spec(wout, l)],
        out_specs=out_specs,
        out_shape=out_shape,
        compiler_params=_params(1),
        name="sgu",
    )(x, sc, sh, g1, gw, win, b_in, lng, lnb, wsp, bsp, wout)


def _ffn_chunks(d_ff):
    return [(c, d_ff + c) for c in range(0, d_ff, FF_CHUNK)]


def _conv3(a, a1, a2, cw_ref, cb_ref, c0):
    cols = slice(c0, c0 + FF_CHUNK)
    return cw_ref[0:1, cols] * a2 + cw_ref[1:2, cols] * a1 + cw_ref[2:3, cols] * a + cb_ref[:, cols]


def _gate_down(conv_g, conv_u, wdn_ref, c0):
    act = (_silu(conv_g) * conv_u).astype(BF16)
    return jnp.dot(act, wdn_ref[c0:c0 + FF_CHUNK, :], preferred_element_type=F32)


def _final(xn, gf_ref):
    if gf_ref is None:
        return xn
    ms = jnp.mean(xn * xn, axis=-1, keepdims=True)
    return xn * lax.rsqrt(ms + EPS) * gf_ref[...]


def _ffn_prompt_kernel(x_ref, sc_ref, sh_ref, g2_ref, gw_ref, wup_ref, cw_ref, cb_ref, wdn_ref, *rest,
                       tiles_per_seq, final):
    gf_ref = rest[0] if final else None
    o_ref, tail_ref, halo_scr, a_scr, act_scr = rest[-5:]
    d_ff = wdn_ref.shape[0]
    x = x_ref[...]
    tm = x.shape[0]
    halo = SUBLANES
    nbuf = a_scr.shape[0]

    @pl.when(pl.program_id(0) % tiles_per_seq == 0)
    def _():
        halo_scr[...] = jnp.zeros(halo_scr.shape, F32)

    h = _modnorm(x, gw_ref[...], sc_ref[0], sh_ref[0]).astype(BF16)

    def up(buf, half, c0):
        cols = slice(c0, c0 + FF_CHUNK)
        a_scr[buf, half, 0:halo, :] = halo_scr[:, cols]
        a_scr[buf, half, halo:halo + tm, :] = jnp.dot(h, wup_ref[:, cols], preferred_element_type=F32)
        halo_scr[:, cols] = a_scr[buf, half, tm:tm + halo, :]

    below = lax.broadcasted_iota(jnp.int32, (SUBLANES, FF_CHUNK), 0)

    def shift_rows(slabs, k):
        rot = [pltpu.roll(s, k, 0) for s in slabs]
        return jnp.stack([jnp.where(below < k, rot[i - 1], rot[i]) for i in range(1, len(slabs))])

    def conv_taps(c0):
        cols = slice(c0, c0 + FF_CHUNK)
        full = lambda row: jnp.broadcast_to(row, (SUBLANES, FF_CHUNK))
        return [full(cw_ref[k:k + 1, cols]) for k in range(CONV_W)] + [full(cb_ref[:, cols])]

    def conv_rows(buf, half, taps, r0):
        slabs = [a_scr[buf, half, r:r + SUBLANES, :] for r in range(r0, r0 + FF_ROWS + halo, SUBLANES)]
        w0, w1, w2, b = taps
        return w0 * shift_rows(slabs, 2) + w1 * shift_rows(slabs, 1) + w2 * jnp.stack(slabs[1:]) + b

    def gate(buf, cg, cu):
        taps_g, taps_u = conv_taps(cg), conv_taps(cu)
        for r0 in range(0, tm, FF_ROWS):
            act = _silu(conv_rows(buf, 0, taps_g, r0)) * conv_rows(buf, 1, taps_u, r0)
            act_scr[r0:r0 + FF_ROWS, cg:cg + FF_CHUNK] = act.reshape(FF_ROWS, FF_CHUNK).astype(BF16)

    chunks = _ffn_chunks(d_ff)
    acc = []

    def down(c0, c1):
        dn = jnp.dot(act_scr[:, c0:c1], wdn_ref[c0:c1, :], preferred_element_type=F32)
        acc[:] = [dn if not acc else acc[0] + dn]

    pending, gated = [], 0
    for j in range(len(chunks) + FFN_UP_AHEAD):
        if j < len(chunks):
            buf, (cg, cu) = j % nbuf, chunks[j]
            up(buf, 0, cg)
            up(buf, 1, cu)
            pending.append((buf, cg, cu))
        if j >= FFN_UP_AHEAD:
            gate(*pending.pop(0))
            gated += 1
            if gated % FFN_DOWN_GROUP == 0 or gated == len(chunks):
                first = (gated - 1) // FFN_DOWN_GROUP * FFN_DOWN_GROUP
                down(first * FF_CHUNK, gated * FF_CHUNK)
    tail_ref[0] = halo_scr[...]
    o_ref[...] = _final(x + g2_ref[0] * acc[0], gf_ref)


def _ffn_prompt(x, sc, sh, g2, gw, wup, wdn, l, cw, cb, gf, batch, tm):
    m, d = x.shape
    c2 = wup.shape[-1]
    tiles_per_seq = m // batch // tm
    row = pl.BlockSpec((tm, d), lambda i: (i, 0))
    final = gf is not None
    in_specs = [row, _mod_spec(sc, tiles_per_seq), _mod_spec(sh, tiles_per_seq), _mod_spec(g2, tiles_per_seq),
                _resident((1, d)), _layer_spec(wup, l), _resident(cw.shape), _resident(cb.shape),
                _layer_spec(wdn, l)]
    args = [x, sc, sh, g2, gw, wup, cw, cb, wdn]
    if final:
        in_specs.append(_resident((1, d)))
        args.append(gf)
    return pl.pallas_call(
        functools.partial(_ffn_prompt_kernel, tiles_per_seq=tiles_per_seq, final=final),
        grid=(m // tm,),
        in_specs=in_specs,
        out_specs=[row, pl.BlockSpec((1, SUBLANES, c2), lambda i: (i // tiles_per_seq, 0, 0))],
        out_shape=[jax.ShapeDtypeStruct((m, d), F32), jax.ShapeDtypeStruct((batch, SUBLANES, c2), F32)],
        scratch_shapes=[pltpu.VMEM((SUBLANES, c2), F32),
                        pltpu.VMEM((FFN_UP_AHEAD + 1, 2, tm + SUBLANES, FF_CHUNK), F32),
                        pltpu.VMEM((tm, c2 // 2), BF16)],
        compiler_params=_params(1),
        name="ffn_prompt",
    )(*args)


def _ffn_sample_kernel(x_ref, sc_ref, sh_ref, g2_ref, gw_ref, st0_ref, st1_ref, wup_ref, cw_ref, cb_ref,
                       wdn_ref, *rest, final):
    gf_ref = rest[0] if final else None
    o_ref, nst_ref, ring = rest[-3:]
    d_ff = wdn_ref.shape[0]
    t = pl.program_id(0)

    @pl.when(t == 0)
    def _():
        ring[1] = st0_ref[...]
        ring[2] = st1_ref[...]

    s0, s1, s2 = t % CONV_W, (t + 2) % CONV_W, (t + 1) % CONV_W
    x = x_ref[...]
    h = _modnorm(x, gw_ref[...], sc_ref[...], sh_ref[...]).astype(BF16)

    def conv_cols(c0):
        cols = slice(c0, c0 + FF_CHUNK)
        a = jnp.dot(h, wup_ref[:, cols], preferred_element_type=F32)
        ring[s0, :, cols] = a
        nst_ref[:, cols] = a
        return _conv3(a, ring[s1, :, cols], ring[s2, :, cols], cw_ref, cb_ref, c0)

    acc = None
    for cg, cu in _ffn_chunks(d_ff):
        dn = _gate_down(conv_cols(cg), conv_cols(cu), wdn_ref, cg)
        acc = dn if acc is None else acc + dn
    o_ref[...] = _final(x + g2_ref[...] * acc, gf_ref)


def _ffn_sample(x2, sc, sh, g2, gw, st2, wup, wdn, l, cw, cb, gf):
    db, d = sc.shape
    t_new = x2.shape[1] // d
    c2 = wup.shape[-1]
    keep = CONV_W - 1
    step = pl.BlockSpec((db, d), lambda t: (0, t))
    final = gf is not None
    in_specs = [step, _resident((db, d)), _resident((db, d)), _resident((db, d)), _resident((1, d)),
                pl.BlockSpec((db, c2), lambda t: (0, 0), pipeline_mode=pl.Buffered(1)),
                pl.BlockSpec((db, c2), lambda t: (0, 1), pipeline_mode=pl.Buffered(1)),
                _layer_spec(wup, l), _resident(cw.shape), _resident(cb.shape), _layer_spec(wdn, l)]
    args = [x2, sc, sh, g2, gw, st2, st2, wup, cw, cb, wdn]
    if final:
        in_specs.append(_resident((1, d)))
        args.append(gf)
    return pl.pallas_call(
        functools.partial(_ffn_sample_kernel, final=final),
        grid=(t_new,),
        in_specs=in_specs,
        out_specs=[step, pl.BlockSpec((db, c2), lambda t: (0, jnp.maximum(t - (t_new - keep), 0)))],
        out_shape=[jax.ShapeDtypeStruct((db, t_new * d), F32), jax.ShapeDtypeStruct((db, keep * c2), F32)],
        scratch_shapes=[pltpu.VMEM((CONV_W, db, c2), F32)],
        compiler_params=_params(1),
        name="ffn_sample",
    )(*args)


def _tile_rows(size, want):
    return want if size % want == 0 else size


def kernel(x_prompt, x_sample, c_prompt, c_sample, cache_k, cache_v, state_conv, w_ada, b_ada, norm_mix,
           norm_ffn, w_qkv, b_qkv, attn_sink, w_o, w_sgu_in, b_sgu_in, sgu_ln_g, sgu_ln_b, w_spatial,
           b_spatial, w_sgu_out, w_up, conv_w, conv_b, w_down, norm_final):
    batch, seq, d = x_prompt.shape
    db, t_new, _ = x_sample.shape
    depth = w_ada.shape[0]

    pad = (-(batch + db)) % SUBLANES
    c_all = jnp.concatenate([c_prompt, c_sample, jnp.zeros((pad, d), F32)], axis=0)
    mod = _ada(c_all, w_ada, b_ada)

    def mod_part(l, k, lo, hi):
        return mod[l, lo:hi, k * d:(k + 1) * d]

    mp, ms = x_prompt.shape[0] * seq, db * t_new
    xp = x_prompt.reshape(mp, d)
    xs = x_sample.reshape(ms, d)

    tm_qkv_p, tm_sgu_p, tm_ffn_p = _tile_rows(seq, 512), _tile_rows(seq, 256), _tile_rows(seq, 512)
    tm_qkv_s, tm_sgu_s = _tile_rows(ms, 512), _tile_rows(ms, 256)

    tab_p = _rope_tables(jnp.arange(seq))
    tab_s = _rope_tables(PAST_LEN + (jnp.arange(ms) % t_new))

    wq_all, wo_all = w_qkv.astype(BF16), w_o.astype(BF16)
    win_all, wout_all = w_sgu_in.astype(BF16), w_sgu_out.astype(BF16)
    wup_all, wdn_all = w_up.astype(BF16), w_down.astype(BF16)

    new_k_p, new_v_p, new_conv_p = [], [], []
    new_k_s, new_v_s, new_conv_s, new_sgu_s = [], [], [], []

    for l in range(depth):
        idx = l // N_MIXERS
        gmix = norm_mix[l].reshape(1, d)
        gffn = norm_ffn[l].reshape(1, d)
        p_mod = [mod_part(l, k, 0, batch)[:, None, :] for k in range(6)]
        s_seq = [mod_part(l, k, batch, batch + db) for k in range(6)]
        s_row = [jnp.repeat(a, t_new, axis=0) for a in s_seq[:3]]

        if l % N_MIXERS == 0:
            bq = b_qkv[idx].reshape(1, -1)
            sink = attn_sink[idx] * LOG2E
            q, k, v = _qkv(xp, p_mod[1], p_mod[0], gmix, wq_all, idx, bq, tab_p, tm_qkv_p, seq // tm_qkv_p,
                           seq // tm_qkv_p)
            nblk = ATTN_BLOCKS_PER_STEP if seq % (ATTN_BLOCKS_PER_STEP * WINDOW) == 0 else 1
            xp = _attn_prompt(sink, q, k, v, wo_all, idx, xp, p_mod[2], batch, nblk)
            keep = min(WINDOW, seq)
            last = lambda a: a.reshape(batch, seq, -1)[:, seq - keep:].reshape(batch, keep, N_KV_HEADS, HEAD_DIM)
            new_k_p.append(last(k))
            new_v_p.append(last(v))

            tile3 = lambda a, tm: a.reshape(ms // tm, tm, d)
            q, k, v = _qkv(xs, tile3(s_row[1], tm_qkv_s), tile3(s_row[0], tm_qkv_s), gmix, wq_all, idx, bq, tab_s,
                           tm_qkv_s, 1, ms // tm_qkv_s)
            w = cache_k.shape[2]
            xs, nk, nv = _attn_sample(sink, q, k, v, cache_k[idx].reshape(db, w, -1),
                                      cache_v[idx].reshape(db, w, -1), wo_all, idx, xs, s_row[2], t_new)
            new_k_s.append(nk.reshape(db, w, N_KV_HEADS, HEAD_DIM))
            new_v_s.append(nv.reshape(db, w, N_KV_HEADS, HEAD_DIM))
        else:
            b_in = b_sgu_in[idx].reshape(1, -1)
            lng = sgu_ln_g[idx].reshape(1, -1)
            lnb = sgu_ln_b[idx].reshape(1, -1)
            tc = min(seq, CHUNK)
            (xp,) = _sgu(xp, p_mod[1], p_mod[0], p_mod[2], gmix, win_all, wout_all, idx, b_in, lng, lnb,
                         w_spatial[idx][:, :tc, :tc], b_spatial[idx][:, :tc, None],
                         tm_sgu_p, seq // tm_sgu_p, False)
            reps = CHUNK // t_new
            eye = jnp.eye(reps, dtype=F32)
            wsp_s = jnp.einsum("ab,gts->gatbs", eye, w_spatial[idx][:, :t_new, :t_new]).reshape(
                SGU_GROUPS, CHUNK, CHUNK)
            bsp_s = jnp.tile(b_spatial[idx][:, :t_new], (1, reps))[:, :, None]
            tile3 = lambda a: a.reshape(ms // tm_sgu_s, tm_sgu_s, d)
            xs, vrows = _sgu(xs, tile3(s_row[1]), tile3(s_row[0]), tile3(s_row[2]), gmix, win_all, wout_all, idx,
                             b_in, lng, lnb, wsp_s, bsp_s, tm_sgu_s, 1, True)
            new_sgu_s.append(vrows.reshape(db, t_new, -1))

        cw = conv_w[l]
        cb = conv_b[l].reshape(1, -1)
        gf = norm_final.reshape(1, d) if l == depth - 1 else None
        xp, tail = _ffn_prompt(xp, p_mod[4], p_mod[3], p_mod[5], gffn, wup_all, wdn_all, l, cw, cb, gf, batch,
                               tm_ffn_p)
        new_conv_p.append(tail[:, SUBLANES - (CONV_W - 1):, :])

        xs2, nst = _ffn_sample(xs.reshape(db, t_new * d), s_seq[4], s_seq[3], s_seq[5], gffn,
                               state_conv[l].reshape(db, -1), wup_all, wdn_all, l, cw, cb, gf)
        xs = xs2.reshape(ms, d)
        new_conv_s.append(nst.reshape(db, CONV_W - 1, -1))

    return (xp.reshape(batch, seq, d), xs.reshape(db, t_new, d),
            jnp.stack(new_k_p), jnp.stack(new_v_p), jnp.stack(new_conv_p),
            jnp.stack(new_k_s), jnp.stack(new_v_s), jnp.stack(new_conv_s), jnp.stack(new_sgu_s))
```

```python
import functools

import jax
import jax.numpy as jnp
from jax import lax
from jax.experimental import pallas as pl
from jax.experimental.pallas import tpu as pltpu

N_HEADS = 16
N_KV_HEADS = 4
HEAD_DIM = 64
Q_PER_KV = N_HEADS // N_KV_HEADS
WINDOW = 128
ROT_DIM = HEAD_DIM // 4
ROPE_THETA = 500000.0
CHUNK = 128
SGU_GROUPS = 4
CONV_W = 3
EPS = 1e-6
N_MIXERS = 2
PAST_LEN = 8192

LANES = 128
SUBLANES = 8
FF_CHUNK = 256
ATTN_BLOCKS_PER_STEP = 2
SGU_ROWS = 64
SGU_LN_ROWS = 16
FF_ROWS = 64
FFN_UP_AHEAD = 3
FFN_DOWN_GROUP = 4
ATTN_SCORES_AHEAD = 3
SAMPLE_UNROLL = 4
LOG2E = 1.4426950408889634
Q_SCALE = HEAD_DIM ** -0.5 * LOG2E
VMEM_LIMIT = 56 * 1024 * 1024
NEG_BIG = -1e30

F32 = jnp.float32
BF16 = jnp.bfloat16


def _params(n_axes=1, vmem=VMEM_LIMIT):
    return pltpu.CompilerParams(dimension_semantics=("arbitrary",) * n_axes, vmem_limit_bytes=vmem)


def _resident(shape):
    nd = len(shape)
    return pl.BlockSpec(shape, lambda *_: (0,) * nd, pipeline_mode=pl.Buffered(1))


def _layer_spec(stacked, l):
    nd = stacked.ndim - 1
    return pl.BlockSpec((None,) + stacked.shape[1:], lambda *_: (l,) + (0,) * nd, pipeline_mode=pl.Buffered(1))


def _modnorm(x, gw, scale, shift):
    ms = jnp.mean(x * x, axis=-1, keepdims=True)
    y = x * lax.rsqrt(ms + EPS) * gw
    return y * (1.0 + scale) + shift


def _silu(x):
    return x * jax.nn.sigmoid(x)


def _gelu_tanh(x):
    c = 0.7978845608028654
    hx = 0.5 * x
    return hx + hx * jnp.tanh(x * (c + (c * 0.044715) * (x * x)))


def _ada_kernel(c_ref, w_ref, b_ref, o_ref):
    s = _silu(c_ref[...]).astype(BF16)
    o_ref[0] = jnp.dot(s, w_ref[0].astype(BF16), preferred_element_type=F32) + b_ref[0]


def _ada(c_all, w_ada, b_ada):
    depth, d, n6 = w_ada.shape
    rows = c_all.shape[0]
    tn = 1024
    return pl.pallas_call(
        _ada_kernel,
        grid=(depth, n6 // tn),
        in_specs=[
            pl.BlockSpec((rows, d), lambda l, n: (0, 0)),
            pl.BlockSpec((1, d, tn), lambda l, n: (l, 0, n)),
            pl.BlockSpec((1, 1, tn), lambda l, n: (l, 0, n)),
        ],
        out_specs=pl.BlockSpec((1, rows, tn), lambda l, n: (l, 0, n)),
        out_shape=jax.ShapeDtypeStruct((depth, rows, n6), F32),
        compiler_params=_params(2),
        name="ada",
    )(c_all, w_ada, b_ada.reshape(depth, 1, n6))


def _qkv_kernel(x_ref, sc_ref, sh_ref, gw_ref, w_ref, b_ref, cos_ref, sa_ref, sb_ref,
                q_ref, k_ref, v_ref, *tails):
    h = _modnorm(x_ref[...], gw_ref[...], sc_ref[0], sh_ref[0]).astype(BF16)
    y = jnp.dot(h, w_ref[...], preferred_element_type=F32) + b_ref[...]
    cos, sa, sb = cos_ref[...], sa_ref[...], sb_ref[...]
    nq = N_HEADS * HEAD_DIM
    nk = N_KV_HEADS * HEAD_DIM

    def rope(yb):
        return yb * cos + pltpu.roll(yb, LANES - ROT_DIM // 2, 1) * sa + pltpu.roll(yb, ROT_DIM // 2, 1) * sb

    for j in range(nq // LANES):
        q_ref[:, j * LANES:(j + 1) * LANES] = (rope(y[:, j * LANES:(j + 1) * LANES]) * Q_SCALE).astype(q_ref.dtype)
    k = [rope(y[:, nq + j * LANES:nq + (j + 1) * LANES]) for j in range(nk // LANES)]
    v = y[:, nq + nk:]
    for j in range(nk // LANES):
        k_ref[:, j * LANES:(j + 1) * LANES] = k[j].astype(k_ref.dtype)
    v_ref[...] = v.astype(v_ref.dtype)
    if tails:
        kt_ref, vt_ref = tails
        first = y.shape[0] - kt_ref.shape[1]
        for j in range(nk // LANES):
            kt_ref[0, :, j * LANES:(j + 1) * LANES] = k[j][first:]
        vt_ref[0] = v[first:]


def _rope_tables(pos):
    half = ROT_DIM // 2
    inv = ROPE_THETA ** (-jnp.arange(0, ROT_DIM, 2, dtype=F32) / ROT_DIM)
    ang = pos.astype(F32)[:, None] * inv[None, :]
    cos, sin = jnp.cos(ang), jnp.sin(ang)
    t = pos.shape[0]
    one = jnp.ones((t, HEAD_DIM - ROT_DIM), F32)
    zero = jnp.zeros((t, HEAD_DIM - ROT_DIM), F32)
    zh = jnp.zeros((t, half), F32)
    cos_t = jnp.concatenate([cos, cos, one], 1)
    sa_t = jnp.concatenate([-sin, zh, zero], 1)
    sb_t = jnp.concatenate([zh, sin, zero], 1)
    rep = LANES // HEAD_DIM
    return tuple(jnp.tile(a, (1, rep)) for a in (cos_t, sa_t, sb_t))


def _mod_spec(mod, tiles_per_group):
    return pl.BlockSpec((1,) + mod.shape[1:], lambda i: (i // tiles_per_group, 0, 0))


def _qkv(x, sc, sh, gw, w, l, b, tables, tm, tiles_per_group, table_tiles, tail_rows=0):
    m, d = x.shape
    n = w.shape[-1]
    nq = N_HEADS * HEAD_DIM
    nk = N_KV_HEADS * HEAD_DIM
    row = lambda width: pl.BlockSpec((tm, width), lambda i: (i, 0))
    tab = pl.BlockSpec((tm, LANES), lambda i: (i % table_tiles, 0))
    dt = BF16 if tail_rows else F32
    tail = pl.BlockSpec((1, tail_rows, nk), lambda i: (i // tiles_per_group, 0, 0))
    tail_shape = jax.ShapeDtypeStruct((m // tm // tiles_per_group, tail_rows, nk), F32)
    return pl.pallas_call(
        _qkv_kernel,
        grid=(m // tm,),
        in_specs=[row(d), _mod_spec(sc, tiles_per_group), _mod_spec(sh, tiles_per_group),
                  _resident((1, d)), _layer_spec(w, l), _resident((1, n)), tab, tab, tab],
        out_specs=[row(nq), row(nk), row(nk)] + [tail, tail] * bool(tail_rows),
        out_shape=[jax.ShapeDtypeStruct((m, nq), dt), jax.ShapeDtypeStruct((m, nk), dt),
                   jax.ShapeDtypeStruct((m, nk), dt)] + [tail_shape, tail_shape] * bool(tail_rows),
        compiler_params=_params(1),
        name="qkv",
    )(x, sc, sh, gw, w, b, *tables)


def _dup_half(x, kv):
    lo = lax.broadcasted_iota(jnp.int32, (1, LANES), 1) < HEAD_DIM
    x = x.astype(F32)
    xr = pltpu.roll(x, HEAD_DIM, 1)
    return jnp.where(lo, x, xr) if kv % 2 == 0 else jnp.where(lo, xr, x)


def _masked_queries(q, kv):
    lo = lax.broadcasted_iota(jnp.int32, (1, LANES), 1) < HEAD_DIM
    q0 = q[:, (2 * kv) * LANES:(2 * kv + 1) * LANES]
    q1 = q[:, (2 * kv + 1) * LANES:(2 * kv + 2) * LANES]
    return jnp.concatenate([jnp.where(lo, q0, 0.0), jnp.where(lo, 0.0, q0),
                            jnp.where(lo, q1, 0.0), jnp.where(lo, 0.0, q1)], axis=0)


def _band(tq, nwin):
    i = lax.broadcasted_iota(jnp.int32, (tq, nwin), 0)
    j = lax.broadcasted_iota(jnp.int32, (tq, nwin), 1)
    diff = WINDOW + i - j
    return jnp.concatenate([(diff >= 0) & (diff <= WINDOW)] * Q_PER_KV, axis=0)


def _band_t(tq, nwin, prev_valid):
    j = lax.broadcasted_iota(jnp.int32, (nwin, tq), 0)
    i = lax.broadcasted_iota(jnp.int32, (nwin, tq), 1)
    diff = WINDOW + i - j
    band = (diff >= 0) & (diff <= WINDOW)
    if prev_valid is not None:
        band = band & ((j >= WINDOW) | prev_valid)
    return jnp.concatenate([band] * Q_PER_KV, axis=1)


def _scores_t(q, kwin, kv):
    kd = _dup_half(kwin[:, (kv // 2) * LANES:(kv // 2 + 1) * LANES], kv)
    return lax.dot_general(kd.astype(BF16), _masked_queries(q, kv).astype(BF16), (((1,), (1,)), ((), ())),
                           preferred_element_type=F32)


def _softmax_pv_t(s, band, vwin, kv, sink_ref):
    tq = s.shape[1] // Q_PER_KV
    nwin = s.shape[0]
    blk = kv // 2
    vt = vwin[:, blk * LANES:(blk + 1) * LANES].astype(F32).T
    vt = jnp.concatenate([vt[(kv % 2) * HEAD_DIM:(kv % 2 + 1) * HEAD_DIM],
                          jnp.ones((2 * SUBLANES, nwin), F32)], axis=0)
    sink = jnp.concatenate([jnp.full((1, tq), sink_ref[Q_PER_KV * kv + g], F32)
                            for g in range(Q_PER_KV)], axis=1)
    s = jnp.where(band, s, NEG_BIG)
    mx = jnp.maximum(jnp.max(s, axis=0, keepdims=True), sink)
    p = jnp.exp2(s - mx)
    o = jnp.dot(vt.astype(BF16), p.astype(BF16), preferred_element_type=F32)
    den = o[HEAD_DIM:HEAD_DIM + 1] + jnp.exp2(sink - mx)
    o = o[:HEAD_DIM] / den
    return [o[:, g * tq:(g + 1) * tq] for g in range(Q_PER_KV)]


def _attn_prompt_kernel(sink_ref, q_ref, ko_ref, kp_ref, vo_ref, vp_ref, wo_ref, x_ref, g1_ref, o_ref):
    nblk = q_ref.shape[0] // WINDOW
    kall = jnp.concatenate([kp_ref[...], ko_ref[...]], axis=0)
    vall = jnp.concatenate([vp_ref[...], vo_ref[...]], axis=0)
    band_first = _band_t(WINDOW, 2 * WINDOW, pl.program_id(1) > 0)
    band_rest = _band_t(WINDOW, 2 * WINDOW, None) if nblk > 1 else None
    heads = {c: [] for c in range(nblk)}

    def finish(c, kv, s):
        win = slice(c * WINDOW, (c + 2) * WINDOW)
        heads[c].extend(_softmax_pv_t(s, band_first if c == 0 else band_rest, vall[win], kv, sink_ref))
        if kv == N_KV_HEADS - 1:
            rows = slice(c * WINDOW, (c + 1) * WINDOW)
            ot = jnp.concatenate(heads[c], axis=0)
            mix = lax.dot_general(ot.astype(BF16), wo_ref[...], (((0,), (0,)), ((), ())),
                                  preferred_element_type=F32)
            o_ref[rows, :] = x_ref[rows, :] + g1_ref[0] * mix

    pending = []
    for c in range(nblk):
        for kv in range(N_KV_HEADS):
            s = _scores_t(q_ref[c * WINDOW:(c + 1) * WINDOW, :], kall[c * WINDOW:(c + 2) * WINDOW], kv)
            pending.append((c, kv, s))
            if len(pending) > ATTN_SCORES_AHEAD:
                finish(*pending.pop(0))
    for unit in pending:
        finish(*unit)


def _attn_prompt(sink, q, k, v, wo, l, x, g1, batch, nblk):
    m, d = x.shape
    tq = nblk * WINDOW
    nb = m // batch // tq
    nk = k.shape[1]
    own = lambda width: pl.BlockSpec((tq, width), lambda b, n: (b * nb + n, 0))
    prev = lambda width: pl.BlockSpec(
        (WINDOW, width), lambda b, n: (b * nb * nblk + jnp.maximum(n * nblk - 1, 0), 0))
    return pl.pallas_call(
        _attn_prompt_kernel,
        grid=(batch, nb),
        in_specs=[pl.BlockSpec(memory_space=pltpu.SMEM), own(q.shape[1]), own(nk), prev(nk), own(nk), prev(nk),
                  _layer_spec(wo, l),
                  own(d), pl.BlockSpec((1, 1, d), lambda b, n: (b, 0, 0))],
        out_specs=own(d),
        out_shape=jax.ShapeDtypeStruct((m, d), F32),
        compiler_params=_params(2),
        name="attn_prompt",
    )(sink, q, k, k, v, v, wo, x, g1)


def _attn_sample_kernel(sink_ref, q_ref, kn_ref, vn_ref, kc_ref, vc_ref, wo_ref, x_ref, g1_ref,
                        o_ref, ko_ref, vo_ref, o_scr, *, t_new):
    nseq, _, _, w = kc_ref.shape
    lanes = lax.broadcasted_iota(jnp.int32, (1, w), 1)
    lo = lanes < HEAD_DIM
    is_new = lanes >= w - t_new
    zpad = jnp.zeros((w - t_new, LANES), F32)
    band = _band(t_new, 2 * w)

    def new_rows(x, blk, at_end):
        rows = x[:, blk * LANES:(blk + 1) * LANES]
        return jnp.concatenate([zpad, rows] if at_end else [rows, zpad], axis=0)

    def body(it, carry):
        units = []
        for u in range(SAMPLE_UNROLL):
            b = it * SAMPLE_UNROLL + u
            r = pl.multiple_of(b * t_new, t_new)
            kn, vn = kn_ref[pl.ds(r, t_new), :], vn_ref[pl.ds(r, t_new), :]
            q = q_ref[pl.ds(r, t_new), :]
            knt = [new_rows(kn, blk, True).T for blk in range(N_KV_HEADS // 2)]
            vnt = [new_rows(vn, blk, True).T for blk in range(N_KV_HEADS // 2)]
            scores, vals = [], []
            for kv in range(N_KV_HEADS):
                half = slice((kv % 2) * HEAD_DIM, (kv % 2 + 1) * HEAD_DIM)
                kt, vt = kc_ref[b, kv], vc_ref[b, kv]
                ko_ref[b, kv] = jnp.where(is_new, knt[kv // 2][half], pltpu.roll(kt, w - t_new, 1))
                vo_ref[b, kv] = jnp.where(is_new, vnt[kv // 2][half], pltpu.roll(vt, w - t_new, 1))
                lhs = _masked_queries(q, kv).astype(BF16)
                kd_new = _dup_half(new_rows(kn, kv // 2, False), kv).astype(BF16)
                s_old = jnp.dot(lhs, jnp.concatenate([kt, kt], axis=0).astype(BF16), preferred_element_type=F32)
                s_new = lax.dot_general(lhs, kd_new, (((1,), (1,)), ((), ())), preferred_element_type=F32)
                scores.append(jnp.concatenate([s_old, s_new], axis=1))
                vals.append((jnp.concatenate([vt, vt], axis=0).astype(BF16),
                             _dup_half(new_rows(vn, kv // 2, False), kv).astype(BF16)))
            units.append((r, scores, vals))
        for r, scores, vals in units:
            cols = []
            for kv in range(N_KV_HEADS):
                sink = jnp.concatenate([jnp.full((t_new, 1), sink_ref[Q_PER_KV * kv + g], F32)
                                        for g in range(Q_PER_KV)], axis=0)
                s = jnp.where(band, scores[kv], NEG_BIG)
                mx = jnp.maximum(jnp.max(s, axis=-1, keepdims=True), sink)
                p = jnp.exp2(s - mx)
                den = jnp.sum(p, axis=-1, keepdims=True) + jnp.exp2(sink - mx)
                p = p.astype(BF16)
                vd_old, vd_new = vals[kv]
                o = lax.dot_general(p[:, :w], vd_old, (((1,), (1,)), ((), ())), preferred_element_type=F32)
                o = (o + jnp.dot(p[:, w:], vd_new, preferred_element_type=F32)) / den
                cols.append(jnp.where(lo, o[0:t_new], o[t_new:2 * t_new]))
                cols.append(jnp.where(lo, o[2 * t_new:3 * t_new], o[3 * t_new:4 * t_new]))
            o_scr[pl.ds(r, t_new), :] = jnp.concatenate(cols, axis=1)
        return carry

    lax.fori_loop(0, nseq // SAMPLE_UNROLL, body, 0)
    mix = jnp.dot(o_scr[...].astype(BF16), wo_ref[...], preferred_element_type=F32)
    o_ref[...] = x_ref[...] + g1_ref[...] * mix


def _attn_sample(sink, q, kn, vn, kct, vct, wo, l, x, g1, t_new):
    m, d = x.shape
    _, db, nkv, hd, w = kct.shape
    nk = nkv * hd
    g = WINDOW // t_new
    tm = g * t_new
    row = lambda width: pl.BlockSpec((tm, width), lambda i: (i, 0))
    cache_in = pl.BlockSpec((None, g, nkv, hd, w), lambda i: (l, i, 0, 0, 0))
    cache_out = pl.BlockSpec((g, nkv, hd, w), lambda i: (i, 0, 0, 0))
    cache_shape = jax.ShapeDtypeStruct((db, nkv, hd, w), F32)
    return pl.pallas_call(
        functools.partial(_attn_sample_kernel, t_new=t_new),
        grid=(db // g,),
        in_specs=[pl.BlockSpec(memory_space=pltpu.SMEM), row(q.shape[1]), row(nk), row(nk), cache_in, cache_in,
                  _layer_spec(wo, l), row(d), row(d)],
        out_specs=[row(d), cache_out, cache_out],
        out_shape=[jax.ShapeDtypeStruct((m, d), F32), cache_shape, cache_shape],
        scratch_shapes=[pltpu.VMEM((tm, q.shape[1]), F32)],
        compiler_params=_params(1),
        name="attn_sample",
    )(sink, q, kn, vn, kct, vct, wo, x, g1)


def _sgu_kernel(x_ref, sc_ref, sh_ref, g1_ref, gw_ref, win_ref, bin_ref, lng_ref, lnb_ref,
                wsp_ref, bsp_ref, wout_ref, o_ref, *rest, emit_v):
    v_out = rest[0] if emit_v else None
    h_scr, z_scr, vb_scr, gated_scr = rest[-4:]
    x = x_ref[...]
    tm = x.shape[0]
    d_sgu = lng_ref.shape[1]
    gdim = d_sgu // SGU_GROUPS
    h_scr[...] = _modnorm(x, gw_ref[...], sc_ref[0], sh_ref[0]).astype(BF16)

    def inproj(c0):
        z_scr[:, c0:c0 + gdim] = jnp.dot(h_scr[...], win_ref[:, c0:c0 + gdim], preferred_element_type=F32)

    def gelu_cols(c0):
        bias = jnp.broadcast_to(bin_ref[:, c0:c0 + gdim], (SUBLANES, gdim))
        for r0 in range(0, tm, SGU_ROWS):
            z = z_scr[r0:r0 + SGU_ROWS, c0:c0 + gdim].reshape(SGU_ROWS // SUBLANES, SUBLANES, gdim)
            z_scr[r0:r0 + SGU_ROWS, c0:c0 + gdim] = _gelu_tanh(z + bias).reshape(SGU_ROWS, gdim)

    def layernorm_v():
        g8 = jnp.broadcast_to(lng_ref[...], (SUBLANES, d_sgu))
        b8 = jnp.broadcast_to(lnb_ref[...], (SUBLANES, d_sgu))
        for r0 in range(0, tm, SGU_LN_ROWS):
            v = z_scr[r0:r0 + SGU_LN_ROWS, d_sgu:].reshape(SGU_LN_ROWS // SUBLANES, SUBLANES, d_sgu)
            vc = v - jnp.mean(v, axis=-1, keepdims=True)
            var = jnp.mean(vc * vc, axis=-1, keepdims=True)
            vn = (vc * lax.rsqrt(var + EPS) * g8 + b8).reshape(SGU_LN_ROWS, d_sgu)
            if emit_v:
                v_out[r0:r0 + SGU_LN_ROWS, :] = vn
            vb_scr[r0:r0 + SGU_LN_ROWS, :] = vn.astype(BF16)

    r = lax.broadcasted_iota(jnp.int32, (CHUNK, CHUNK), 0)
    c = lax.broadcasted_iota(jnp.int32, (CHUNK, CHUNK), 1)

    def mix(g):
        cols = slice(g * gdim, (g + 1) * gdim)
        wm = jnp.where(r >= c, wsp_ref[g], 0.0).astype(BF16)
        for ch in range(tm // CHUNK):
            rows = slice(ch * CHUNK, (ch + 1) * CHUNK)
            mixed = jnp.dot(wm, vb_scr[rows, cols], preferred_element_type=F32) + bsp_ref[g]
            gated_scr[rows, cols] = (z_scr[rows, cols] * mixed).astype(BF16)

    acc = []

    def outproj(g):
        cols = slice(g * gdim, (g + 1) * gdim)
        part = jnp.dot(gated_scr[:, cols], wout_ref[cols, :], preferred_element_type=F32)
        acc[:] = [part if not acc else acc[0] + part]

    ucols = [g * gdim for g in range(SGU_GROUPS)]
    vcols = [d_sgu + g * gdim for g in range(SGU_GROUPS)]
    order = vcols + ucols
    for i, c0 in enumerate(order):
        inproj(c0)
        if i >= 2:
            gelu_cols(order[i - 2])
        if i == len(vcols) + 1:
            layernorm_v()
    gelu_cols(order[-2])
    mix(0)
    gelu_cols(order[-1])
    for g in range(SGU_GROUPS):
        if g + 1 < SGU_GROUPS:
            mix(g + 1)
        outproj(g)
    o_ref[...] = x + g1_ref[0] * acc[0]


def _sgu(x, sc, sh, g1, gw, win, wout, l, b_in, lng, lnb, wsp, bsp, tm, tiles_per_group, emit_v):
    m, d = x.shape
    d_sgu = wout.shape[1]
    row = lambda width: pl.BlockSpec((tm, width), lambda i: (i, 0))
    out_specs = [row(d)]
    out_shape = [jax.ShapeDtypeStruct((m, d), F32)]
    if emit_v:
        out_specs.append(row(d_sgu))
        out_shape.append(jax.ShapeDtypeStruct((m, d_sgu), F32))
    return pl.pallas_call(
        functools.partial(_sgu_kernel, emit_v=emit_v),
        grid=(m // tm,),
        in_specs=[row(d), _mod_spec(sc, tiles_per_group), _mod_spec(sh, tiles_per_group),
                  _mod_spec(g1, tiles_per_group), _resident((1, d)), _layer_spec(win, l),
                  _resident(b_in.shape), _resident(lng.shape), _resident(lnb.shape),
                  _resident(wsp.shape), _resident(bsp.shape), _layer_spec(wout, l)],
        out_specs=out_specs,
        out_shape=out_shape,
        scratch_shapes=[pltpu.VMEM((tm, d), BF16), pltpu.VMEM((tm, 2 * d_sgu), F32),
                        pltpu.VMEM((tm, d_sgu), BF16), pltpu.VMEM((tm, d_sgu), BF16)],
        compiler_params=_params(1),
        name="sgu",
    )(x, sc, sh, g1, gw, win, b_in, lng, lnb, wsp, bsp, wout)


def _ffn_chunks(d_ff):
    return [(c, d_ff + c) for c in range(0, d_ff, FF_CHUNK)]


def _conv3(a, a1, a2, cw_ref, cb_ref, c0):
    cols = slice(c0, c0 + FF_CHUNK)
    return cw_ref[0:1, cols] * a2 + cw_ref[1:2, cols] * a1 + cw_ref[2:3, cols] * a + cb_ref[:, cols]


def _gate_down(conv_g, conv_u, wdn_ref, c0):
    act = (_silu(conv_g) * conv_u).astype(BF16)
    return jnp.dot(act, wdn_ref[c0:c0 + FF_CHUNK, :], preferred_element_type=F32)


def _final(xn, gf_ref):
    if gf_ref is None:
        return xn
    ms = jnp.mean(xn * xn, axis=-1, keepdims=True)
    return xn * lax.rsqrt(ms + EPS) * gf_ref[...]


def _ffn_prompt_kernel(x_ref, sc_ref, sh_ref, g2_ref, gw_ref, wup_ref, cw_ref, cb_ref, wdn_ref, *rest,
                       tiles_per_seq, final):
    gf_ref = rest[0] if final else None
    o_ref, tail_ref, halo_scr, a_scr, act_scr = rest[-5:]
    d_ff = wdn_ref.shape[0]
    x = x_ref[...]
    tm = x.shape[0]
    halo = SUBLANES
    nbuf = a_scr.shape[0]

    @pl.when(pl.program_id(0) % tiles_per_seq == 0)
    def _():
        halo_scr[...] = jnp.zeros(halo_scr.shape, F32)

    h = _modnorm(x, gw_ref[...], sc_ref[0], sh_ref[0]).astype(BF16)

    def up(buf, half, c0):
        cols = slice(c0, c0 + FF_CHUNK)
        a_scr[buf, half, 0:halo, :] = halo_scr[:, cols]
        a_scr[buf, half, halo:halo + tm, :] = jnp.dot(h, wup_ref[:, cols], preferred_element_type=F32)
        halo_scr[:, cols] = a_scr[buf, half, tm:tm + halo, :]

    below = lax.broadcasted_iota(jnp.int32, (SUBLANES, FF_CHUNK), 0)

    def shift_rows(slabs, k):
        rot = [pltpu.roll(s, k, 0) for s in slabs]
        return jnp.stack([jnp.where(below < k, rot[i - 1], rot[i]) for i in range(1, len(slabs))])

    def conv_taps(c0):
        cols = slice(c0, c0 + FF_CHUNK)
        full = lambda row: jnp.broadcast_to(row, (SUBLANES, FF_CHUNK))
        return [full(cw_ref[k:k + 1, cols]) for k in range(CONV_W)] + [full(cb_ref[:, cols])]

    def conv_rows(buf, half, taps, r0):
        slabs = [a_scr[buf, half, r:r + SUBLANES, :] for r in range(r0, r0 + FF_ROWS + halo, SUBLANES)]
        w0, w1, w2, b = taps
        return w0 * shift_rows(slabs, 2) + w1 * shift_rows(slabs, 1) + w2 * jnp.stack(slabs[1:]) + b

    def gate(buf, cg, cu):
        taps_g, taps_u = conv_taps(cg), conv_taps(cu)
        for r0 in range(0, tm, FF_ROWS):
            act = _silu(conv_rows(buf, 0, taps_g, r0)) * conv_rows(buf, 1, taps_u, r0)
            act_scr[r0:r0 + FF_ROWS, cg:cg + FF_CHUNK] = act.reshape(FF_ROWS, FF_CHUNK).astype(BF16)

    chunks = _ffn_chunks(d_ff)
    acc = []

    def down(c0, c1):
        dn = jnp.dot(act_scr[:, c0:c1], wdn_ref[c0:c1, :], preferred_element_type=F32)
        acc[:] = [dn if not acc else acc[0] + dn]

    pending, gated = [], 0
    for j in range(len(chunks) + FFN_UP_AHEAD):
        if j < len(chunks):
            buf, (cg, cu) = j % nbuf, chunks[j]
            up(buf, 0, cg)
            up(buf, 1, cu)
            pending.append((buf, cg, cu))
        if j >= FFN_UP_AHEAD:
            gate(*pending.pop(0))
            gated += 1
            if gated % FFN_DOWN_GROUP == 0 or gated == len(chunks):
                first = (gated - 1) // FFN_DOWN_GROUP * FFN_DOWN_GROUP
                down(first * FF_CHUNK, gated * FF_CHUNK)
    tail_ref[0] = halo_scr[...]
    o_ref[...] = _final(x + g2_ref[0] * acc[0], gf_ref)


def _ffn_prompt(x, sc, sh, g2, gw, wup, wdn, l, cw, cb, gf, batch, tm):
    m, d = x.shape
    c2 = wup.shape[-1]
    tiles_per_seq = m // batch // tm
    row = pl.BlockSpec((tm, d), lambda i: (i, 0))
    final = gf is not None
    in_specs = [row, _mod_spec(sc, tiles_per_seq), _mod_spec(sh, tiles_per_seq), _mod_spec(g2, tiles_per_seq),
                _resident((1, d)), _layer_spec(wup, l), _resident(cw.shape), _resident(cb.shape),
                _layer_spec(wdn, l)]
    args = [x, sc, sh, g2, gw, wup, cw, cb, wdn]
    if final:
        in_specs.append(_resident((1, d)))
        args.append(gf)
    return pl.pallas_call(
        functools.partial(_ffn_prompt_kernel, tiles_per_seq=tiles_per_seq, final=final),
        grid=(m // tm,),
        in_specs=in_specs,
        out_specs=[row, pl.BlockSpec((1, SUBLANES, c2), lambda i: (i // tiles_per_seq, 0, 0))],
        out_shape=[jax.ShapeDtypeStruct((m, d), F32), jax.ShapeDtypeStruct((batch, SUBLANES, c2), F32)],
        scratch_shapes=[pltpu.VMEM((SUBLANES, c2), F32),
                        pltpu.VMEM((FFN_UP_AHEAD + 1, 2, tm + SUBLANES, FF_CHUNK), F32),
                        pltpu.VMEM((tm, c2 // 2), BF16)],
        compiler_params=_params(1),
        name="ffn_prompt",
    )(*args)


def _ffn_sample_kernel(x_ref, sc_ref, sh_ref, g2_ref, gw_ref, st0_ref, st1_ref, wup_ref, cw_ref, cb_ref,
                       wdn_ref, *rest, final):
    gf_ref = rest[0] if final else None
    o_ref, nst_ref, ring = rest[-3:]
    d_ff = wdn_ref.shape[0]
    t = pl.program_id(0)

    @pl.when(t == 0)
    def _():
        ring[1] = st0_ref[...]
        ring[2] = st1_ref[...]

    s0, s1, s2 = t % CONV_W, (t + 2) % CONV_W, (t + 1) % CONV_W
    x = x_ref[...]
    h = _modnorm(x, gw_ref[...], sc_ref[...], sh_ref[...]).astype(BF16)

    def conv_cols(c0):
        cols = slice(c0, c0 + FF_CHUNK)
        a = jnp.dot(h, wup_ref[:, cols], preferred_element_type=F32)
        ring[s0, :, cols] = a
        nst_ref[:, cols] = a
        return _conv3(a, ring[s1, :, cols], ring[s2, :, cols], cw_ref, cb_ref, c0)

    acc = None
    for cg, cu in _ffn_chunks(d_ff):
        dn = _gate_down(conv_cols(cg), conv_cols(cu), wdn_ref, cg)
        acc = dn if acc is None else acc + dn
    o_ref[...] = _final(x + g2_ref[...] * acc, gf_ref)


def _ffn_sample(x2, sc, sh, g2, gw, st2, wup, wdn, l, cw, cb, gf):
    db, d = sc.shape
    t_new = x2.shape[1] // d
    c2 = wup.shape[-1]
    keep = CONV_W - 1
    step = pl.BlockSpec((db, d), lambda t: (0, t))
    final = gf is not None
    in_specs = [step, _resident((db, d)), _resident((db, d)), _resident((db, d)), _resident((1, d)),
                pl.BlockSpec((db, c2), lambda t: (0, 0), pipeline_mode=pl.Buffered(1)),
                pl.BlockSpec((db, c2), lambda t: (0, 1), pipeline_mode=pl.Buffered(1)),
                _layer_spec(wup, l), _resident(cw.shape), _resident(cb.shape), _layer_spec(wdn, l)]
    args = [x2, sc, sh, g2, gw, st2, st2, wup, cw, cb, wdn]
    if final:
        in_specs.append(_resident((1, d)))
        args.append(gf)
    return pl.pallas_call(
        functools.partial(_ffn_sample_kernel, final=final),
        grid=(t_new,),
        in_specs=in_specs,
        out_specs=[step, pl.BlockSpec((db, c2), lambda t: (0, jnp.maximum(t - (t_new - keep), 0)))],
        out_shape=[jax.ShapeDtypeStruct((db, t_new * d), F32), jax.ShapeDtypeStruct((db, keep * c2), F32)],
        scratch_shapes=[pltpu.VMEM((CONV_W, db, c2), F32)],
        compiler_params=_params(1),
        name="ffn_sample",
    )(*args)


def _tile_rows(size, want):
    return want if size % want == 0 else size


def kernel(x_prompt, x_sample, c_prompt, c_sample, cache_k, cache_v, state_conv, w_ada, b_ada, norm_mix,
           norm_ffn, w_qkv, b_qkv, attn_sink, w_o, w_sgu_in, b_sgu_in, sgu_ln_g, sgu_ln_b, w_spatial,
           b_spatial, w_sgu_out, w_up, conv_w, conv_b, w_down, norm_final):
    batch, seq, d = x_prompt.shape
    db, t_new, _ = x_sample.shape
    depth = w_ada.shape[0]

    pad = (-(batch + db)) % SUBLANES
    c_all = jnp.concatenate([c_prompt, c_sample, jnp.zeros((pad, d), F32)], axis=0)
    mod = _ada(c_all, w_ada, b_ada)

    def mod_part(l, k, lo, hi):
        return mod[l, lo:hi, k * d:(k + 1) * d]

    mp, ms = x_prompt.shape[0] * seq, db * t_new
    xp = x_prompt.reshape(mp, d)
    xs = x_sample.reshape(ms, d)

    tm_qkv_p, tm_sgu_p, tm_ffn_p = _tile_rows(seq, 512), _tile_rows(seq, 512), _tile_rows(seq, 512)
    tm_qkv_s, tm_sgu_s = _tile_rows(ms, 512), _tile_rows(ms, 256)

    tab_p = _rope_tables(jnp.arange(seq))
    tab_s = _rope_tables(PAST_LEN + (jnp.arange(ms) % t_new))

    wq_all, wo_all = w_qkv.astype(BF16), w_o.astype(BF16)
    win_all, wout_all = w_sgu_in.astype(BF16), w_sgu_out.astype(BF16)
    wup_all, wdn_all = w_up.astype(BF16), w_down.astype(BF16)

    kct = jnp.transpose(cache_k, (0, 1, 3, 4, 2))
    vct = jnp.transpose(cache_v, (0, 1, 3, 4, 2))

    new_k_p, new_v_p, new_conv_p = [], [], []
    new_k_s, new_v_s, new_conv_s, new_sgu_s = [], [], [], []

    for l in range(depth):
        idx = l // N_MIXERS
        gmix = norm_mix[l].reshape(1, d)
        gffn = norm_ffn[l].reshape(1, d)
        p_mod = [mod_part(l, k, 0, batch)[:, None, :] for k in range(6)]
        s_seq = [mod_part(l, k, batch, batch + db) for k in range(6)]
        s_row = [jnp.repeat(a, t_new, axis=0) for a in s_seq[:3]]

        if l % N_MIXERS == 0:
            bq = b_qkv[idx].reshape(1, -1)
            sink = attn_sink[idx] * LOG2E
            keep = min(WINDOW, seq)
            q, k, v, k_last, v_last = _qkv(xp, p_mod[1], p_mod[0], gmix, wq_all, idx, bq, tab_p, tm_qkv_p,
                                           seq // tm_qkv_p, seq // tm_qkv_p, tail_rows=keep)
            nblk = ATTN_BLOCKS_PER_STEP if seq % (ATTN_BLOCKS_PER_STEP * WINDOW) == 0 else 1
            xp = _attn_prompt(sink, q, k, v, wo_all, idx, xp, p_mod[2], batch, nblk)
            new_k_p.append(k_last.reshape(batch, keep, N_KV_HEADS, HEAD_DIM))
            new_v_p.append(v_last.reshape(batch, keep, N_KV_HEADS, HEAD_DIM))

            tile3 = lambda a, tm: a.reshape(ms // tm, tm, d)
            q, k, v = _qkv(xs, tile3(s_row[1], tm_qkv_s), tile3(s_row[0], tm_qkv_s), gmix, wq_all, idx, bq, tab_s,
                           tm_qkv_s, 1, ms // tm_qkv_s)
            xs, nk, nv = _attn_sample(sink, q, k, v, kct, vct, wo_all, idx, xs, s_row[2], t_new)
            new_k_s.append(nk)
            new_v_s.append(nv)
        else:
            b_in = b_sgu_in[idx].reshape(1, -1)
            lng = sgu_ln_g[idx].reshape(1, -1)
            lnb = sgu_ln_b[idx].reshape(1, -1)
            tc = min(seq, CHUNK)
            (xp,) = _sgu(xp, p_mod[1], p_mod[0], p_mod[2], gmix, win_all, wout_all, idx, b_in, lng, lnb,
                         w_spatial[idx][:, :tc, :tc], b_spatial[idx][:, :tc, None],
                         tm_sgu_p, seq // tm_sgu_p, False)
            reps = CHUNK // t_new
            eye = jnp.eye(reps, dtype=F32)
            wsp_s = jnp.einsum("ab,gts->gatbs", eye, w_spatial[idx][:, :t_new, :t_new]).reshape(
                SGU_GROUPS, CHUNK, CHUNK)
            bsp_s = jnp.tile(b_spatial[idx][:, :t_new], (1, reps))[:, :, None]
            tile3 = lambda a: a.reshape(ms // tm_sgu_s, tm_sgu_s, d)
            xs, vrows = _sgu(xs, tile3(s_row[1]), tile3(s_row[0]), tile3(s_row[2]), gmix, win_all, wout_all, idx,
                             b_in, lng, lnb, wsp_s, bsp_s, tm_sgu_s, 1, True)
            new_sgu_s.append(vrows.reshape(db, t_new, -1))

        cw = conv_w[l]
        cb = conv_b[l].reshape(1, -1)
        gf = norm_final.reshape(1, d) if l == depth - 1 else None
        xp, tail = _ffn_prompt(xp, p_mod[4], p_mod[3], p_mod[5], gffn, wup_all, wdn_all, l, cw, cb, gf, batch,
                               tm_ffn_p)
        new_conv_p.append(tail[:, SUBLANES - (CONV_W - 1):, :])

        xs2, nst = _ffn_sample(xs.reshape(db, t_new * d), s_seq[4], s_seq[3], s_seq[5], gffn,
                               state_conv[l].reshape(db, -1), wup_all, wdn_all, l, cw, cb, gf)
        xs = xs2.reshape(ms, d)
        new_conv_s.append(nst.reshape(db, CONV_W - 1, -1))

    return (xp.reshape(batch, seq, d), xs.reshape(db, t_new, d),
            jnp.stack(new_k_p), jnp.stack(new_v_p), jnp.stack(new_conv_p),
            jnp.transpose(jnp.stack(new_k_s), (0, 1, 4, 2, 3)), jnp.transpose(jnp.stack(new_v_s), (0, 1, 4, 2, 3)),
            jnp.stack(new_conv_s), jnp.stack(new_sgu_s))
```

```python
import functools

import jax
import jax.numpy as jnp
from jax import lax
from jax.experimental import pallas as pl
from jax.experimental.pallas import tpu as pltpu

N_HEADS = 16
N_KV_HEADS = 4
HEAD_DIM = 64
Q_PER_KV = N_HEADS // N_KV_HEADS
WINDOW = 128
ROT_DIM = HEAD_DIM // 4
ROPE_THETA = 500000.0
CHUNK = 128
SGU_GROUPS = 4
CONV_W = 3
EPS = 1e-6
N_MIXERS = 2
PAST_LEN = 8192

LANES = 128
SUBLANES = 8
FF_CHUNK = 256
ATTN_BLOCKS_PER_STEP = 4
SGU_ROWS = 64
SGU_LN_ROWS = 16
FF_ROWS = 64
FFN_UP_AHEAD = 3
FFN_DOWN_GROUP = 4
ATTN_SCORES_AHEAD = 3
SAMPLE_UNROLL = 4
LOG2E = 1.4426950408889634
Q_SCALE = HEAD_DIM ** -0.5 * LOG2E
VMEM_LIMIT = 56 * 1024 * 1024
NEG_BIG = -1e30

F32 = jnp.float32
BF16 = jnp.bfloat16


def _params(n_axes=1, vmem=VMEM_LIMIT):
    return pltpu.CompilerParams(dimension_semantics=("arbitrary",) * n_axes, vmem_limit_bytes=vmem)


def _resident(shape):
    nd = len(shape)
    return pl.BlockSpec(shape, lambda *_: (0,) * nd, pipeline_mode=pl.Buffered(1))


def _layer_spec(stacked, l):
    nd = stacked.ndim - 1
    return pl.BlockSpec((None,) + stacked.shape[1:], lambda *_: (l,) + (0,) * nd, pipeline_mode=pl.Buffered(1))


def _modnorm(x, gw, scale, shift):
    ms = jnp.mean(x * x, axis=-1, keepdims=True)
    y = x * lax.rsqrt(ms + EPS) * gw
    return y * (1.0 + scale) + shift


def _silu(x):
    return x * jax.nn.sigmoid(x)


def _gelu_tanh(x):
    c = 0.7978845608028654
    hx = 0.5 * x
    return hx + hx * jnp.tanh(x * (c + (c * 0.044715) * (x * x)))


def _ada_kernel(c_ref, w_ref, b_ref, o_ref):
    s = _silu(c_ref[...]).astype(BF16)
    o_ref[0] = jnp.dot(s, w_ref[0].astype(BF16), preferred_element_type=F32) + b_ref[0]


def _ada(c_all, w_ada, b_ada):
    depth, d, n6 = w_ada.shape
    rows = c_all.shape[0]
    tn = 1024
    return pl.pallas_call(
        _ada_kernel,
        grid=(depth, n6 // tn),
        in_specs=[
            pl.BlockSpec((rows, d), lambda l, n: (0, 0)),
            pl.BlockSpec((1, d, tn), lambda l, n: (l, 0, n)),
            pl.BlockSpec((1, 1, tn), lambda l, n: (l, 0, n)),
        ],
        out_specs=pl.BlockSpec((1, rows, tn), lambda l, n: (l, 0, n)),
        out_shape=jax.ShapeDtypeStruct((depth, rows, n6), F32),
        compiler_params=_params(2),
        name="ada",
    )(c_all, w_ada, b_ada.reshape(depth, 1, n6))


def _qkv_kernel(x_ref, sc_ref, sh_ref, gw_ref, w_ref, b_ref, cos_ref, sa_ref, sb_ref,
                q_ref, k_ref, v_ref, *tails):
    h = _modnorm(x_ref[...], gw_ref[...], sc_ref[0], sh_ref[0]).astype(BF16)
    y = jnp.dot(h, w_ref[...], preferred_element_type=F32) + b_ref[...]
    cos, sa, sb = cos_ref[...], sa_ref[...], sb_ref[...]
    nq = N_HEADS * HEAD_DIM
    nk = N_KV_HEADS * HEAD_DIM

    def rope(yb):
        return yb * cos + pltpu.roll(yb, LANES - ROT_DIM // 2, 1) * sa + pltpu.roll(yb, ROT_DIM // 2, 1) * sb

    for j in range(nq // LANES):
        q_ref[:, j * LANES:(j + 1) * LANES] = (rope(y[:, j * LANES:(j + 1) * LANES]) * Q_SCALE).astype(q_ref.dtype)
    k = [rope(y[:, nq + j * LANES:nq + (j + 1) * LANES]) for j in range(nk // LANES)]
    v = y[:, nq + nk:]
    for j in range(nk // LANES):
        k_ref[:, j * LANES:(j + 1) * LANES] = k[j].astype(k_ref.dtype)
    v_ref[...] = v.astype(v_ref.dtype)
    if tails:
        kt_ref, vt_ref = tails
        first = y.shape[0] - kt_ref.shape[1]
        for j in range(nk // LANES):
            kt_ref[0, :, j * LANES:(j + 1) * LANES] = k[j][first:]
        vt_ref[0] = v[first:]


def _rope_tables(pos):
    half = ROT_DIM // 2
    inv = ROPE_THETA ** (-jnp.arange(0, ROT_DIM, 2, dtype=F32) / ROT_DIM)
    ang = pos.astype(F32)[:, None] * inv[None, :]
    cos, sin = jnp.cos(ang), jnp.sin(ang)
    t = pos.shape[0]
    one = jnp.ones((t, HEAD_DIM - ROT_DIM), F32)
    zero = jnp.zeros((t, HEAD_DIM - ROT_DIM), F32)
    zh = jnp.zeros((t, half), F32)
    cos_t = jnp.concatenate([cos, cos, one], 1)
    sa_t = jnp.concatenate([-sin, zh, zero], 1)
    sb_t = jnp.concatenate([zh, sin, zero], 1)
    rep = LANES // HEAD_DIM
    return tuple(jnp.tile(a, (1, rep)) for a in (cos_t, sa_t, sb_t))


def _mod_spec(mod, tiles_per_group):
    return pl.BlockSpec((1,) + mod.shape[1:], lambda i: (i // tiles_per_group, 0, 0))


def _qkv(x, sc, sh, gw, w, l, b, tables, tm, tiles_per_group, table_tiles, tail_rows=0):
    m, d = x.shape
    n = w.shape[-1]
    nq = N_HEADS * HEAD_DIM
    nk = N_KV_HEADS * HEAD_DIM
    row = lambda width: pl.BlockSpec((tm, width), lambda i: (i, 0))
    tab = pl.BlockSpec((tm, LANES), lambda i: (i % table_tiles, 0))
    dt = BF16 if tail_rows else F32
    tail = pl.BlockSpec((1, tail_rows, nk), lambda i: (i // tiles_per_group, 0, 0))
    tail_shape = jax.ShapeDtypeStruct((m // tm // tiles_per_group, tail_rows, nk), F32)
    return pl.pallas_call(
        _qkv_kernel,
        grid=(m // tm,),
        in_specs=[row(d), _mod_spec(sc, tiles_per_group), _mod_spec(sh, tiles_per_group),
                  _resident((1, d)), _layer_spec(w, l), _resident((1, n)), tab, tab, tab],
        out_specs=[row(nq), row(nk), row(nk)] + [tail, tail] * bool(tail_rows),
        out_shape=[jax.ShapeDtypeStruct((m, nq), dt), jax.ShapeDtypeStruct((m, nk), dt),
                   jax.ShapeDtypeStruct((m, nk), dt)] + [tail_shape, tail_shape] * bool(tail_rows),
        compiler_params=_params(1),
        name="qkv",
    )(x, sc, sh, gw, w, b, *tables)


def _dup_half(x, kv):
    lo = lax.broadcasted_iota(jnp.int32, (1, LANES), 1) < HEAD_DIM
    x = x.astype(F32)
    xr = pltpu.roll(x, HEAD_DIM, 1)
    return jnp.where(lo, x, xr) if kv % 2 == 0 else jnp.where(lo, xr, x)


def _masked_queries(q, kv):
    lo = lax.broadcasted_iota(jnp.int32, (1, LANES), 1) < HEAD_DIM
    q0 = q[:, (2 * kv) * LANES:(2 * kv + 1) * LANES]
    q1 = q[:, (2 * kv + 1) * LANES:(2 * kv + 2) * LANES]
    return jnp.concatenate([jnp.where(lo, q0, 0.0), jnp.where(lo, 0.0, q0),
                            jnp.where(lo, q1, 0.0), jnp.where(lo, 0.0, q1)], axis=0)


def _band(tq, nwin):
    i = lax.broadcasted_iota(jnp.int32, (tq, nwin), 0)
    j = lax.broadcasted_iota(jnp.int32, (tq, nwin), 1)
    diff = WINDOW + i - j
    return jnp.concatenate([(diff >= 0) & (diff <= WINDOW)] * Q_PER_KV, axis=0)


def _band_t(tq, nwin, prev_valid):
    j = lax.broadcasted_iota(jnp.int32, (nwin, tq), 0)
    i = lax.broadcasted_iota(jnp.int32, (nwin, tq), 1)
    diff = WINDOW + i - j
    band = (diff >= 0) & (diff <= WINDOW)
    if prev_valid is not None:
        band = band & ((j >= WINDOW) | prev_valid)
    return jnp.concatenate([band] * Q_PER_KV, axis=1)


def _scores_t(q, kwin, kv):
    kd = _dup_half(kwin[:, (kv // 2) * LANES:(kv // 2 + 1) * LANES], kv)
    return lax.dot_general(kd.astype(BF16), _masked_queries(q, kv).astype(BF16), (((1,), (1,)), ((), ())),
                           preferred_element_type=F32)


def _softmax_pv_t(s, band, vwin, kv, sink_ref):
    tq = s.shape[1] // Q_PER_KV
    nwin = s.shape[0]
    blk = kv // 2
    vt = vwin[:, blk * LANES:(blk + 1) * LANES].astype(F32).T
    vt = jnp.concatenate([vt[(kv % 2) * HEAD_DIM:(kv % 2 + 1) * HEAD_DIM],
                          jnp.ones((2 * SUBLANES, nwin), F32)], axis=0)
    sink = jnp.concatenate([jnp.full((1, tq), sink_ref[Q_PER_KV * kv + g], F32)
                            for g in range(Q_PER_KV)], axis=1)
    s = jnp.where(band, s, NEG_BIG)
    mx = jnp.maximum(jnp.max(s, axis=0, keepdims=True), sink)
    p = jnp.exp2(s - mx)
    o = jnp.dot(vt.astype(BF16), p.astype(BF16), preferred_element_type=F32)
    den = o[HEAD_DIM:HEAD_DIM + 1] + jnp.exp2(sink - mx)
    o = o[:HEAD_DIM] / den
    return [o[:, g * tq:(g + 1) * tq] for g in range(Q_PER_KV)]


def _attn_prompt_kernel(sink_ref, q_ref, ko_ref, kp_ref, vo_ref, vp_ref, wo_ref, x_ref, g1_ref, o_ref):
    nblk = q_ref.shape[0] // WINDOW
    kall = jnp.concatenate([kp_ref[...], ko_ref[...]], axis=0)
    vall = jnp.concatenate([vp_ref[...], vo_ref[...]], axis=0)
    band_first = _band_t(WINDOW, 2 * WINDOW, pl.program_id(1) > 0)
    band_rest = _band_t(WINDOW, 2 * WINDOW, None) if nblk > 1 else None
    heads = {c: [] for c in range(nblk)}

    def finish(c, kv, s):
        win = slice(c * WINDOW, (c + 2) * WINDOW)
        heads[c].extend(_softmax_pv_t(s, band_first if c == 0 else band_rest, vall[win], kv, sink_ref))
        if kv == N_KV_HEADS - 1:
            rows = slice(c * WINDOW, (c + 1) * WINDOW)
            ot = jnp.concatenate(heads[c], axis=0)
            mix = lax.dot_general(ot.astype(BF16), wo_ref[...], (((0,), (0,)), ((), ())),
                                  preferred_element_type=F32)
            o_ref[rows, :] = x_ref[rows, :] + g1_ref[0] * mix

    pending = []
    for c in range(nblk):
        for kv in range(N_KV_HEADS):
            s = _scores_t(q_ref[c * WINDOW:(c + 1) * WINDOW, :], kall[c * WINDOW:(c + 2) * WINDOW], kv)
            pending.append((c, kv, s))
            if len(pending) > ATTN_SCORES_AHEAD:
                finish(*pending.pop(0))
    for unit in pending:
        finish(*unit)


def _attn_prompt(sink, q, k, v, wo, l, x, g1, batch, nblk):
    m, d = x.shape
    tq = nblk * WINDOW
    nb = m // batch // tq
    nk = k.shape[1]
    own = lambda width: pl.BlockSpec((tq, width), lambda b, n: (b * nb + n, 0))
    prev = lambda width: pl.BlockSpec(
        (WINDOW, width), lambda b, n: (b * nb * nblk + jnp.maximum(n * nblk - 1, 0), 0))
    return pl.pallas_call(
        _attn_prompt_kernel,
        grid=(batch, nb),
        in_specs=[pl.BlockSpec(memory_space=pltpu.SMEM), own(q.shape[1]), own(nk), prev(nk), own(nk), prev(nk),
                  _layer_spec(wo, l),
                  own(d), pl.BlockSpec((1, 1, d), lambda b, n: (b, 0, 0))],
        out_specs=own(d),
        out_shape=jax.ShapeDtypeStruct((m, d), F32),
        compiler_params=_params(2),
        name="attn_prompt",
    )(sink, q, k, k, v, v, wo, x, g1)


def _attn_sample_kernel(sink_ref, q_ref, kn_ref, vn_ref, kc_ref, vc_ref, wo_ref, x_ref, g1_ref,
                        o_ref, ko_ref, vo_ref, o_scr, *, t_new):
    nseq, _, _, w = kc_ref.shape
    lanes = lax.broadcasted_iota(jnp.int32, (1, w), 1)
    lo = lanes < HEAD_DIM
    is_new = lanes >= w - t_new
    zpad = jnp.zeros((w - t_new, LANES), F32)
    band = _band(t_new, 2 * w)

    def new_rows(x, blk, at_end):
        rows = x[:, blk * LANES:(blk + 1) * LANES]
        return jnp.concatenate([zpad, rows] if at_end else [rows, zpad], axis=0)

    def body(it, carry):
        units = []
        for u in range(SAMPLE_UNROLL):
            b = it * SAMPLE_UNROLL + u
            r = pl.multiple_of(b * t_new, t_new)
            kn, vn = kn_ref[pl.ds(r, t_new), :], vn_ref[pl.ds(r, t_new), :]
            q = q_ref[pl.ds(r, t_new), :]
            knt = [new_rows(kn, blk, True).T for blk in range(N_KV_HEADS // 2)]
            vnt = [new_rows(vn, blk, True).T for blk in range(N_KV_HEADS // 2)]
            scores, vals = [], []
            for kv in range(N_KV_HEADS):
                half = slice((kv % 2) * HEAD_DIM, (kv % 2 + 1) * HEAD_DIM)
                kt, vt = kc_ref[b, kv], vc_ref[b, kv]
                ko_ref[b, kv] = jnp.where(is_new, knt[kv // 2][half], pltpu.roll(kt, w - t_new, 1))
                vo_ref[b, kv] = jnp.where(is_new, vnt[kv // 2][half], pltpu.roll(vt, w - t_new, 1))
                lhs = _masked_queries(q, kv).astype(BF16)
                kd_new = _dup_half(new_rows(kn, kv // 2, False), kv).astype(BF16)
                s_old = jnp.dot(lhs, jnp.concatenate([kt, kt], axis=0).astype(BF16), preferred_element_type=F32)
                s_new = lax.dot_general(lhs, kd_new, (((1,), (1,)), ((), ())), preferred_element_type=F32)
                scores.append(jnp.concatenate([s_old, s_new], axis=1))
                vals.append((jnp.concatenate([vt, vt], axis=0).astype(BF16),
                             _dup_half(new_rows(vn, kv // 2, False), kv).astype(BF16)))
            units.append((r, scores, vals))
        for r, scores, vals in units:
            cols = []
            for kv in range(N_KV_HEADS):
                sink = jnp.concatenate([jnp.full((t_new, 1), sink_ref[Q_PER_KV * kv + g], F32)
                                        for g in range(Q_PER_KV)], axis=0)
                s = jnp.where(band, scores[kv], NEG_BIG)
                mx = jnp.maximum(jnp.max(s, axis=-1, keepdims=True), sink)
                p = jnp.exp2(s - mx)
                den = jnp.sum(p, axis=-1, keepdims=True) + jnp.exp2(sink - mx)
                p = p.astype(BF16)
                vd_old, vd_new = vals[kv]
                o = lax.dot_general(p[:, :w], vd_old, (((1,), (1,)), ((), ())), preferred_element_type=F32)
                o = (o + jnp.dot(p[:, w:], vd_new, preferred_element_type=F32)) / den
                cols.append(jnp.where(lo, o[0:t_new], o[t_new:2 * t_new]))
                cols.append(jnp.where(lo, o[2 * t_new:3 * t_new], o[3 * t_new:4 * t_new]))
            o_scr[pl.ds(r, t_new), :] = jnp.concatenate(cols, axis=1)
        return carry

    lax.fori_loop(0, nseq // SAMPLE_UNROLL, body, 0)
    mix = jnp.dot(o_scr[...].astype(BF16), wo_ref[...], preferred_element_type=F32)
    o_ref[...] = x_ref[...] + g1_ref[...] * mix


def _attn_sample(sink, q, kn, vn, kct, vct, wo, l, x, g1, t_new):
    m, d = x.shape
    _, db, nkv, hd, w = kct.shape
    nk = nkv * hd
    g = WINDOW // t_new
    tm = g * t_new
    row = lambda width: pl.BlockSpec((tm, width), lambda i: (i, 0))
    cache_in = pl.BlockSpec((None, g, nkv, hd, w), lambda i: (l, i, 0, 0, 0))
    cache_out = pl.BlockSpec((g, nkv, hd, w), lambda i: (i, 0, 0, 0))
    cache_shape = jax.ShapeDtypeStruct((db, nkv, hd, w), F32)
    return pl.pallas_call(
        functools.partial(_attn_sample_kernel, t_new=t_new),
        grid=(db // g,),
        in_specs=[pl.BlockSpec(memory_space=pltpu.SMEM), row(q.shape[1]), row(nk), row(nk), cache_in, cache_in,
                  _layer_spec(wo, l), row(d), row(d)],
        out_specs=[row(d), cache_out, cache_out],
        out_shape=[jax.ShapeDtypeStruct((m, d), F32), cache_shape, cache_shape],
        scratch_shapes=[pltpu.VMEM((tm, q.shape[1]), F32)],
        compiler_params=_params(1),
        name="attn_sample",
    )(sink, q, kn, vn, kct, vct, wo, x, g1)


def _sgu_kernel(x_ref, sc_ref, sh_ref, g1_ref, gw_ref, win_ref, bin_ref, lng_ref, lnb_ref,
                wsp_ref, bsp_ref, wout_ref, o_ref, *rest, emit_v):
    v_out = rest[0] if emit_v else None
    h_scr, z_scr, vb_scr, gated_scr = rest[-4:]
    x = x_ref[...]
    tm = x.shape[0]
    d_sgu = lng_ref.shape[1]
    gdim = d_sgu // SGU_GROUPS
    h_scr[...] = _modnorm(x, gw_ref[...], sc_ref[0], sh_ref[0]).astype(BF16)

    def inproj(c0):
        z_scr[:, c0:c0 + gdim] = jnp.dot(h_scr[...], win_ref[:, c0:c0 + gdim], preferred_element_type=F32)

    def gelu_cols(c0):
        bias = jnp.broadcast_to(bin_ref[:, c0:c0 + gdim], (SUBLANES, gdim))
        for r0 in range(0, tm, SGU_ROWS):
            z = z_scr[r0:r0 + SGU_ROWS, c0:c0 + gdim].reshape(SGU_ROWS // SUBLANES, SUBLANES, gdim)
            z_scr[r0:r0 + SGU_ROWS, c0:c0 + gdim] = _gelu_tanh(z + bias).reshape(SGU_ROWS, gdim)

    def layernorm_v():
        g8 = jnp.broadcast_to(lng_ref[...], (SUBLANES, d_sgu))
        b8 = jnp.broadcast_to(lnb_ref[...], (SUBLANES, d_sgu))
        for r0 in range(0, tm, SGU_LN_ROWS):
            v = z_scr[r0:r0 + SGU_LN_ROWS, d_sgu:].reshape(SGU_LN_ROWS // SUBLANES, SUBLANES, d_sgu)
            vc = v - jnp.mean(v, axis=-1, keepdims=True)
            var = jnp.mean(vc * vc, axis=-1, keepdims=True)
            vn = (vc * lax.rsqrt(var + EPS) * g8 + b8).reshape(SGU_LN_ROWS, d_sgu)
            if emit_v:
                v_out[r0:r0 + SGU_LN_ROWS, :] = vn
            vb_scr[r0:r0 + SGU_LN_ROWS, :] = vn.astype(BF16)

    r = lax.broadcasted_iota(jnp.int32, (CHUNK, CHUNK), 0)
    c = lax.broadcasted_iota(jnp.int32, (CHUNK, CHUNK), 1)

    def mix(g):
        cols = slice(g * gdim, (g + 1) * gdim)
        wm = jnp.where(r >= c, wsp_ref[g], 0.0).astype(BF16)
        for ch in range(tm // CHUNK):
            rows = slice(ch * CHUNK, (ch + 1) * CHUNK)
            mixed = jnp.dot(wm, vb_scr[rows, cols], preferred_element_type=F32) + bsp_ref[g]
            gated_scr[rows, cols] = (z_scr[rows, cols] * mixed).astype(BF16)

    acc = []

    def outproj(g):
        cols = slice(g * gdim, (g + 1) * gdim)
        part = jnp.dot(gated_scr[:, cols], wout_ref[cols, :], preferred_element_type=F32)
        acc[:] = [part if not acc else acc[0] + part]

    ucols = [g * gdim for g in range(SGU_GROUPS)]
    vcols = [d_sgu + g * gdim for g in range(SGU_GROUPS)]
    order = vcols + ucols
    for i, c0 in enumerate(order):
        inproj(c0)
        if i >= 2:
            gelu_cols(order[i - 2])
        if i == len(vcols) + 1:
            layernorm_v()
    gelu_cols(order[-2])
    mix(0)
    gelu_cols(order[-1])
    for g in range(SGU_GROUPS):
        if g + 1 < SGU_GROUPS:
            mix(g + 1)
        outproj(g)
    o_ref[...] = x + g1_ref[0] * acc[0]


def _sgu(x, sc, sh, g1, gw, win, wout, l, b_in, lng, lnb, wsp, bsp, tm, tiles_per_group, emit_v):
    m, d = x.shape
    d_sgu = wout.shape[1]
    row = lambda width: pl.BlockSpec((tm, width), lambda i: (i, 0))
    out_specs = [row(d)]
    out_shape = [jax.ShapeDtypeStruct((m, d), F32)]
    if emit_v:
        out_specs.append(row(d_sgu))
        out_shape.append(jax.ShapeDtypeStruct((m, d_sgu), F32))
    return pl.pallas_call(
        functools.partial(_sgu_kernel, emit_v=emit_v),
        grid=(m // tm,),
        in_specs=[row(d), _mod_spec(sc, tiles_per_group), _mod_spec(sh, tiles_per_group),
                  _mod_spec(g1, tiles_per_group), _resident((1, d)), _layer_spec(win, l),
                  _resident(b_in.shape), _resident(lng.shape), _resident(lnb.shape),
                  _resident(wsp.shape), _resident(bsp.shape), _layer_spec(wout, l)],
        out_specs=out_specs,
        out_shape=out_shape,
        scratch_shapes=[pltpu.VMEM((tm, d), BF16), pltpu.VMEM((tm, 2 * d_sgu), F32),
                        pltpu.VMEM((tm, d_sgu), BF16), pltpu.VMEM((tm, d_sgu), BF16)],
        compiler_params=_params(1),
        name="sgu",
    )(x, sc, sh, g1, gw, win, b_in, lng, lnb, wsp, bsp, wout)


def _ffn_chunks(d_ff):
    return [(c, d_ff + c) for c in range(0, d_ff, FF_CHUNK)]


def _final(xn, gf_ref):
    if gf_ref is None:
        return xn
    ms = jnp.mean(xn * xn, axis=-1, keepdims=True)
    return xn * lax.rsqrt(ms + EPS) * gf_ref[...]


def _ffn_prompt_kernel(x_ref, sc_ref, sh_ref, g2_ref, gw_ref, wup_ref, cw_ref, cb_ref, wdn_ref, *rest,
                       tiles_per_seq, final):
    gf_ref = rest[0] if final else None
    o_ref, tail_ref, halo_scr, a_scr, act_scr = rest[-5:]
    d_ff = wdn_ref.shape[0]
    x = x_ref[...]
    tm = x.shape[0]
    halo = SUBLANES
    nbuf = a_scr.shape[0]

    @pl.when(pl.program_id(0) % tiles_per_seq == 0)
    def _():
        halo_scr[...] = jnp.zeros(halo_scr.shape, F32)

    h = _modnorm(x, gw_ref[...], sc_ref[0], sh_ref[0]).astype(BF16)

    def up(buf, half, c0):
        cols = slice(c0, c0 + FF_CHUNK)
        a_scr[buf, half, 0:halo, :] = halo_scr[:, cols]
        a_scr[buf, half, halo:halo + tm, :] = jnp.dot(h, wup_ref[:, cols], preferred_element_type=F32)
        halo_scr[:, cols] = a_scr[buf, half, tm:tm + halo, :]

    below = lax.broadcasted_iota(jnp.int32, (SUBLANES, FF_CHUNK), 0)

    def shift_rows(slabs, k):
        rot = [pltpu.roll(s, k, 0) for s in slabs]
        return jnp.stack([jnp.where(below < k, rot[i - 1], rot[i]) for i in range(1, len(slabs))])

    def conv_taps(c0):
        cols = slice(c0, c0 + FF_CHUNK)
        full = lambda row: jnp.broadcast_to(row, (SUBLANES, FF_CHUNK))
        return [full(cw_ref[k:k + 1, cols]) for k in range(CONV_W)] + [full(cb_ref[:, cols])]

    def conv_rows(buf, half, taps, r0):
        slabs = [a_scr[buf, half, r:r + SUBLANES, :] for r in range(r0, r0 + FF_ROWS + halo, SUBLANES)]
        w0, w1, w2, b = taps
        return w0 * shift_rows(slabs, 2) + w1 * shift_rows(slabs, 1) + w2 * jnp.stack(slabs[1:]) + b

    def gate(buf, cg, cu):
        taps_g, taps_u = conv_taps(cg), conv_taps(cu)
        for r0 in range(0, tm, FF_ROWS):
            act = _silu(conv_rows(buf, 0, taps_g, r0)) * conv_rows(buf, 1, taps_u, r0)
            act_scr[r0:r0 + FF_ROWS, cg:cg + FF_CHUNK] = act.reshape(FF_ROWS, FF_CHUNK).astype(BF16)

    chunks = _ffn_chunks(d_ff)
    acc = []

    def down(c0, c1):
        dn = jnp.dot(act_scr[:, c0:c1], wdn_ref[c0:c1, :], preferred_element_type=F32)
        acc[:] = [dn if not acc else acc[0] + dn]

    pending, gated = [], 0
    for j in range(len(chunks) + FFN_UP_AHEAD):
        if j < len(chunks):
            buf, (cg, cu) = j % nbuf, chunks[j]
            up(buf, 0, cg)
            up(buf, 1, cu)
            pending.append((buf, cg, cu))
        if j >= FFN_UP_AHEAD:
            gate(*pending.pop(0))
            gated += 1
            if gated % FFN_DOWN_GROUP == 0 or gated == len(chunks):
                first = (gated - 1) // FFN_DOWN_GROUP * FFN_DOWN_GROUP
                down(first * FF_CHUNK, gated * FF_CHUNK)
    tail_ref[0] = halo_scr[...]
    o_ref[...] = _final(x + g2_ref[0] * acc[0], gf_ref)


def _ffn_prompt(x, sc, sh, g2, gw, wup, wdn, l, cw, cb, gf, batch, tm):
    m, d = x.shape
    c2 = wup.shape[-1]
    tiles_per_seq = m // batch // tm
    row = pl.BlockSpec((tm, d), lambda i: (i, 0))
    final = gf is not None
    in_specs = [row, _mod_spec(sc, tiles_per_seq), _mod_spec(sh, tiles_per_seq), _mod_spec(g2, tiles_per_seq),
                _resident((1, d)), _layer_spec(wup, l), _resident(cw.shape), _resident(cb.shape),
                _layer_spec(wdn, l)]
    args = [x, sc, sh, g2, gw, wup, cw, cb, wdn]
    if final:
        in_specs.append(_resident((1, d)))
        args.append(gf)
    return pl.pallas_call(
        functools.partial(_ffn_prompt_kernel, tiles_per_seq=tiles_per_seq, final=final),
        grid=(m // tm,),
        in_specs=in_specs,
        out_specs=[row, pl.BlockSpec((1, SUBLANES, c2), lambda i: (i // tiles_per_seq, 0, 0))],
        out_shape=[jax.ShapeDtypeStruct((m, d), F32), jax.ShapeDtypeStruct((batch, SUBLANES, c2), F32)],
        scratch_shapes=[pltpu.VMEM((SUBLANES, c2), F32),
                        pltpu.VMEM((FFN_UP_AHEAD + 1, 2, tm + SUBLANES, FF_CHUNK), F32),
                        pltpu.VMEM((tm, c2 // 2), BF16)],
        compiler_params=_params(1),
        name="ffn_prompt",
    )(*args)


def _ffn_sample_kernel(x_ref, sc_ref, sh_ref, g2_ref, gw_ref, st_ref, wup_ref, cw_ref, cb_ref, wdn_ref, *rest,
                       final):
    gf_ref = rest[0] if final else None
    o_ref, nst_ref, h_scr, a_scr, act_scr = rest[-5:]
    nb, d = sc_ref.shape
    t_new = x_ref.shape[1] // d
    rows = t_new * nb
    keep = CONV_W - 1
    d_ff = wdn_ref.shape[0]
    c2 = 2 * d_ff
    nbuf = a_scr.shape[0]

    for t in range(t_new):
        xt = x_ref[:, t * d:(t + 1) * d]
        h_scr[t * nb:(t + 1) * nb, :] = _modnorm(xt, gw_ref[...], sc_ref[...], sh_ref[...]).astype(BF16)

    def up(buf, half, c0):
        cols = slice(c0, c0 + FF_CHUNK)
        for r in range(keep):
            a_scr[buf, half, r * nb:(r + 1) * nb, :] = st_ref[:, r * c2 + c0:r * c2 + c0 + FF_CHUNK]
        a_scr[buf, half, keep * nb:keep * nb + rows, :] = jnp.dot(h_scr[...], wup_ref[:, cols],
                                                                  preferred_element_type=F32)
        for r in range(keep):
            nst_ref[:, r * c2 + c0:r * c2 + c0 + FF_CHUNK] = a_scr[buf, half, rows + r * nb:rows + (r + 1) * nb, :]

    def conv_taps(c0):
        cols = slice(c0, c0 + FF_CHUNK)
        full = lambda row: jnp.broadcast_to(row, (SUBLANES, FF_CHUNK))
        return [full(cw_ref[k:k + 1, cols]) for k in range(CONV_W)] + [full(cb_ref[:, cols])]

    def conv_rows(buf, half, taps, r0):
        blk = lambda k: a_scr[buf, half, r0 + k * nb:r0 + k * nb + FF_ROWS, :].reshape(
            FF_ROWS // SUBLANES, SUBLANES, FF_CHUNK)
        w0, w1, w2, b = taps
        return w0 * blk(0) + w1 * blk(1) + w2 * blk(2) + b

    def gate(buf, cg, cu):
        taps_g, taps_u = conv_taps(cg), conv_taps(cu)
        for r0 in range(0, rows, FF_ROWS):
            act = _silu(conv_rows(buf, 0, taps_g, r0)) * conv_rows(buf, 1, taps_u, r0)
            act_scr[r0:r0 + FF_ROWS, cg:cg + FF_CHUNK] = act.reshape(FF_ROWS, FF_CHUNK).astype(BF16)

    chunks = _ffn_chunks(d_ff)
    acc = []

    def down(c0, c1):
        dn = jnp.dot(act_scr[:, c0:c1], wdn_ref[c0:c1, :], preferred_element_type=F32)
        acc[:] = [dn if not acc else acc[0] + dn]

    pending, gated = [], 0
    for j in range(len(chunks) + FFN_UP_AHEAD):
        if j < len(chunks):
            buf, (cg, cu) = j % nbuf, chunks[j]
            up(buf, 0, cg)
            up(buf, 1, cu)
            pending.append((buf, cg, cu))
        if j >= FFN_UP_AHEAD:
            gate(*pending.pop(0))
            gated += 1
            if gated % FFN_DOWN_GROUP == 0 or gated == len(chunks):
                first = (gated - 1) // FFN_DOWN_GROUP * FFN_DOWN_GROUP
                down(first * FF_CHUNK, gated * FF_CHUNK)
    for t in range(t_new):
        xt = x_ref[:, t * d:(t + 1) * d]
        o_ref[:, t * d:(t + 1) * d] = _final(xt + g2_ref[...] * acc[0][t * nb:(t + 1) * nb, :], gf_ref)


def _ffn_sample(x2, sc, sh, g2, gw, st2, wup, wdn, l, cw, cb, gf, nb):
    db, d = sc.shape
    t_new = x2.shape[1] // d
    c2 = wup.shape[-1]
    keep = CONV_W - 1
    rows = t_new * nb
    final = gf is not None
    seqs = lambda width: pl.BlockSpec((nb, width), lambda i: (i, 0))
    in_specs = [seqs(t_new * d), seqs(d), seqs(d), seqs(d), _resident((1, d)), seqs(keep * c2),
                _layer_spec(wup, l), _resident(cw.shape), _resident(cb.shape), _layer_spec(wdn, l)]
    args = [x2, sc, sh, g2, gw, st2, wup, cw, cb, wdn]
    if final:
        in_specs.append(_resident((1, d)))
        args.append(gf)
    return pl.pallas_call(
        functools.partial(_ffn_sample_kernel, final=final),
        grid=(db // nb,),
        in_specs=in_specs,
        out_specs=[seqs(t_new * d), seqs(keep * c2)],
        out_shape=[jax.ShapeDtypeStruct((db, t_new * d), F32), jax.ShapeDtypeStruct((db, keep * c2), F32)],
        scratch_shapes=[pltpu.VMEM((rows, d), BF16),
                        pltpu.VMEM((FFN_UP_AHEAD + 1, 2, rows + keep * nb, FF_CHUNK), F32),
                        pltpu.VMEM((rows, c2 // 2), BF16)],
        compiler_params=_params(1),
        name="ffn_sample",
    )(*args)


def _tile_rows(size, want):
    return want if size % want == 0 else size


def kernel(x_prompt, x_sample, c_prompt, c_sample, cache_k, cache_v, state_conv, w_ada, b_ada, norm_mix,
           norm_ffn, w_qkv, b_qkv, attn_sink, w_o, w_sgu_in, b_sgu_in, sgu_ln_g, sgu_ln_b, w_spatial,
           b_spatial, w_sgu_out, w_up, conv_w, conv_b, w_down, norm_final):
    batch, seq, d = x_prompt.shape
    db, t_new, _ = x_sample.shape
    depth = w_ada.shape[0]

    pad = (-(batch + db)) % SUBLANES
    c_all = jnp.concatenate([c_prompt, c_sample, jnp.zeros((pad, d), F32)], axis=0)
    mod = _ada(c_all, w_ada, b_ada)

    def mod_part(l, k, lo, hi):
        return mod[l, lo:hi, k * d:(k + 1) * d]

    mp, ms = x_prompt.shape[0] * seq, db * t_new
    xp = x_prompt.reshape(mp, d)
    xs = x_sample.reshape(ms, d)

    tm_qkv_p, tm_sgu_p, tm_ffn_p = _tile_rows(seq, 512), _tile_rows(seq, 512), _tile_rows(seq, 512)
    tm_qkv_s, tm_sgu_s = _tile_rows(ms, 512), _tile_rows(ms, 256)
    nb_ffn_s = _tile_rows(db, 512 // t_new)

    tab_p = _rope_tables(jnp.arange(seq))
    tab_s = _rope_tables(PAST_LEN + (jnp.arange(ms) % t_new))

    wq_all, wo_all = w_qkv.astype(BF16), w_o.astype(BF16)
    win_all, wout_all = w_sgu_in.astype(BF16), w_sgu_out.astype(BF16)
    wup_all, wdn_all = w_up.astype(BF16), w_down.astype(BF16)

    kct = jnp.transpose(cache_k, (0, 1, 3, 4, 2))
    vct = jnp.transpose(cache_v, (0, 1, 3, 4, 2))

    new_k_p, new_v_p, new_conv_p = [], [], []
    new_k_s, new_v_s, new_conv_s, new_sgu_s = [], [], [], []

    for l in range(depth):
        idx = l // N_MIXERS
        gmix = norm_mix[l].reshape(1, d)
        gffn = norm_ffn[l].reshape(1, d)
        p_mod = [mod_part(l, k, 0, batch)[:, None, :] for k in range(6)]
        s_seq = [mod_part(l, k, batch, batch + db) for k in range(6)]
        s_row = [jnp.repeat(a, t_new, axis=0) for a in s_seq[:3]]

        if l % N_MIXERS == 0:
            bq = b_qkv[idx].reshape(1, -1)
            sink = attn_sink[idx] * LOG2E
            keep = min(WINDOW, seq)
            q, k, v, k_last, v_last = _qkv(xp, p_mod[1], p_mod[0], gmix, wq_all, idx, bq, tab_p, tm_qkv_p,
                                           seq // tm_qkv_p, seq // tm_qkv_p, tail_rows=keep)
            nblk = ATTN_BLOCKS_PER_STEP if seq % (ATTN_BLOCKS_PER_STEP * WINDOW) == 0 else 1
            xp = _attn_prompt(sink, q, k, v, wo_all, idx, xp, p_mod[2], batch, nblk)
            new_k_p.append(k_last.reshape(batch, keep, N_KV_HEADS, HEAD_DIM))
            new_v_p.append(v_last.reshape(batch, keep, N_KV_HEADS, HEAD_DIM))

            tile3 = lambda a, tm: a.reshape(ms // tm, tm, d)
            q, k, v = _qkv(xs, tile3(s_row[1], tm_qkv_s), tile3(s_row[0], tm_qkv_s), gmix, wq_all, idx, bq, tab_s,
                           tm_qkv_s, 1, ms // tm_qkv_s)
            xs, nk, nv = _attn_sample(sink, q, k, v, kct, vct, wo_all, idx, xs, s_row[2], t_new)
            new_k_s.append(nk)
            new_v_s.append(nv)
        else:
            b_in = b_sgu_in[idx].reshape(1, -1)
            lng = sgu_ln_g[idx].reshape(1, -1)
            lnb = sgu_ln_b[idx].reshape(1, -1)
            tc = min(seq, CHUNK)
            (xp,) = _sgu(xp, p_mod[1], p_mod[0], p_mod[2], gmix, win_all, wout_all, idx, b_in, lng, lnb,
                         w_spatial[idx][:, :tc, :tc], b_spatial[idx][:, :tc, None],
                         tm_sgu_p, seq // tm_sgu_p, False)
            reps = CHUNK // t_new
            eye = jnp.eye(reps, dtype=F32)
            wsp_s = jnp.einsum("ab,gts->gatbs", eye, w_spatial[idx][:, :t_new, :t_new]).reshape(
                SGU_GROUPS, CHUNK, CHUNK)
            bsp_s = jnp.tile(b_spatial[idx][:, :t_new], (1, reps))[:, :, None]
            tile3 = lambda a: a.reshape(ms // tm_sgu_s, tm_sgu_s, d)
            xs, vrows = _sgu(xs, tile3(s_row[1]), tile3(s_row[0]), tile3(s_row[2]), gmix, win_all, wout_all, idx,
                             b_in, lng, lnb, wsp_s, bsp_s, tm_sgu_s, 1, True)
            new_sgu_s.append(vrows.reshape(db, t_new, -1))

        cw = conv_w[l]
        cb = conv_b[l].reshape(1, -1)
        gf = norm_final.reshape(1, d) if l == depth - 1 else None
        xp, tail = _ffn_prompt(xp, p_mod[4], p_mod[3], p_mod[5], gffn, wup_all, wdn_all, l, cw, cb, gf, batch,
                               tm_ffn_p)
        new_conv_p.append(tail[:, SUBLANES - (CONV_W - 1):, :])

        xs2, nst = _ffn_sample(xs.reshape(db, t_new * d), s_seq[4], s_seq[3], s_seq[5], gffn,
                               state_conv[l].reshape(db, -1), wup_all, wdn_all, l, cw, cb, gf, nb_ffn_s)
        xs = xs2.reshape(ms, d)
        new_conv_s.append(nst.reshape(db, CONV_W - 1, -1))

    return (xp.reshape(batch, seq, d), xs.reshape(db, t_new, d),
            jnp.stack(new_k_p), jnp.stack(new_v_p), jnp.stack(new_conv_p),
            jnp.transpose(jnp.stack(new_k_s), (0, 1, 4, 2, 3)), jnp.transpose(jnp.stack(new_v_s), (0, 1, 4, 2, 3)),
            jnp.stack(new_conv_s), jnp.stack(new_sgu_s))
```

```python
import functools

import jax
import jax.numpy as jnp
from jax import lax
from jax.experimental import pallas as pl
from jax.experimental.pallas import tpu as pltpu

N_HEADS = 16
N_KV_HEADS = 4
HEAD_DIM = 64
Q_PER_KV = N_HEADS // N_KV_HEADS
WINDOW = 128
ROT_DIM = HEAD_DIM // 4
ROPE_THETA = 500000.0
CHUNK = 128
SGU_GROUPS = 4
CONV_W = 3
EPS = 1e-6
N_MIXERS = 2
PAST_LEN = 8192

LANES = 128
SUBLANES = 8
FF_CHUNK = 256
ATTN_BLOCKS_PER_STEP = 4
SGU_ROWS = 64
SGU_LN_ROWS = 16
FF_ROWS = 64
FFN_UP_AHEAD = 3
FFN_DOWN_GROUP = 4
ATTN_SCORES_AHEAD = 3
SAMPLE_UNROLL = 4
LOG2E = 1.4426950408889634
Q_SCALE = HEAD_DIM ** -0.5 * LOG2E
VMEM_LIMIT = 56 * 1024 * 1024
NEG_BIG = -1e30

F32 = jnp.float32
BF16 = jnp.bfloat16


def _params(n_axes=1, vmem=VMEM_LIMIT):
    return pltpu.CompilerParams(dimension_semantics=("arbitrary",) * n_axes, vmem_limit_bytes=vmem)


def _resident(shape):
    nd = len(shape)
    return pl.BlockSpec(shape, lambda *_: (0,) * nd, pipeline_mode=pl.Buffered(1))


def _layer_spec(stacked, l):
    nd = stacked.ndim - 1
    return pl.BlockSpec((None,) + stacked.shape[1:], lambda *_: (l,) + (0,) * nd, pipeline_mode=pl.Buffered(1))


def _modnorm(x, gw, scale, shift):
    ms = jnp.mean(x * x, axis=-1, keepdims=True)
    y = x * lax.rsqrt(ms + EPS) * gw
    return y * (1.0 + scale) + shift


def _silu(x):
    return x * jax.nn.sigmoid(x)


def _gelu_tanh(x):
    c = 0.7978845608028654
    hx = 0.5 * x
    return hx + hx * jnp.tanh(x * (c + (c * 0.044715) * (x * x)))


def _ada_kernel(c_ref, w_ref, b_ref, o_ref):
    s = _silu(c_ref[...]).astype(BF16)
    o_ref[0] = jnp.dot(s, w_ref[0].astype(BF16), preferred_element_type=F32) + b_ref[0]


def _ada(c_all, w_ada, b_ada):
    depth, d, n6 = w_ada.shape
    rows = c_all.shape[0]
    tn = 1024
    return pl.pallas_call(
        _ada_kernel,
        grid=(depth, n6 // tn),
        in_specs=[
            pl.BlockSpec((rows, d), lambda l, n: (0, 0)),
            pl.BlockSpec((1, d, tn), lambda l, n: (l, 0, n)),
            pl.BlockSpec((1, 1, tn), lambda l, n: (l, 0, n)),
        ],
        out_specs=pl.BlockSpec((1, rows, tn), lambda l, n: (l, 0, n)),
        out_shape=jax.ShapeDtypeStruct((depth, rows, n6), F32),
        compiler_params=_params(2),
        name="ada",
    )(c_all, w_ada, b_ada.reshape(depth, 1, n6))


def _qkv_kernel(x_ref, sc_ref, sh_ref, gw_ref, w_ref, b_ref, cos_ref, sa_ref, sb_ref,
                q_ref, k_ref, v_ref, *tails):
    h = _modnorm(x_ref[...], gw_ref[...], sc_ref[0], sh_ref[0]).astype(BF16)
    y = jnp.dot(h, w_ref[...], preferred_element_type=F32) + b_ref[...]
    cos, sa, sb = cos_ref[...], sa_ref[...], sb_ref[...]
    nq = N_HEADS * HEAD_DIM
    nk = N_KV_HEADS * HEAD_DIM

    def rope(yb):
        return yb * cos + pltpu.roll(yb, LANES - ROT_DIM // 2, 1) * sa + pltpu.roll(yb, ROT_DIM // 2, 1) * sb

    for j in range(nq // LANES):
        q_ref[:, j * LANES:(j + 1) * LANES] = (rope(y[:, j * LANES:(j + 1) * LANES]) * Q_SCALE).astype(q_ref.dtype)
    k = [rope(y[:, nq + j * LANES:nq + (j + 1) * LANES]) for j in range(nk // LANES)]
    v = y[:, nq + nk:]
    for j in range(nk // LANES):
        k_ref[:, j * LANES:(j + 1) * LANES] = k[j].astype(k_ref.dtype)
    v_ref[...] = v.astype(v_ref.dtype)
    if tails:
        kt_ref, vt_ref = tails
        first = y.shape[0] - kt_ref.shape[1]
        for j in range(nk // LANES):
            kt_ref[0, :, j * LANES:(j + 1) * LANES] = k[j][first:]
        vt_ref[0] = v[first:]


def _rope_tables(pos):
    half = ROT_DIM // 2
    inv = ROPE_THETA ** (-jnp.arange(0, ROT_DIM, 2, dtype=F32) / ROT_DIM)
    ang = pos.astype(F32)[:, None] * inv[None, :]
    cos, sin = jnp.cos(ang), jnp.sin(ang)
    t = pos.shape[0]
    one = jnp.ones((t, HEAD_DIM - ROT_DIM), F32)
    zero = jnp.zeros((t, HEAD_DIM - ROT_DIM), F32)
    zh = jnp.zeros((t, half), F32)
    cos_t = jnp.concatenate([cos, cos, one], 1)
    sa_t = jnp.concatenate([-sin, zh, zero], 1)
    sb_t = jnp.concatenate([zh, sin, zero], 1)
    rep = LANES // HEAD_DIM
    return tuple(jnp.tile(a, (1, rep)) for a in (cos_t, sa_t, sb_t))


def _mod_spec(mod, tiles_per_group):
    return pl.BlockSpec((1,) + mod.shape[1:], lambda i: (i // tiles_per_group, 0, 0))


def _qkv(x, sc, sh, gw, w, l, b, tables, tm, tiles_per_group, table_tiles, tail_rows=0):
    m, d = x.shape
    n = w.shape[-1]
    nq = N_HEADS * HEAD_DIM
    nk = N_KV_HEADS * HEAD_DIM
    row = lambda width: pl.BlockSpec((tm, width), lambda i: (i, 0))
    tab = pl.BlockSpec((tm, LANES), lambda i: (i % table_tiles, 0))
    dt = BF16 if tail_rows else F32
    tail = pl.BlockSpec((1, tail_rows, nk), lambda i: (i // tiles_per_group, 0, 0))
    tail_shape = jax.ShapeDtypeStruct((m // tm // tiles_per_group, tail_rows, nk), F32)
    return pl.pallas_call(
        _qkv_kernel,
        grid=(m // tm,),
        in_specs=[row(d), _mod_spec(sc, tiles_per_group), _mod_spec(sh, tiles_per_group),
                  _resident((1, d)), _layer_spec(w, l), _resident((1, n)), tab, tab, tab],
        out_specs=[row(nq), row(nk), row(nk)] + [tail, tail] * bool(tail_rows),
        out_shape=[jax.ShapeDtypeStruct((m, nq), dt), jax.ShapeDtypeStruct((m, nk), dt),
                   jax.ShapeDtypeStruct((m, nk), dt)] + [tail_shape, tail_shape] * bool(tail_rows),
        compiler_params=_params(1),
        name="qkv",
    )(x, sc, sh, gw, w, b, *tables)


def _dup_half(x, kv):
    lo = lax.broadcasted_iota(jnp.int32, (1, LANES), 1) < HEAD_DIM
    x = x.astype(F32)
    xr = pltpu.roll(x, HEAD_DIM, 1)
    return jnp.where(lo, x, xr) if kv % 2 == 0 else jnp.where(lo, xr, x)


def _masked_queries(q, kv):
    lo = lax.broadcasted_iota(jnp.int32, (1, LANES), 1) < HEAD_DIM
    q0 = q[:, (2 * kv) * LANES:(2 * kv + 1) * LANES]
    q1 = q[:, (2 * kv + 1) * LANES:(2 * kv + 2) * LANES]
    return jnp.concatenate([jnp.where(lo, q0, 0.0), jnp.where(lo, 0.0, q0),
                            jnp.where(lo, q1, 0.0), jnp.where(lo, 0.0, q1)], axis=0)


def _band(tq, nwin):
    i = lax.broadcasted_iota(jnp.int32, (tq, nwin), 0)
    j = lax.broadcasted_iota(jnp.int32, (tq, nwin), 1)
    diff = WINDOW + i - j
    return jnp.concatenate([(diff >= 0) & (diff <= WINDOW)] * Q_PER_KV, axis=0)


def _band_t(tq, nwin, prev_valid):
    j = lax.broadcasted_iota(jnp.int32, (nwin, tq), 0)
    i = lax.broadcasted_iota(jnp.int32, (nwin, tq), 1)
    diff = WINDOW + i - j
    band = (diff >= 0) & (diff <= WINDOW)
    if prev_valid is not None:
        band = band & ((j >= WINDOW) | prev_valid)
    return jnp.concatenate([band] * Q_PER_KV, axis=1)


def _scores_t(q, kwin, kv):
    kd = _dup_half(kwin[:, (kv // 2) * LANES:(kv // 2 + 1) * LANES], kv)
    return lax.dot_general(kd.astype(BF16), _masked_queries(q, kv).astype(BF16), (((1,), (1,)), ((), ())),
                           preferred_element_type=F32)


def _softmax_pv_t(s, band, vwin, kv, sink_ref):
    tq = s.shape[1] // Q_PER_KV
    nwin = s.shape[0]
    blk = kv // 2
    vt = vwin[:, blk * LANES:(blk + 1) * LANES].astype(F32).T
    vt = jnp.concatenate([vt[(kv % 2) * HEAD_DIM:(kv % 2 + 1) * HEAD_DIM],
                          jnp.ones((2 * SUBLANES, nwin), F32)], axis=0)
    sink = jnp.concatenate([jnp.full((1, tq), sink_ref[Q_PER_KV * kv + g], F32)
                            for g in range(Q_PER_KV)], axis=1)
    s = jnp.where(band, s, NEG_BIG)
    mx = jnp.maximum(jnp.max(s, axis=0, keepdims=True), sink)
    p = jnp.exp2(s - mx)
    o = jnp.dot(vt.astype(BF16), p.astype(BF16), preferred_element_type=F32)
    den = o[HEAD_DIM:HEAD_DIM + 1] + jnp.exp2(sink - mx)
    o = o[:HEAD_DIM] / den
    return [o[:, g * tq:(g + 1) * tq] for g in range(Q_PER_KV)]


def _attn_prompt_kernel(sink_ref, q_ref, ko_ref, kp_ref, vo_ref, vp_ref, wo_ref, x_ref, g1_ref, o_ref):
    nblk = q_ref.shape[0] // WINDOW
    kall = jnp.concatenate([kp_ref[...], ko_ref[...]], axis=0)
    vall = jnp.concatenate([vp_ref[...], vo_ref[...]], axis=0)
    band_first = _band_t(WINDOW, 2 * WINDOW, pl.program_id(1) > 0)
    band_rest = _band_t(WINDOW, 2 * WINDOW, None) if nblk > 1 else None
    heads = {c: [] for c in range(nblk)}

    def finish(c, kv, s):
        win = slice(c * WINDOW, (c + 2) * WINDOW)
        heads[c].extend(_softmax_pv_t(s, band_first if c == 0 else band_rest, vall[win], kv, sink_ref))
        if kv == N_KV_HEADS - 1:
            rows = slice(c * WINDOW, (c + 1) * WINDOW)
            ot = jnp.concatenate(heads[c], axis=0)
            mix = lax.dot_general(ot.astype(BF16), wo_ref[...], (((0,), (0,)), ((), ())),
                                  preferred_element_type=F32)
            o_ref[rows, :] = x_ref[rows, :] + g1_ref[0] * mix

    pending = []
    for c in range(nblk):
        for kv in range(N_KV_HEADS):
            s = _scores_t(q_ref[c * WINDOW:(c + 1) * WINDOW, :], kall[c * WINDOW:(c + 2) * WINDOW], kv)
            pending.append((c, kv, s))
            if len(pending) > ATTN_SCORES_AHEAD:
                finish(*pending.pop(0))
    for unit in pending:
        finish(*unit)


def _attn_prompt(sink, q, k, v, wo, l, x, g1, batch, nblk):
    m, d = x.shape
    tq = nblk * WINDOW
    nb = m // batch // tq
    nk = k.shape[1]
    own = lambda width: pl.BlockSpec((tq, width), lambda b, n: (b * nb + n, 0))
    prev = lambda width: pl.BlockSpec(
        (WINDOW, width), lambda b, n: (b * nb * nblk + jnp.maximum(n * nblk - 1, 0), 0))
    return pl.pallas_call(
        _attn_prompt_kernel,
        grid=(batch, nb),
        in_specs=[pl.BlockSpec(memory_space=pltpu.SMEM), own(q.shape[1]), own(nk), prev(nk), own(nk), prev(nk),
                  _layer_spec(wo, l),
                  own(d), pl.BlockSpec((1, 1, d), lambda b, n: (b, 0, 0))],
        out_specs=own(d),
        out_shape=jax.ShapeDtypeStruct((m, d), F32),
        compiler_params=_params(2),
        name="attn_prompt",
    )(sink, q, k, k, v, v, wo, x, g1)


def _attn_sample_kernel(sink_ref, q_ref, kn_ref, vn_ref, kc_ref, vc_ref, wo_ref, x_ref, g1_ref,
                        o_ref, ko_ref, vo_ref, o_scr, *, t_new):
    nseq, _, _, w = kc_ref.shape
    lanes = lax.broadcasted_iota(jnp.int32, (1, w), 1)
    lo = lanes < HEAD_DIM
    is_new = lanes >= w - t_new
    zpad = jnp.zeros((w - t_new, LANES), F32)
    band = _band(t_new, 2 * w)

    def new_rows(x, blk, at_end):
        rows = x[:, blk * LANES:(blk + 1) * LANES]
        return jnp.concatenate([zpad, rows] if at_end else [rows, zpad], axis=0)

    def body(it, carry):
        units = []
        for u in range(SAMPLE_UNROLL):
            b = it * SAMPLE_UNROLL + u
            r = pl.multiple_of(b * t_new, t_new)
            kn, vn = kn_ref[pl.ds(r, t_new), :], vn_ref[pl.ds(r, t_new), :]
            q = q_ref[pl.ds(r, t_new), :]
            knt = [new_rows(kn, blk, True).T for blk in range(N_KV_HEADS // 2)]
            vnt = [new_rows(vn, blk, True).T for blk in range(N_KV_HEADS // 2)]
            scores, vals = [], []
            for kv in range(N_KV_HEADS):
                half = slice((kv % 2) * HEAD_DIM, (kv % 2 + 1) * HEAD_DIM)
                kt, vt = kc_ref[b, kv], vc_ref[b, kv]
                ko_ref[b, kv] = jnp.where(is_new, knt[kv // 2][half], pltpu.roll(kt, w - t_new, 1))
                vo_ref[b, kv] = jnp.where(is_new, vnt[kv // 2][half], pltpu.roll(vt, w - t_new, 1))
                lhs = _masked_queries(q, kv).astype(BF16)
                kd_new = _dup_half(new_rows(kn, kv // 2, False), kv).astype(BF16)
                s_old = jnp.dot(lhs, jnp.concatenate([kt, kt], axis=0).astype(BF16), preferred_element_type=F32)
                s_new = lax.dot_general(lhs, kd_new, (((1,), (1,)), ((), ())), preferred_element_type=F32)
                scores.append(jnp.concatenate([s_old, s_new], axis=1))
                vals.append((jnp.concatenate([vt, vt], axis=0).astype(BF16),
                             _dup_half(new_rows(vn, kv // 2, False), kv).astype(BF16)))
            units.append((r, scores, vals))
        for r, scores, vals in units:
            cols = []
            for kv in range(N_KV_HEADS):
                sink = jnp.concatenate([jnp.full((t_new, 1), sink_ref[Q_PER_KV * kv + g], F32)
                                        for g in range(Q_PER_KV)], axis=0)
                s = jnp.where(band, scores[kv], NEG_BIG)
                mx = jnp.maximum(jnp.max(s, axis=-1, keepdims=True), sink)
                p = jnp.exp2(s - mx)
                den = jnp.sum(p, axis=-1, keepdims=True) + jnp.exp2(sink - mx)
                p = p.astype(BF16)
                vd_old, vd_new = vals[kv]
                o = lax.dot_general(p[:, :w], vd_old, (((1,), (1,)), ((), ())), preferred_element_type=F32)
                o = (o + jnp.dot(p[:, w:], vd_new, preferred_element_type=F32)) / den
                cols.append(jnp.where(lo, o[0:t_new], o[t_new:2 * t_new]))
                cols.append(jnp.where(lo, o[2 * t_new:3 * t_new], o[3 * t_new:4 * t_new]))
            o_scr[pl.ds(r, t_new), :] = jnp.concatenate(cols, axis=1)
        return carry

    lax.fori_loop(0, nseq // SAMPLE_UNROLL, body, 0)
    mix = jnp.dot(o_scr[...].astype(BF16), wo_ref[...], preferred_element_type=F32)
    o_ref[...] = x_ref[...] + g1_ref[...] * mix


def _attn_sample(sink, q, kn, vn, kct, vct, wo, l, x, g1, t_new):
    m, d = x.shape
    _, db, nkv, hd, w = kct.shape
    nk = nkv * hd
    g = WINDOW // t_new
    tm = g * t_new
    row = lambda width: pl.BlockSpec((tm, width), lambda i: (i, 0))
    cache_in = pl.BlockSpec((None, g, nkv, hd, w), lambda i: (l, i, 0, 0, 0))
    cache_out = pl.BlockSpec((g, nkv, hd, w), lambda i: (i, 0, 0, 0))
    cache_shape = jax.ShapeDtypeStruct((db, nkv, hd, w), F32)
    return pl.pallas_call(
        functools.partial(_attn_sample_kernel, t_new=t_new),
        grid=(db // g,),
        in_specs=[pl.BlockSpec(memory_space=pltpu.SMEM), row(q.shape[1]), row(nk), row(nk), cache_in, cache_in,
                  _layer_spec(wo, l), row(d), row(d)],
        out_specs=[row(d), cache_out, cache_out],
        out_shape=[jax.ShapeDtypeStruct((m, d), F32), cache_shape, cache_shape],
        scratch_shapes=[pltpu.VMEM((tm, q.shape[1]), F32)],
        compiler_params=_params(1),
        name="attn_sample",
    )(sink, q, kn, vn, kct, vct, wo, x, g1)


def _sgu_kernel(x_ref, sc_ref, sh_ref, g1_ref, gw_ref, win_ref, bin_ref, lng_ref, lnb_ref,
                wsp_ref, bsp_ref, wout_ref, o_ref, *rest, emit_v):
    v_out = rest[0] if emit_v else None
    h_scr, z_scr, vb_scr, gated_scr = rest[-4:]
    x = x_ref[...]
    tm = x.shape[0]
    d_sgu = lng_ref.shape[1]
    gdim = d_sgu // SGU_GROUPS
    h_scr[...] = _modnorm(x, gw_ref[...], sc_ref[0], sh_ref[0]).astype(BF16)

    def inproj(c0):
        z_scr[:, c0:c0 + gdim] = jnp.dot(h_scr[...], win_ref[:, c0:c0 + gdim], preferred_element_type=F32)

    def gelu_cols(c0):
        bias = jnp.broadcast_to(bin_ref[:, c0:c0 + gdim], (SUBLANES, gdim))
        for r0 in range(0, tm, SGU_ROWS):
            z = z_scr[r0:r0 + SGU_ROWS, c0:c0 + gdim].reshape(SGU_ROWS // SUBLANES, SUBLANES, gdim)
            z_scr[r0:r0 + SGU_ROWS, c0:c0 + gdim] = _gelu_tanh(z + bias).reshape(SGU_ROWS, gdim)

    def layernorm_v():
        g8 = jnp.broadcast_to(lng_ref[...], (SUBLANES, d_sgu))
        b8 = jnp.broadcast_to(lnb_ref[...], (SUBLANES, d_sgu))
        for r0 in range(0, tm, SGU_LN_ROWS):
            v = z_scr[r0:r0 + SGU_LN_ROWS, d_sgu:].reshape(SGU_LN_ROWS // SUBLANES, SUBLANES, d_sgu)
            vc = v - jnp.mean(v, axis=-1, keepdims=True)
            var = jnp.mean(vc * vc, axis=-1, keepdims=True)
            vn = (vc * lax.rsqrt(var + EPS) * g8 + b8).reshape(SGU_LN_ROWS, d_sgu)
            if emit_v:
                v_out[r0:r0 + SGU_LN_ROWS, :] = vn
            vb_scr[r0:r0 + SGU_LN_ROWS, :] = vn.astype(BF16)

    r = lax.broadcasted_iota(jnp.int32, (CHUNK, CHUNK), 0)
    c = lax.broadcasted_iota(jnp.int32, (CHUNK, CHUNK), 1)

    def mix(g):
        cols = slice(g * gdim, (g + 1) * gdim)
        wm = jnp.where(r >= c, wsp_ref[g], 0.0).astype(BF16)
        for ch in range(tm // CHUNK):
            rows = slice(ch * CHUNK, (ch + 1) * CHUNK)
            mixed = jnp.dot(wm, vb_scr[rows, cols], preferred_element_type=F32) + bsp_ref[g]
            gated_scr[rows, cols] = (z_scr[rows, cols] * mixed).astype(BF16)

    acc = []

    def outproj(g):
        cols = slice(g * gdim, (g + 1) * gdim)
        part = jnp.dot(gated_scr[:, cols], wout_ref[cols, :], preferred_element_type=F32)
        acc[:] = [part if not acc else acc[0] + part]

    ucols = [g * gdim for g in range(SGU_GROUPS)]
    vcols = [d_sgu + g * gdim for g in range(SGU_GROUPS)]
    order = vcols + ucols
    for i, c0 in enumerate(order):
        inproj(c0)
        if i >= 2:
            gelu_cols(order[i - 2])
        if i == len(vcols) + 1:
            layernorm_v()
    gelu_cols(order[-2])
    mix(0)
    gelu_cols(order[-1])
    for g in range(SGU_GROUPS):
        if g + 1 < SGU_GROUPS:
            mix(g + 1)
        outproj(g)
    o_ref[...] = x + g1_ref[0] * acc[0]


def _sgu(x, sc, sh, g1, gw, win, wout, l, b_in, lng, lnb, wsp, bsp, tm, tiles_per_group, emit_v):
    m, d = x.shape
    d_sgu = wout.shape[1]
    row = lambda width: pl.BlockSpec((tm, width), lambda i: (i, 0))
    out_specs = [row(d)]
    out_shape = [jax.ShapeDtypeStruct((m, d), F32)]
    if emit_v:
        out_specs.append(row(d_sgu))
        out_shape.append(jax.ShapeDtypeStruct((m, d_sgu), F32))
    return pl.pallas_call(
        functools.partial(_sgu_kernel, emit_v=emit_v),
        grid=(m // tm,),
        in_specs=[row(d), _mod_spec(sc, tiles_per_group), _mod_spec(sh, tiles_per_group),
                  _mod_spec(g1, tiles_per_group), _resident((1, d)), _layer_spec(win, l),
                  _resident(b_in.shape), _resident(lng.shape), _resident(lnb.shape),
                  _resident(wsp.shape), _resident(bsp.shape), _layer_spec(wout, l)],
        out_specs=out_specs,
        out_shape=out_shape,
        scratch_shapes=[pltpu.VMEM((tm, d), BF16), pltpu.VMEM((tm, 2 * d_sgu), F32),
                        pltpu.VMEM((tm, d_sgu), BF16), pltpu.VMEM((tm, d_sgu), BF16)],
        compiler_params=_params(1),
        name="sgu",
    )(x, sc, sh, g1, gw, win, b_in, lng, lnb, wsp, bsp, wout)


def _ffn_chunks(d_ff):
    return [(c, d_ff + c) for c in range(0, d_ff, FF_CHUNK)]


def _final(xn, gf_ref):
    if gf_ref is None:
        return xn
    ms = jnp.mean(xn * xn, axis=-1, keepdims=True)
    return xn * lax.rsqrt(ms + EPS) * gf_ref[...]


def _ffn_prompt_kernel(x_ref, sc_ref, sh_ref, g2_ref, gw_ref, wup_ref, cw_ref, cb_ref, wdn_ref, *rest,
                       tiles_per_seq, final):
    gf_ref = rest[0] if final else None
    o_ref, tail_ref, halo_scr, a_scr, act_scr = rest[-5:]
    d_ff = wdn_ref.shape[0]
    tm, d = x_ref.shape
    seg = tm // SUBLANES
    lead = (CONV_W - 1) * SUBLANES
    nbuf = a_scr.shape[0]

    @pl.when(pl.program_id(0) % tiles_per_seq == 0)
    def _():
        halo_scr[...] = jnp.zeros(halo_scr.shape, F32)

    hs = _modnorm(x_ref[...], gw_ref[...], sc_ref[0], sh_ref[0])
    h = pltpu.einshape("sqd->qsd", hs.reshape(SUBLANES, seg, d)).reshape(tm, d).astype(BF16)
    first = lax.broadcasted_iota(jnp.int32, (SUBLANES, FF_CHUNK), 0) == 0

    def up(buf, half, c0):
        cols = slice(c0, c0 + FF_CHUNK)
        a_scr[buf, half, lead:lead + tm, :] = jnp.dot(h, wup_ref[:, cols], preferred_element_type=F32)
        for k in range(CONV_W - 1):
            last = a_scr[buf, half, lead + tm - (2 - k) * SUBLANES:lead + tm - (1 - k) * SUBLANES, :]
            prev = halo_scr[k * SUBLANES:(k + 1) * SUBLANES, cols]
            a_scr[buf, half, k * SUBLANES:(k + 1) * SUBLANES, :] = jnp.where(
                first, pltpu.roll(prev, 1, 0), pltpu.roll(last, 1, 0))
            halo_scr[k * SUBLANES:(k + 1) * SUBLANES, cols] = last

    def conv_taps(c0):
        cols = slice(c0, c0 + FF_CHUNK)
        full = lambda row: jnp.broadcast_to(row, (SUBLANES, FF_CHUNK))
        return [full(cw_ref[k:k + 1, cols]) for k in range(CONV_W)] + [full(cb_ref[:, cols])]

    def conv_rows(buf, half, taps, r0):
        blk = lambda k: a_scr[buf, half, r0 + k * SUBLANES:r0 + k * SUBLANES + FF_ROWS, :].reshape(
            FF_ROWS // SUBLANES, SUBLANES, FF_CHUNK)
        w0, w1, w2, b = taps
        return w0 * blk(0) + w1 * blk(1) + w2 * blk(2) + b

    def gate(buf, cg, cu):
        taps_g, taps_u = conv_taps(cg), conv_taps(cu)
        for r0 in range(0, tm, FF_ROWS):
            act = _silu(conv_rows(buf, 0, taps_g, r0)) * conv_rows(buf, 1, taps_u, r0)
            act_scr[r0:r0 + FF_ROWS, cg:cg + FF_CHUNK] = act.reshape(FF_ROWS, FF_CHUNK).astype(BF16)

    chunks = _ffn_chunks(d_ff)
    acc = []

    def down(c0, c1):
        dn = jnp.dot(act_scr[:, c0:c1], wdn_ref[c0:c1, :], preferred_element_type=F32)
        acc[:] = [dn if not acc else acc[0] + dn]

    pending, gated = [], 0
    for j in range(len(chunks) + FFN_UP_AHEAD):
        if j < len(chunks):
            buf, (cg, cu) = j % nbuf, chunks[j]
            up(buf, 0, cg)
            up(buf, 1, cu)
            pending.append((buf, cg, cu))
        if j >= FFN_UP_AHEAD:
            gate(*pending.pop(0))
            gated += 1
            if gated % FFN_DOWN_GROUP == 0 or gated == len(chunks):
                lo = (gated - 1) // FFN_DOWN_GROUP * FFN_DOWN_GROUP
                down(lo * FF_CHUNK, gated * FF_CHUNK)
    tail_ref[0] = halo_scr[...]
    y = pltpu.einshape("qsd->sqd", acc[0].reshape(seg, SUBLANES, d)).reshape(tm, d)
    o_ref[...] = _final(x_ref[...] + g2_ref[0] * y, gf_ref)


def _ffn_prompt(x, sc, sh, g2, gw, wup, wdn, l, cw, cb, gf, batch, tm):
    m, d = x.shape
    c2 = wup.shape[-1]
    tiles_per_seq = m // batch // tm
    lead = (CONV_W - 1) * SUBLANES
    row = pl.BlockSpec((tm, d), lambda i: (i, 0))
    final = gf is not None
    in_specs = [row, _mod_spec(sc, tiles_per_seq), _mod_spec(sh, tiles_per_seq), _mod_spec(g2, tiles_per_seq),
                _resident((1, d)), _layer_spec(wup, l), _resident(cw.shape), _resident(cb.shape),
                _layer_spec(wdn, l)]
    args = [x, sc, sh, g2, gw, wup, cw, cb, wdn]
    if final:
        in_specs.append(_resident((1, d)))
        args.append(gf)
    return pl.pallas_call(
        functools.partial(_ffn_prompt_kernel, tiles_per_seq=tiles_per_seq, final=final),
        grid=(m // tm,),
        in_specs=in_specs,
        out_specs=[row, pl.BlockSpec((1, lead, c2), lambda i: (i // tiles_per_seq, 0, 0))],
        out_shape=[jax.ShapeDtypeStruct((m, d), F32), jax.ShapeDtypeStruct((batch, lead, c2), F32)],
        scratch_shapes=[pltpu.VMEM((lead, c2), F32),
                        pltpu.VMEM((FFN_UP_AHEAD + 1, 2, tm + lead, FF_CHUNK), F32),
                        pltpu.VMEM((tm, c2 // 2), BF16)],
        compiler_params=_params(1),
        name="ffn_prompt",
    )(*args)


def _ffn_sample_kernel(x_ref, sc_ref, sh_ref, g2_ref, gw_ref, st_ref, wup_ref, cw_ref, cb_ref, wdn_ref, *rest,
                       final):
    gf_ref = rest[0] if final else None
    o_ref, nst_ref, h_scr, a_scr, act_scr = rest[-5:]
    nb, d = sc_ref.shape
    t_new = x_ref.shape[1] // d
    rows = t_new * nb
    keep = CONV_W - 1
    d_ff = wdn_ref.shape[0]
    c2 = 2 * d_ff
    nbuf = a_scr.shape[0]

    for t in range(t_new):
        xt = x_ref[:, t * d:(t + 1) * d]
        h_scr[t * nb:(t + 1) * nb, :] = _modnorm(xt, gw_ref[...], sc_ref[...], sh_ref[...]).astype(BF16)

    def up(buf, half, c0):
        cols = slice(c0, c0 + FF_CHUNK)
        for r in range(keep):
            a_scr[buf, half, r * nb:(r + 1) * nb, :] = st_ref[:, r * c2 + c0:r * c2 + c0 + FF_CHUNK]
        a_scr[buf, half, keep * nb:keep * nb + rows, :] = jnp.dot(h_scr[...], wup_ref[:, cols],
                                                                  preferred_element_type=F32)
        for r in range(keep):
            nst_ref[:, r * c2 + c0:r * c2 + c0 + FF_CHUNK] = a_scr[buf, half, rows + r * nb:rows + (r + 1) * nb, :]

    def conv_taps(c0):
        cols = slice(c0, c0 + FF_CHUNK)
        full = lambda row: jnp.broadcast_to(row, (SUBLANES, FF_CHUNK))
        return [full(cw_ref[k:k + 1, cols]) for k in range(CONV_W)] + [full(cb_ref[:, cols])]

    def conv_rows(buf, half, taps, r0):
        blk = lambda k: a_scr[buf, half, r0 + k * nb:r0 + k * nb + FF_ROWS, :].reshape(
            FF_ROWS // SUBLANES, SUBLANES, FF_CHUNK)
        w0, w1, w2, b = taps
        return w0 * blk(0) + w1 * blk(1) + w2 * blk(2) + b

    def gate(buf, cg, cu):
        taps_g, taps_u = conv_taps(cg), conv_taps(cu)
        for r0 in range(0, rows, FF_ROWS):
            act = _silu(conv_rows(buf, 0, taps_g, r0)) * conv_rows(buf, 1, taps_u, r0)
            act_scr[r0:r0 + FF_ROWS, cg:cg + FF_CHUNK] = act.reshape(FF_ROWS, FF_CHUNK).astype(BF16)

    chunks = _ffn_chunks(d_ff)
    acc = []

    def down(c0, c1):
        dn = jnp.dot(act_scr[:, c0:c1], wdn_ref[c0:c1, :], preferred_element_type=F32)
        acc[:] = [dn if not acc else acc[0] + dn]

    pending, gated = [], 0
    for j in range(len(chunks) + FFN_UP_AHEAD):
        if j < len(chunks):
            buf, (cg, cu) = j % nbuf, chunks[j]
            up(buf, 0, cg)
            up(buf, 1, cu)
            pending.append((buf, cg, cu))
        if j >= FFN_UP_AHEAD:
            gate(*pending.pop(0))
            gated += 1
            if gated % FFN_DOWN_GROUP == 0 or gated == len(chunks):
                first = (gated - 1) // FFN_DOWN_GROUP * FFN_DOWN_GROUP
                down(first * FF_CHUNK, gated * FF_CHUNK)
    for t in range(t_new):
        xt = x_ref[:, t * d:(t + 1) * d]
        o_ref[:, t * d:(t + 1) * d] = _final(xt + g2_ref[...] * acc[0][t * nb:(t + 1) * nb, :], gf_ref)


def _ffn_sample(x2, sc, sh, g2, gw, st2, wup, wdn, l, cw, cb, gf, nb):
    db, d = sc.shape
    t_new = x2.shape[1] // d
    c2 = wup.shape[-1]
    keep = CONV_W - 1
    rows = t_new * nb
    final = gf is not None
    seqs = lambda width: pl.BlockSpec((nb, width), lambda i: (i, 0))
    in_specs = [seqs(t_new * d), seqs(d), seqs(d), seqs(d), _resident((1, d)), seqs(keep * c2),
                _layer_spec(wup, l), _resident(cw.shape), _resident(cb.shape), _layer_spec(wdn, l)]
    args = [x2, sc, sh, g2, gw, st2, wup, cw, cb, wdn]
    if final:
        in_specs.append(_resident((1, d)))
        args.append(gf)
    return pl.pallas_call(
        functools.partial(_ffn_sample_kernel, final=final),
        grid=(db // nb,),
        in_specs=in_specs,
        out_specs=[seqs(t_new * d), seqs(keep * c2)],
        out_shape=[jax.ShapeDtypeStruct((db, t_new * d), F32), jax.ShapeDtypeStruct((db, keep * c2), F32)],
        scratch_shapes=[pltpu.VMEM((rows, d), BF16),
                        pltpu.VMEM((FFN_UP_AHEAD + 1, 2, rows + keep * nb, FF_CHUNK), F32),
                        pltpu.VMEM((rows, c2 // 2), BF16)],
        compiler_params=_params(1),
        name="ffn_sample",
    )(*args)


def _tile_rows(size, want):
    return want if size % want == 0 else size


def kernel(x_prompt, x_sample, c_prompt, c_sample, cache_k, cache_v, state_conv, w_ada, b_ada, norm_mix,
           norm_ffn, w_qkv, b_qkv, attn_sink, w_o, w_sgu_in, b_sgu_in, sgu_ln_g, sgu_ln_b, w_spatial,
           b_spatial, w_sgu_out, w_up, conv_w, conv_b, w_down, norm_final):
    batch, seq, d = x_prompt.shape
    db, t_new, _ = x_sample.shape
    depth = w_ada.shape[0]

    pad = (-(batch + db)) % SUBLANES
    c_all = jnp.concatenate([c_prompt, c_sample, jnp.zeros((pad, d), F32)], axis=0)
    mod = _ada(c_all, w_ada, b_ada)

    def mod_part(l, k, lo, hi):
        return mod[l, lo:hi, k * d:(k + 1) * d]

    mp, ms = x_prompt.shape[0] * seq, db * t_new
    xp = x_prompt.reshape(mp, d)
    xs = x_sample.reshape(ms, d)

    tm_qkv_p, tm_sgu_p, tm_ffn_p = _tile_rows(seq, 512), _tile_rows(seq, 512), _tile_rows(seq, 512)
    tm_qkv_s, tm_sgu_s = _tile_rows(ms, 512), _tile_rows(ms, 256)
    nb_ffn_s = _tile_rows(db, 512 // t_new)

    tab_p = _rope_tables(jnp.arange(seq))
    tab_s = _rope_tables(PAST_LEN + (jnp.arange(ms) % t_new))

    wq_all, wo_all = w_qkv.astype(BF16), w_o.astype(BF16)
    win_all, wout_all = w_sgu_in.astype(BF16), w_sgu_out.astype(BF16)
    wup_all, wdn_all = w_up.astype(BF16), w_down.astype(BF16)

    kct = jnp.transpose(cache_k, (0, 1, 3, 4, 2))
    vct = jnp.transpose(cache_v, (0, 1, 3, 4, 2))

    new_k_p, new_v_p, new_conv_p = [], [], []
    new_k_s, new_v_s, new_conv_s, new_sgu_s = [], [], [], []

    for l in range(depth):
        idx = l // N_MIXERS
        gmix = norm_mix[l].reshape(1, d)
        gffn = norm_ffn[l].reshape(1, d)
        p_mod = [mod_part(l, k, 0, batch)[:, None, :] for k in range(6)]
        s_seq = [mod_part(l, k, batch, batch + db) for k in range(6)]
        s_row = [jnp.repeat(a, t_new, axis=0) for a in s_seq[:3]]

        if l % N_MIXERS == 0:
            bq = b_qkv[idx].reshape(1, -1)
            sink = attn_sink[idx] * LOG2E
            keep = min(WINDOW, seq)
            q, k, v, k_last, v_last = _qkv(xp, p_mod[1], p_mod[0], gmix, wq_all, idx, bq, tab_p, tm_qkv_p,
                                           seq // tm_qkv_p, seq // tm_qkv_p, tail_rows=keep)
            nblk = ATTN_BLOCKS_PER_STEP if seq % (ATTN_BLOCKS_PER_STEP * WINDOW) == 0 else 1
            xp = _attn_prompt(sink, q, k, v, wo_all, idx, xp, p_mod[2], batch, nblk)
            new_k_p.append(k_last.reshape(batch, keep, N_KV_HEADS, HEAD_DIM))
            new_v_p.append(v_last.reshape(batch, keep, N_KV_HEADS, HEAD_DIM))

            tile3 = lambda a, tm: a.reshape(ms // tm, tm, d)
            q, k, v = _qkv(xs, tile3(s_row[1], tm_qkv_s), tile3(s_row[0], tm_qkv_s), gmix, wq_all, idx, bq, tab_s,
                           tm_qkv_s, 1, ms // tm_qkv_s)
            xs, nk, nv = _attn_sample(sink, q, k, v, kct, vct, wo_all, idx, xs, s_row[2], t_new)
            new_k_s.append(nk)
            new_v_s.append(nv)
        else:
            b_in = b_sgu_in[idx].reshape(1, -1)
            lng = sgu_ln_g[idx].reshape(1, -1)
            lnb = sgu_ln_b[idx].reshape(1, -1)
            tc = min(seq, CHUNK)
            (xp,) = _sgu(xp, p_mod[1], p_mod[0], p_mod[2], gmix, win_all, wout_all, idx, b_in, lng, lnb,
                         w_spatial[idx][:, :tc, :tc], b_spatial[idx][:, :tc, None],
                         tm_sgu_p, seq // tm_sgu_p, False)
            reps = CHUNK // t_new
            eye = jnp.eye(reps, dtype=F32)
            wsp_s = jnp.einsum("ab,gts->gatbs", eye, w_spatial[idx][:, :t_new, :t_new]).reshape(
                SGU_GROUPS, CHUNK, CHUNK)
            bsp_s = jnp.tile(b_spatial[idx][:, :t_new], (1, reps))[:, :, None]
            tile3 = lambda a: a.reshape(ms // tm_sgu_s, tm_sgu_s, d)
            xs, vrows = _sgu(xs, tile3(s_row[1]), tile3(s_row[0]), tile3(s_row[2]), gmix, win_all, wout_all, idx,
                             b_in, lng, lnb, wsp_s, bsp_s, tm_sgu_s, 1, True)
            new_sgu_s.append(vrows.reshape(db, t_new, -1))

        cw = conv_w[l]
        cb = conv_b[l].reshape(1, -1)
        gf = norm_final.reshape(1, d) if l == depth - 1 else None
        xp, tail = _ffn_prompt(xp, p_mod[4], p_mod[3], p_mod[5], gffn, wup_all, wdn_all, l, cw, cb, gf, batch,
                               tm_ffn_p)
        new_conv_p.append(tail[:, SUBLANES - 1::SUBLANES, :])

        xs2, nst = _ffn_sample(xs.reshape(db, t_new * d), s_seq[4], s_seq[3], s_seq[5], gffn,
                               state_conv[l].reshape(db, -1), wup_all, wdn_all, l, cw, cb, gf, nb_ffn_s)
        xs = xs2.reshape(ms, d)
        new_conv_s.append(nst.reshape(db, CONV_W - 1, -1))

    return (xp.reshape(batch, seq, d), xs.reshape(db, t_new, d),
            jnp.stack(new_k_p), jnp.stack(new_v_p), jnp.stack(new_conv_p),
            jnp.transpose(jnp.stack(new_k_s), (0, 1, 4, 2, 3)), jnp.transpose(jnp.stack(new_v_s), (0, 1, 4, 2, 3)),
            jnp.stack(new_conv_s), jnp.stack(new_sgu_s))
```

```python
import functools

import jax
import jax.numpy as jnp
from jax import lax
from jax.experimental import pallas as pl
from jax.experimental.pallas import tpu as pltpu

N_HEADS = 16
N_KV_HEADS = 4
HEAD_DIM = 64
Q_PER_KV = N_HEADS // N_KV_HEADS
WINDOW = 128
ROT_DIM = HEAD_DIM // 4
ROPE_THETA = 500000.0
CHUNK = 128
SGU_GROUPS = 4
CONV_W = 3
EPS = 1e-6
N_MIXERS = 2
PAST_LEN = 8192

LANES = 128
SUBLANES = 8
FF_CHUNK = 256
ATTN_BLOCKS_PER_STEP = 4
SGU_ROWS = 64
SGU_LN_ROWS = 16
FF_ROWS = 64
FFN_UP_AHEAD = 3
FFN_DOWN_GROUP = 4
ATTN_SCORES_AHEAD = 3
SAMPLE_UNROLL = 4
LOG2E = 1.4426950408889634
Q_SCALE = HEAD_DIM ** -0.5 * LOG2E
VMEM_LIMIT = 56 * 1024 * 1024
NEG_BIG = -1e30

F32 = jnp.float32
BF16 = jnp.bfloat16


def _params(n_axes=1, vmem=VMEM_LIMIT):
    return pltpu.CompilerParams(dimension_semantics=("arbitrary",) * n_axes, vmem_limit_bytes=vmem)


def _resident(shape):
    nd = len(shape)
    return pl.BlockSpec(shape, lambda *_: (0,) * nd, pipeline_mode=pl.Buffered(1))


def _layer_spec(stacked, l):
    nd = stacked.ndim - 1
    return pl.BlockSpec((None,) + stacked.shape[1:], lambda *_: (l,) + (0,) * nd, pipeline_mode=pl.Buffered(1))


def _per_row(m, rows):
    g, d = m.shape
    if g in (1, rows):
        return m
    return jnp.broadcast_to(m[:, None, :], (g, rows // g, d)).reshape(rows, d)


def _modnorm(x, gw, scale, shift):
    ms = jnp.mean(x * x, axis=-1, keepdims=True)
    y = x * lax.rsqrt(ms + EPS) * gw
    return y * (1.0 + _per_row(scale, x.shape[0])) + _per_row(shift, x.shape[0])


def _silu(x):
    return x * jax.nn.sigmoid(x)


def _gelu_tanh(x):
    c = 0.7978845608028654
    hx = 0.5 * x
    return hx + hx * jnp.tanh(x * (c + (c * 0.044715) * (x * x)))


def _ada_kernel(c_ref, w_ref, b_ref, o_ref):
    s = _silu(c_ref[...]).astype(BF16)
    o_ref[0] = jnp.dot(s, w_ref[0].astype(BF16), preferred_element_type=F32) + b_ref[0]


def _ada(c_all, w_ada, b_ada):
    depth, d, n6 = w_ada.shape
    rows = c_all.shape[0]
    tn = 1024
    return pl.pallas_call(
        _ada_kernel,
        grid=(depth, n6 // tn),
        in_specs=[
            pl.BlockSpec((rows, d), lambda l, n: (0, 0)),
            pl.BlockSpec((1, d, tn), lambda l, n: (l, 0, n)),
            pl.BlockSpec((1, 1, tn), lambda l, n: (l, 0, n)),
        ],
        out_specs=pl.BlockSpec((1, rows, tn), lambda l, n: (l, 0, n)),
        out_shape=jax.ShapeDtypeStruct((depth, rows, n6), F32),
        compiler_params=_params(2),
        name="ada",
    )(c_all, w_ada, b_ada.reshape(depth, 1, n6))


def _qkv_kernel(x_ref, sc_ref, sh_ref, gw_ref, w_ref, b_ref, cos_ref, sa_ref, sb_ref,
                q_ref, k_ref, v_ref, *tails):
    h = _modnorm(x_ref[...], gw_ref[...], sc_ref[0], sh_ref[0]).astype(BF16)
    y = jnp.dot(h, w_ref[...], preferred_element_type=F32) + b_ref[...]
    cos, sa, sb = cos_ref[...], sa_ref[...], sb_ref[...]
    nq = N_HEADS * HEAD_DIM
    nk = N_KV_HEADS * HEAD_DIM

    def rope(yb):
        return yb * cos + pltpu.roll(yb, LANES - ROT_DIM // 2, 1) * sa + pltpu.roll(yb, ROT_DIM // 2, 1) * sb

    for j in range(nq // LANES):
        q_ref[:, j * LANES:(j + 1) * LANES] = (rope(y[:, j * LANES:(j + 1) * LANES]) * Q_SCALE).astype(q_ref.dtype)
    k = [rope(y[:, nq + j * LANES:nq + (j + 1) * LANES]) for j in range(nk // LANES)]
    v = y[:, nq + nk:]
    for j in range(nk // LANES):
        k_ref[:, j * LANES:(j + 1) * LANES] = k[j].astype(k_ref.dtype)
    v_ref[...] = v.astype(v_ref.dtype)
    if tails:
        kt_ref, vt_ref = tails
        first = y.shape[0] - kt_ref.shape[1]
        for j in range(nk // LANES):
            kt_ref[0, :, j * LANES:(j + 1) * LANES] = k[j][first:]
        vt_ref[0] = v[first:]


def _rope_tables(pos):
    half = ROT_DIM // 2
    inv = ROPE_THETA ** (-jnp.arange(0, ROT_DIM, 2, dtype=F32) / ROT_DIM)
    ang = pos.astype(F32)[:, None] * inv[None, :]
    cos, sin = jnp.cos(ang), jnp.sin(ang)
    t = pos.shape[0]
    one = jnp.ones((t, HEAD_DIM - ROT_DIM), F32)
    zero = jnp.zeros((t, HEAD_DIM - ROT_DIM), F32)
    zh = jnp.zeros((t, half), F32)
    cos_t = jnp.concatenate([cos, cos, one], 1)
    sa_t = jnp.concatenate([-sin, zh, zero], 1)
    sb_t = jnp.concatenate([zh, sin, zero], 1)
    rep = LANES // HEAD_DIM
    return tuple(jnp.tile(a, (1, rep)) for a in (cos_t, sa_t, sb_t))


def _mod_spec(mod, tiles_per_group):
    return pl.BlockSpec((1,) + mod.shape[1:], lambda i: (i // tiles_per_group, 0, 0))


def _qkv(x, sc, sh, gw, w, l, b, tables, tm, tiles_per_group, table_tiles, tail_rows=0):
    m, d = x.shape
    n = w.shape[-1]
    nq = N_HEADS * HEAD_DIM
    nk = N_KV_HEADS * HEAD_DIM
    row = lambda width: pl.BlockSpec((tm, width), lambda i: (i, 0))
    tab = pl.BlockSpec((tm, LANES), lambda i: (i % table_tiles, 0))
    dt = BF16 if tail_rows else F32
    tail = pl.BlockSpec((1, tail_rows, nk), lambda i: (i // tiles_per_group, 0, 0))
    tail_shape = jax.ShapeDtypeStruct((m // tm // tiles_per_group, tail_rows, nk), F32)
    return pl.pallas_call(
        _qkv_kernel,
        grid=(m // tm,),
        in_specs=[row(d), _mod_spec(sc, tiles_per_group), _mod_spec(sh, tiles_per_group),
                  _resident((1, d)), _layer_spec(w, l), _resident((1, n)), tab, tab, tab],
        out_specs=[row(nq), row(nk), row(nk)] + [tail, tail] * bool(tail_rows),
        out_shape=[jax.ShapeDtypeStruct((m, nq), dt), jax.ShapeDtypeStruct((m, nk), dt),
                   jax.ShapeDtypeStruct((m, nk), dt)] + [tail_shape, tail_shape] * bool(tail_rows),
        compiler_params=_params(1),
        name="qkv",
    )(x, sc, sh, gw, w, b, *tables)


def _dup_half(x, kv):
    lo = lax.broadcasted_iota(jnp.int32, (1, LANES), 1) < HEAD_DIM
    x = x.astype(F32)
    xr = pltpu.roll(x, HEAD_DIM, 1)
    return jnp.where(lo, x, xr) if kv % 2 == 0 else jnp.where(lo, xr, x)


def _masked_queries(q, kv):
    lo = lax.broadcasted_iota(jnp.int32, (1, LANES), 1) < HEAD_DIM
    q0 = q[:, (2 * kv) * LANES:(2 * kv + 1) * LANES]
    q1 = q[:, (2 * kv + 1) * LANES:(2 * kv + 2) * LANES]
    return jnp.concatenate([jnp.where(lo, q0, 0.0), jnp.where(lo, 0.0, q0),
                            jnp.where(lo, q1, 0.0), jnp.where(lo, 0.0, q1)], axis=0)


def _band(tq, nwin):
    i = lax.broadcasted_iota(jnp.int32, (tq, nwin), 0)
    j = lax.broadcasted_iota(jnp.int32, (tq, nwin), 1)
    diff = WINDOW + i - j
    return jnp.concatenate([(diff >= 0) & (diff <= WINDOW)] * Q_PER_KV, axis=0)


def _band_t(tq, nwin, prev_valid):
    j = lax.broadcasted_iota(jnp.int32, (nwin, tq), 0)
    i = lax.broadcasted_iota(jnp.int32, (nwin, tq), 1)
    diff = WINDOW + i - j
    band = (diff >= 0) & (diff <= WINDOW)
    if prev_valid is not None:
        band = band & ((j >= WINDOW) | prev_valid)
    return jnp.concatenate([band] * Q_PER_KV, axis=1)


def _scores_t(q, kwin, kv):
    kd = _dup_half(kwin[:, (kv // 2) * LANES:(kv // 2 + 1) * LANES], kv)
    return lax.dot_general(kd.astype(BF16), _masked_queries(q, kv).astype(BF16), (((1,), (1,)), ((), ())),
                           preferred_element_type=F32)


def _softmax_pv_t(s, band, vwin, kv, sink_ref):
    tq = s.shape[1] // Q_PER_KV
    nwin = s.shape[0]
    blk = kv // 2
    vt = vwin[:, blk * LANES:(blk + 1) * LANES].astype(F32).T
    vt = jnp.concatenate([vt[(kv % 2) * HEAD_DIM:(kv % 2 + 1) * HEAD_DIM],
                          jnp.ones((2 * SUBLANES, nwin), F32)], axis=0)
    sink = jnp.concatenate([jnp.full((1, tq), sink_ref[Q_PER_KV * kv + g], F32)
                            for g in range(Q_PER_KV)], axis=1)
    s = jnp.where(band, s, NEG_BIG)
    mx = jnp.maximum(jnp.max(s, axis=0, keepdims=True), sink)
    p = jnp.exp2(s - mx)
    o = jnp.dot(vt.astype(BF16), p.astype(BF16), preferred_element_type=F32)
    den = o[HEAD_DIM:HEAD_DIM + 1] + jnp.exp2(sink - mx)
    o = o[:HEAD_DIM] / den
    return [o[:, g * tq:(g + 1) * tq] for g in range(Q_PER_KV)]


def _attn_prompt_kernel(sink_ref, q_ref, ko_ref, kp_ref, vo_ref, vp_ref, wo_ref, x_ref, g1_ref, o_ref):
    nblk = q_ref.shape[0] // WINDOW
    kall = jnp.concatenate([kp_ref[...], ko_ref[...]], axis=0)
    vall = jnp.concatenate([vp_ref[...], vo_ref[...]], axis=0)
    band_first = _band_t(WINDOW, 2 * WINDOW, pl.program_id(1) > 0)
    band_rest = _band_t(WINDOW, 2 * WINDOW, None) if nblk > 1 else None
    heads = {c: [] for c in range(nblk)}

    def finish(c, kv, s):
        win = slice(c * WINDOW, (c + 2) * WINDOW)
        heads[c].extend(_softmax_pv_t(s, band_first if c == 0 else band_rest, vall[win], kv, sink_ref))
        if kv == N_KV_HEADS - 1:
            rows = slice(c * WINDOW, (c + 1) * WINDOW)
            ot = jnp.concatenate(heads[c], axis=0)
            mix = lax.dot_general(ot.astype(BF16), wo_ref[...], (((0,), (0,)), ((), ())),
                                  preferred_element_type=F32)
            o_ref[rows, :] = x_ref[rows, :] + g1_ref[0] * mix

    pending = []
    for c in range(nblk):
        for kv in range(N_KV_HEADS):
            s = _scores_t(q_ref[c * WINDOW:(c + 1) * WINDOW, :], kall[c * WINDOW:(c + 2) * WINDOW], kv)
            pending.append((c, kv, s))
            if len(pending) > ATTN_SCORES_AHEAD:
                finish(*pending.pop(0))
    for unit in pending:
        finish(*unit)


def _attn_prompt(sink, q, k, v, wo, l, x, g1, batch, nblk):
    m, d = x.shape
    tq = nblk * WINDOW
    nb = m // batch // tq
    nk = k.shape[1]
    own = lambda width: pl.BlockSpec((tq, width), lambda b, n: (b * nb + n, 0))
    prev = lambda width: pl.BlockSpec(
        (WINDOW, width), lambda b, n: (b * nb * nblk + jnp.maximum(n * nblk - 1, 0), 0))
    return pl.pallas_call(
        _attn_prompt_kernel,
        grid=(batch, nb),
        in_specs=[pl.BlockSpec(memory_space=pltpu.SMEM), own(q.shape[1]), own(nk), prev(nk), own(nk), prev(nk),
                  _layer_spec(wo, l),
                  own(d), pl.BlockSpec((1, 1, d), lambda b, n: (b, 0, 0))],
        out_specs=own(d),
        out_shape=jax.ShapeDtypeStruct((m, d), F32),
        compiler_params=_params(2),
        name="attn_prompt",
    )(sink, q, k, k, v, v, wo, x, g1)


def _attn_sample_kernel(sink_ref, q_ref, kn_ref, vn_ref, kc_ref, vc_ref, wo_ref, x_ref, g1_ref,
                        o_ref, ko_ref, vo_ref, o_scr, *, t_new):
    nseq, _, _, w = kc_ref.shape
    lanes = lax.broadcasted_iota(jnp.int32, (1, w), 1)
    lo = lanes < HEAD_DIM
    is_new = lanes >= w - t_new
    zpad = jnp.zeros((w - t_new, LANES), F32)
    band = _band(t_new, 2 * w)

    def new_rows(x, blk, at_end):
        rows = x[:, blk * LANES:(blk + 1) * LANES]
        return jnp.concatenate([zpad, rows] if at_end else [rows, zpad], axis=0)

    def body(it, carry):
        units = []
        for u in range(SAMPLE_UNROLL):
            b = it * SAMPLE_UNROLL + u
            r = pl.multiple_of(b * t_new, t_new)
            kn, vn = kn_ref[pl.ds(r, t_new), :], vn_ref[pl.ds(r, t_new), :]
            q = q_ref[pl.ds(r, t_new), :]
            knt = [new_rows(kn, blk, True).T for blk in range(N_KV_HEADS // 2)]
            vnt = [new_rows(vn, blk, True).T for blk in range(N_KV_HEADS // 2)]
            scores, vals = [], []
            for kv in range(N_KV_HEADS):
                half = slice((kv % 2) * HEAD_DIM, (kv % 2 + 1) * HEAD_DIM)
                kt, vt = kc_ref[b, kv], vc_ref[b, kv]
                ko_ref[b, kv] = jnp.where(is_new, knt[kv // 2][half], pltpu.roll(kt, w - t_new, 1))
                vo_ref[b, kv] = jnp.where(is_new, vnt[kv // 2][half], pltpu.roll(vt, w - t_new, 1))
                lhs = _masked_queries(q, kv).astype(BF16)
                kd_new = _dup_half(new_rows(kn, kv // 2, False), kv).astype(BF16)
                s_old = jnp.dot(lhs, jnp.concatenate([kt, kt], axis=0).astype(BF16), preferred_element_type=F32)
                s_new = lax.dot_general(lhs, kd_new, (((1,), (1,)), ((), ())), preferred_element_type=F32)
                scores.append(jnp.concatenate([s_old, s_new], axis=1))
                vals.append((jnp.concatenate([vt, vt], axis=0).astype(BF16),
                             _dup_half(new_rows(vn, kv // 2, False), kv).astype(BF16)))
            units.append((r, scores, vals))
        for r, scores, vals in units:
            cols = []
            for kv in range(N_KV_HEADS):
                sink = jnp.concatenate([jnp.full((t_new, 1), sink_ref[Q_PER_KV * kv + g], F32)
                                        for g in range(Q_PER_KV)], axis=0)
                s = jnp.where(band, scores[kv], NEG_BIG)
                mx = jnp.maximum(jnp.max(s, axis=-1, keepdims=True), sink)
                p = jnp.exp2(s - mx)
                den = jnp.sum(p, axis=-1, keepdims=True) + jnp.exp2(sink - mx)
                p = p.astype(BF16)
                vd_old, vd_new = vals[kv]
                o = lax.dot_general(p[:, :w], vd_old, (((1,), (1,)), ((), ())), preferred_element_type=F32)
                o = (o + jnp.dot(p[:, w:], vd_new, preferred_element_type=F32)) / den
                cols.append(jnp.where(lo, o[0:t_new], o[t_new:2 * t_new]))
                cols.append(jnp.where(lo, o[2 * t_new:3 * t_new], o[3 * t_new:4 * t_new]))
            o_scr[pl.ds(r, t_new), :] = jnp.concatenate(cols, axis=1)
        return carry

    lax.fori_loop(0, nseq // SAMPLE_UNROLL, body, 0)
    mix = jnp.dot(o_scr[...].astype(BF16), wo_ref[...], preferred_element_type=F32)
    o_ref[...] = x_ref[...] + _per_row(g1_ref[...], mix.shape[0]) * mix


def _attn_sample(sink, q, kn, vn, kct, vct, wo, l, x, g1, t_new):
    m, d = x.shape
    _, db, nkv, hd, w = kct.shape
    nk = nkv * hd
    g = WINDOW // t_new
    tm = g * t_new
    row = lambda width: pl.BlockSpec((tm, width), lambda i: (i, 0))
    cache_in = pl.BlockSpec((None, g, nkv, hd, w), lambda i: (l, i, 0, 0, 0))
    cache_out = pl.BlockSpec((g, nkv, hd, w), lambda i: (i, 0, 0, 0))
    cache_shape = jax.ShapeDtypeStruct((db, nkv, hd, w), F32)
    return pl.pallas_call(
        functools.partial(_attn_sample_kernel, t_new=t_new),
        grid=(db // g,),
        in_specs=[pl.BlockSpec(memory_space=pltpu.SMEM), row(q.shape[1]), row(nk), row(nk), cache_in, cache_in,
                  _layer_spec(wo, l), row(d), pl.BlockSpec((g, d), lambda i: (i, 0))],
        out_specs=[row(d), cache_out, cache_out],
        out_shape=[jax.ShapeDtypeStruct((m, d), F32), cache_shape, cache_shape],
        scratch_shapes=[pltpu.VMEM((tm, q.shape[1]), F32)],
        compiler_params=_params(1),
        name="attn_sample",
    )(sink, q, kn, vn, kct, vct, wo, x, g1)


def _sgu_kernel(x_ref, sc_ref, sh_ref, g1_ref, gw_ref, win_ref, bin_ref, lng_ref, lnb_ref,
                wsp_ref, bsp_ref, wout_ref, o_ref, *rest, emit_v):
    v_out = rest[0] if emit_v else None
    h_scr, z_scr, vb_scr, gated_scr = rest[-4:]
    x = x_ref[...]
    tm = x.shape[0]
    d_sgu = lng_ref.shape[1]
    gdim = d_sgu // SGU_GROUPS
    h_scr[...] = _modnorm(x, gw_ref[...], sc_ref[0], sh_ref[0]).astype(BF16)

    def inproj(c0):
        z_scr[:, c0:c0 + gdim] = jnp.dot(h_scr[...], win_ref[:, c0:c0 + gdim], preferred_element_type=F32)

    def gelu_cols(c0):
        bias = jnp.broadcast_to(bin_ref[:, c0:c0 + gdim], (SUBLANES, gdim))
        for r0 in range(0, tm, SGU_ROWS):
            z = z_scr[r0:r0 + SGU_ROWS, c0:c0 + gdim].reshape(SGU_ROWS // SUBLANES, SUBLANES, gdim)
            z_scr[r0:r0 + SGU_ROWS, c0:c0 + gdim] = _gelu_tanh(z + bias).reshape(SGU_ROWS, gdim)

    def layernorm_v():
        g8 = jnp.broadcast_to(lng_ref[...], (SUBLANES, d_sgu))
        b8 = jnp.broadcast_to(lnb_ref[...], (SUBLANES, d_sgu))
        for r0 in range(0, tm, SGU_LN_ROWS):
            v = z_scr[r0:r0 + SGU_LN_ROWS, d_sgu:].reshape(SGU_LN_ROWS // SUBLANES, SUBLANES, d_sgu)
            vc = v - jnp.mean(v, axis=-1, keepdims=True)
            var = jnp.mean(vc * vc, axis=-1, keepdims=True)
            vn = (vc * lax.rsqrt(var + EPS) * g8 + b8).reshape(SGU_LN_ROWS, d_sgu)
            if emit_v:
                v_out[r0:r0 + SGU_LN_ROWS, :] = vn
            vb_scr[r0:r0 + SGU_LN_ROWS, :] = vn.astype(BF16)

    r = lax.broadcasted_iota(jnp.int32, (CHUNK, CHUNK), 0)
    c = lax.broadcasted_iota(jnp.int32, (CHUNK, CHUNK), 1)

    def mix(g):
        cols = slice(g * gdim, (g + 1) * gdim)
        wm = jnp.where(r >= c, wsp_ref[g], 0.0).astype(BF16)
        for ch in range(tm // CHUNK):
            rows = slice(ch * CHUNK, (ch + 1) * CHUNK)
            mixed = jnp.dot(wm, vb_scr[rows, cols], preferred_element_type=F32) + bsp_ref[g]
            gated_scr[rows, cols] = (z_scr[rows, cols] * mixed).astype(BF16)

    acc = []

    def outproj(g):
        cols = slice(g * gdim, (g + 1) * gdim)
        part = jnp.dot(gated_scr[:, cols], wout_ref[cols, :], preferred_element_type=F32)
        acc[:] = [part if not acc else acc[0] + part]

    ucols = [g * gdim for g in range(SGU_GROUPS)]
    vcols = [d_sgu + g * gdim for g in range(SGU_GROUPS)]
    order = vcols + ucols
    for i, c0 in enumerate(order):
        inproj(c0)
        if i >= 2:
            gelu_cols(order[i - 2])
        if i == len(vcols) + 1:
            layernorm_v()
    gelu_cols(order[-2])
    mix(0)
    gelu_cols(order[-1])
    for g in range(SGU_GROUPS):
        if g + 1 < SGU_GROUPS:
            mix(g + 1)
        outproj(g)
    o_ref[...] = x + _per_row(g1_ref[0], tm) * acc[0]


def _sgu(x, sc, sh, g1, gw, win, wout, l, b_in, lng, lnb, wsp, bsp, tm, tiles_per_group, emit_v):
    m, d = x.shape
    d_sgu = wout.shape[1]
    row = lambda width: pl.BlockSpec((tm, width), lambda i: (i, 0))
    out_specs = [row(d)]
    out_shape = [jax.ShapeDtypeStruct((m, d), F32)]
    if emit_v:
        out_specs.append(row(d_sgu))
        out_shape.append(jax.ShapeDtypeStruct((m, d_sgu), F32))
    return pl.pallas_call(
        functools.partial(_sgu_kernel, emit_v=emit_v),
        grid=(m // tm,),
        in_specs=[row(d), _mod_spec(sc, tiles_per_group), _mod_spec(sh, tiles_per_group),
                  _mod_spec(g1, tiles_per_group), _resident((1, d)), _layer_spec(win, l),
                  _resident(b_in.shape), _resident(lng.shape), _resident(lnb.shape),
                  _resident(wsp.shape), _resident(bsp.shape), _layer_spec(wout, l)],
        out_specs=out_specs,
        out_shape=out_shape,
        scratch_shapes=[pltpu.VMEM((tm, d), BF16), pltpu.VMEM((tm, 2 * d_sgu), F32),
                        pltpu.VMEM((tm, d_sgu), BF16), pltpu.VMEM((tm, d_sgu), BF16)],
        compiler_params=_params(1),
        name="sgu",
    )(x, sc, sh, g1, gw, win, b_in, lng, lnb, wsp, bsp, wout)


def _ffn_chunks(d_ff):
    return [(c, d_ff + c) for c in range(0, d_ff, FF_CHUNK)]


def _final(xn, gf_ref):
    if gf_ref is None:
        return xn
    ms = jnp.mean(xn * xn, axis=-1, keepdims=True)
    return xn * lax.rsqrt(ms + EPS) * gf_ref[...]


def _ffn_prompt_kernel(x_ref, sc_ref, sh_ref, g2_ref, gw_ref, wup_ref, cw_ref, cb_ref, wdn_ref, *rest,
                       tiles_per_seq, final):
    gf_ref = rest[0] if final else None
    o_ref, tail_ref, halo_scr, a_scr, act_scr = rest[-5:]
    d_ff = wdn_ref.shape[0]
    tm, d = x_ref.shape
    seg = tm // SUBLANES
    lead = (CONV_W - 1) * SUBLANES
    nbuf = a_scr.shape[0]

    @pl.when(pl.program_id(0) % tiles_per_seq == 0)
    def _():
        halo_scr[...] = jnp.zeros(halo_scr.shape, F32)

    hs = _modnorm(x_ref[...], gw_ref[...], sc_ref[0], sh_ref[0])
    h = jnp.swapaxes(hs.reshape(SUBLANES, seg, d), 0, 1).reshape(tm, d).astype(BF16)
    first = lax.broadcasted_iota(jnp.int32, (SUBLANES, FF_CHUNK), 0) == 0

    def up(buf, half, c0):
        cols = slice(c0, c0 + FF_CHUNK)
        a_scr[buf, half, lead:lead + tm, :] = jnp.dot(h, wup_ref[:, cols], preferred_element_type=F32)
        for k in range(CONV_W - 1):
            last = a_scr[buf, half, lead + tm - (2 - k) * SUBLANES:lead + tm - (1 - k) * SUBLANES, :]
            prev = halo_scr[k * SUBLANES:(k + 1) * SUBLANES, cols]
            a_scr[buf, half, k * SUBLANES:(k + 1) * SUBLANES, :] = jnp.where(
                first, pltpu.roll(prev, 1, 0), pltpu.roll(last, 1, 0))
            halo_scr[k * SUBLANES:(k + 1) * SUBLANES, cols] = last

    def conv_taps(c0):
        cols = slice(c0, c0 + FF_CHUNK)
        full = lambda row: jnp.broadcast_to(row, (SUBLANES, FF_CHUNK))
        return [full(cw_ref[k:k + 1, cols]) for k in range(CONV_W)] + [full(cb_ref[:, cols])]

    def conv_rows(buf, half, taps, r0):
        blk = lambda k: a_scr[buf, half, r0 + k * SUBLANES:r0 + k * SUBLANES + FF_ROWS, :].reshape(
            FF_ROWS // SUBLANES, SUBLANES, FF_CHUNK)
        w0, w1, w2, b = taps
        return w0 * blk(0) + w1 * blk(1) + w2 * blk(2) + b

    def gate(buf, cg, cu):
        taps_g, taps_u = conv_taps(cg), conv_taps(cu)
        for r0 in range(0, tm, FF_ROWS):
            act = _silu(conv_rows(buf, 0, taps_g, r0)) * conv_rows(buf, 1, taps_u, r0)
            act_scr[r0:r0 + FF_ROWS, cg:cg + FF_CHUNK] = act.reshape(FF_ROWS, FF_CHUNK).astype(BF16)

    chunks = _ffn_chunks(d_ff)
    acc = []

    def down(c0, c1):
        dn = jnp.dot(act_scr[:, c0:c1], wdn_ref[c0:c1, :], preferred_element_type=F32)
        acc[:] = [dn if not acc else acc[0] + dn]

    pending, gated = [], 0
    for j in range(len(chunks) + FFN_UP_AHEAD):
        if j < len(chunks):
            buf, (cg, cu) = j % nbuf, chunks[j]
            up(buf, 0, cg)
            up(buf, 1, cu)
            pending.append((buf, cg, cu))
        if j >= FFN_UP_AHEAD:
            gate(*pending.pop(0))
            gated += 1
            if gated % FFN_DOWN_GROUP == 0 or gated == len(chunks):
                lo = (gated - 1) // FFN_DOWN_GROUP * FFN_DOWN_GROUP
                down(lo * FF_CHUNK, gated * FF_CHUNK)
    tail_ref[0] = halo_scr[...]
    y = jnp.swapaxes(acc[0].reshape(seg, SUBLANES, d), 0, 1).reshape(tm, d)
    o_ref[...] = _final(x_ref[...] + g2_ref[0] * y, gf_ref)


def _ffn_prompt(x, sc, sh, g2, gw, wup, wdn, l, cw, cb, gf, batch, tm):
    m, d = x.shape
    c2 = wup.shape[-1]
    tiles_per_seq = m // batch // tm
    lead = (CONV_W - 1) * SUBLANES
    row = pl.BlockSpec((tm, d), lambda i: (i, 0))
    final = gf is not None
    in_specs = [row, _mod_spec(sc, tiles_per_seq), _mod_spec(sh, tiles_per_seq), _mod_spec(g2, tiles_per_seq),
                _resident((1, d)), _layer_spec(wup, l), _resident(cw.shape), _resident(cb.shape),
                _layer_spec(wdn, l)]
    args = [x, sc, sh, g2, gw, wup, cw, cb, wdn]
    if final:
        in_specs.append(_resident((1, d)))
        args.append(gf)
    return pl.pallas_call(
        functools.partial(_ffn_prompt_kernel, tiles_per_seq=tiles_per_seq, final=final),
        grid=(m // tm,),
        in_specs=in_specs,
        out_specs=[row, pl.BlockSpec((1, lead, c2), lambda i: (i // tiles_per_seq, 0, 0))],
        out_shape=[jax.ShapeDtypeStruct((m, d), F32), jax.ShapeDtypeStruct((batch, lead, c2), F32)],
        scratch_shapes=[pltpu.VMEM((lead, c2), F32),
                        pltpu.VMEM((FFN_UP_AHEAD + 1, 2, tm + lead, FF_CHUNK), F32),
                        pltpu.VMEM((tm, c2 // 2), BF16)],
        compiler_params=_params(1),
        name="ffn_prompt",
    )(*args)


def _ffn_sample_kernel(x_ref, sc_ref, sh_ref, g2_ref, gw_ref, st_ref, wup_ref, cw_ref, cb_ref, wdn_ref, *rest,
                       final):
    gf_ref = rest[0] if final else None
    o_ref, nst_ref, h_scr, a_scr, act_scr = rest[-5:]
    nb, d = sc_ref.shape
    rows = x_ref.shape[0]
    t_new = rows // nb
    keep = CONV_W - 1
    d_ff = wdn_ref.shape[0]
    c2 = 2 * d_ff
    nbuf = a_scr.shape[0]

    per_row = lambda m: jnp.concatenate([m] * t_new, axis=0)
    xt = jnp.swapaxes(x_ref[...].reshape(nb, t_new, d), 0, 1).reshape(rows, d)
    h_scr[...] = _modnorm(xt, gw_ref[...], per_row(sc_ref[...]), per_row(sh_ref[...])).astype(BF16)

    def up(buf, half, c0):
        cols = slice(c0, c0 + FF_CHUNK)
        for r in range(keep):
            a_scr[buf, half, r * nb:(r + 1) * nb, :] = st_ref[:, r * c2 + c0:r * c2 + c0 + FF_CHUNK]
        a_scr[buf, half, keep * nb:keep * nb + rows, :] = jnp.dot(h_scr[...], wup_ref[:, cols],
                                                                  preferred_element_type=F32)
        for r in range(keep):
            nst_ref[:, r * c2 + c0:r * c2 + c0 + FF_CHUNK] = a_scr[buf, half, rows + r * nb:rows + (r + 1) * nb, :]

    def conv_taps(c0):
        cols = slice(c0, c0 + FF_CHUNK)
        full = lambda row: jnp.broadcast_to(row, (SUBLANES, FF_CHUNK))
        return [full(cw_ref[k:k + 1, cols]) for k in range(CONV_W)] + [full(cb_ref[:, cols])]

    def conv_rows(buf, half, taps, r0):
        blk = lambda k: a_scr[buf, half, r0 + k * nb:r0 + k * nb + FF_ROWS, :].reshape(
            FF_ROWS // SUBLANES, SUBLANES, FF_CHUNK)
        w0, w1, w2, b = taps
        return w0 * blk(0) + w1 * blk(1) + w2 * blk(2) + b

    def gate(buf, cg, cu):
        taps_g, taps_u = conv_taps(cg), conv_taps(cu)
        for r0 in range(0, rows, FF_ROWS):
            act = _silu(conv_rows(buf, 0, taps_g, r0)) * conv_rows(buf, 1, taps_u, r0)
            act_scr[r0:r0 + FF_ROWS, cg:cg + FF_CHUNK] = act.reshape(FF_ROWS, FF_CHUNK).astype(BF16)

    chunks = _ffn_chunks(d_ff)
    acc = []

    def down(c0, c1):
        dn = jnp.dot(act_scr[:, c0:c1], wdn_ref[c0:c1, :], preferred_element_type=F32)
        acc[:] = [dn if not acc else acc[0] + dn]

    pending, gated = [], 0
    for j in range(len(chunks) + FFN_UP_AHEAD):
        if j < len(chunks):
            buf, (cg, cu) = j % nbuf, chunks[j]
            up(buf, 0, cg)
            up(buf, 1, cu)
            pending.append((buf, cg, cu))
        if j >= FFN_UP_AHEAD:
            gate(*pending.pop(0))
            gated += 1
            if gated % FFN_DOWN_GROUP == 0 or gated == len(chunks):
                first = (gated - 1) // FFN_DOWN_GROUP * FFN_DOWN_GROUP
                down(first * FF_CHUNK, gated * FF_CHUNK)
    out = _final(xt + per_row(g2_ref[...]) * acc[0], gf_ref)
    o_ref[...] = jnp.swapaxes(out.reshape(t_new, nb, d), 0, 1).reshape(rows, d)


def _ffn_sample(x, sc, sh, g2, gw, st2, wup, wdn, l, cw, cb, gf, nb):
    db, d = sc.shape
    t_new = x.shape[0] // db
    c2 = wup.shape[-1]
    keep = CONV_W - 1
    rows = t_new * nb
    final = gf is not None
    seqs = lambda width: pl.BlockSpec((nb, width), lambda i: (i, 0))
    xrows = pl.BlockSpec((rows, d), lambda i: (i, 0))
    in_specs = [xrows, seqs(d), seqs(d), seqs(d), _resident((1, d)), seqs(keep * c2),
                _layer_spec(wup, l), _resident(cw.shape), _resident(cb.shape), _layer_spec(wdn, l)]
    args = [x, sc, sh, g2, gw, st2, wup, cw, cb, wdn]
    if final:
        in_specs.append(_resident((1, d)))
        args.append(gf)
    return pl.pallas_call(
        functools.partial(_ffn_sample_kernel, final=final),
        grid=(db // nb,),
        in_specs=in_specs,
        out_specs=[xrows, seqs(keep * c2)],
        out_shape=[jax.ShapeDtypeStruct((db * t_new, d), F32), jax.ShapeDtypeStruct((db, keep * c2), F32)],
        scratch_shapes=[pltpu.VMEM((rows, d), BF16),
                        pltpu.VMEM((FFN_UP_AHEAD + 1, 2, rows + keep * nb, FF_CHUNK), F32),
                        pltpu.VMEM((rows, c2 // 2), BF16)],
        compiler_params=_params(1),
        name="ffn_sample",
    )(*args)


def _tile_rows(size, want):
    return want if size % want == 0 else size


def kernel(x_prompt, x_sample, c_prompt, c_sample, cache_k, cache_v, state_conv, w_ada, b_ada, norm_mix,
           norm_ffn, w_qkv, b_qkv, attn_sink, w_o, w_sgu_in, b_sgu_in, sgu_ln_g, sgu_ln_b, w_spatial,
           b_spatial, w_sgu_out, w_up, conv_w, conv_b, w_down, norm_final):
    batch, seq, d = x_prompt.shape
    db, t_new, _ = x_sample.shape
    depth = w_ada.shape[0]

    pad = (-(batch + db)) % SUBLANES
    c_all = jnp.concatenate([c_prompt, c_sample, jnp.zeros((pad, d), F32)], axis=0)
    mod = _ada(c_all, w_ada, b_ada)

    def mod_part(l, k, lo, hi):
        return mod[l, lo:hi, k * d:(k + 1) * d]

    mp, ms = x_prompt.shape[0] * seq, db * t_new
    xp = x_prompt.reshape(mp, d)
    xs = x_sample.reshape(ms, d)

    tm_qkv_p, tm_sgu_p, tm_ffn_p = _tile_rows(seq, 512), _tile_rows(seq, 512), _tile_rows(seq, 512)
    tm_qkv_s, tm_sgu_s = _tile_rows(ms, 512), _tile_rows(ms, 256)
    nb_ffn_s = _tile_rows(db, 512 // t_new)

    tab_p = _rope_tables(jnp.arange(seq))
    tab_s = _rope_tables(PAST_LEN + (jnp.arange(ms) % t_new))

    wq_all, wo_all = w_qkv.astype(BF16), w_o.astype(BF16)
    win_all, wout_all = w_sgu_in.astype(BF16), w_sgu_out.astype(BF16)
    wup_all, wdn_all = w_up.astype(BF16), w_down.astype(BF16)

    kct = jnp.transpose(cache_k, (0, 1, 3, 4, 2))
    vct = jnp.transpose(cache_v, (0, 1, 3, 4, 2))

    new_k_p, new_v_p, new_conv_p = [], [], []
    new_k_s, new_v_s, new_conv_s, new_sgu_s = [], [], [], []

    for l in range(depth):
        idx = l // N_MIXERS
        gmix = norm_mix[l].reshape(1, d)
        gffn = norm_ffn[l].reshape(1, d)
        p_mod = [mod_part(l, k, 0, batch)[:, None, :] for k in range(6)]
        s_seq = [mod_part(l, k, batch, batch + db) for k in range(6)]
        seq_tiles = lambda a, tm: a.reshape(ms // tm, tm // t_new, d)

        if l % N_MIXERS == 0:
            bq = b_qkv[idx].reshape(1, -1)
            sink = attn_sink[idx] * LOG2E
            keep = min(WINDOW, seq)
            q, k, v, k_last, v_last = _qkv(xp, p_mod[1], p_mod[0], gmix, wq_all, idx, bq, tab_p, tm_qkv_p,
                                           seq // tm_qkv_p, seq // tm_qkv_p, tail_rows=keep)
            nblk = ATTN_BLOCKS_PER_STEP if seq % (ATTN_BLOCKS_PER_STEP * WINDOW) == 0 else 1
            xp = _attn_prompt(sink, q, k, v, wo_all, idx, xp, p_mod[2], batch, nblk)
            new_k_p.append(k_last.reshape(batch, keep, N_KV_HEADS, HEAD_DIM))
            new_v_p.append(v_last.reshape(batch, keep, N_KV_HEADS, HEAD_DIM))

            q, k, v = _qkv(xs, seq_tiles(s_seq[1], tm_qkv_s), seq_tiles(s_seq[0], tm_qkv_s), gmix, wq_all, idx, bq, tab_s,
                           tm_qkv_s, 1, ms // tm_qkv_s)
            xs, nk, nv = _attn_sample(sink, q, k, v, kct, vct, wo_all, idx, xs, s_seq[2], t_new)
            new_k_s.append(nk)
            new_v_s.append(nv)
        else:
            b_in = b_sgu_in[idx].reshape(1, -1)
            lng = sgu_ln_g[idx].reshape(1, -1)
            lnb = sgu_ln_b[idx].reshape(1, -1)
            tc = min(seq, CHUNK)
            (xp,) = _sgu(xp, p_mod[1], p_mod[0], p_mod[2], gmix, win_all, wout_all, idx, b_in, lng, lnb,
                         w_spatial[idx][:, :tc, :tc], b_spatial[idx][:, :tc, None],
                         tm_sgu_p, seq // tm_sgu_p, False)
            reps = CHUNK // t_new
            eye = jnp.eye(reps, dtype=F32)
            wsp_s = jnp.einsum("ab,gts->gatbs", eye, w_spatial[idx][:, :t_new, :t_new]).reshape(
                SGU_GROUPS, CHUNK, CHUNK)
            bsp_s = jnp.tile(b_spatial[idx][:, :t_new], (1, reps))[:, :, None]
            xs, vrows = _sgu(xs, seq_tiles(s_seq[1], tm_sgu_s), seq_tiles(s_seq[0], tm_sgu_s),
                             seq_tiles(s_seq[2], tm_sgu_s), gmix, win_all, wout_all, idx,
                             b_in, lng, lnb, wsp_s, bsp_s, tm_sgu_s, 1, True)
            new_sgu_s.append(vrows.reshape(db, t_new, -1))

        cw = conv_w[l]
        cb = conv_b[l].reshape(1, -1)
        gf = norm_final.reshape(1, d) if l == depth - 1 else None
        xp, tail = _ffn_prompt(xp, p_mod[4], p_mod[3], p_mod[5], gffn, wup_all, wdn_all, l, cw, cb, gf, batch,
                               tm_ffn_p)
        new_conv_p.append(tail[:, SUBLANES - 1::SUBLANES, :])

        xs, nst = _ffn_sample(xs, s_seq[4], s_seq[3], s_seq[5], gffn,
                              state_conv[l].reshape(db, -1), wup_all, wdn_all, l, cw, cb, gf, nb_ffn_s)
        new_conv_s.append(nst.reshape(db, CONV_W - 1, -1))

    return (xp.reshape(batch, seq, d), xs.reshape(db, t_new, d),
            jnp.stack(new_k_p), jnp.stack(new_v_p), jnp.stack(new_conv_p),
            jnp.transpose(jnp.stack(new_k_s), (0, 1, 4, 2, 3)), jnp.transpose(jnp.stack(new_v_s), (0, 1, 4, 2, 3)),
            jnp.stack(new_conv_s), jnp.stack(new_sgu_s))
```

```python
import functools

import jax
import jax.numpy as jnp
from jax import lax
from jax.experimental import pallas as pl
from jax.experimental.pallas import tpu as pltpu

N_HEADS = 16
N_KV_HEADS = 4
HEAD_DIM = 64
Q_PER_KV = N_HEADS // N_KV_HEADS
WINDOW = 128
ROT_DIM = HEAD_DIM // 4
ROPE_THETA = 500000.0
CHUNK = 128
SGU_GROUPS = 4
CONV_W = 3
EPS = 1e-6
N_MIXERS = 2
PAST_LEN = 8192

LANES = 128
SUBLANES = 8
FF_CHUNK = 256
ATTN_BLOCKS_PER_STEP = 8
SGU_ROWS = 64
SGU_LN_ROWS = 16
FF_ROWS = 64
FFN_UP_AHEAD = 3
FFN_DOWN_GROUP = 4
ATTN_SCORES_AHEAD = 5
SAMPLE_UNROLL = 4
LOG2E = 1.4426950408889634
Q_SCALE = HEAD_DIM ** -0.5 * LOG2E
VMEM_LIMIT = 56 * 1024 * 1024
NEG_BIG = -1e30

F32 = jnp.float32
BF16 = jnp.bfloat16


def _params(n_axes=1, vmem=VMEM_LIMIT):
    return pltpu.CompilerParams(dimension_semantics=("arbitrary",) * n_axes, vmem_limit_bytes=vmem)


def _resident(shape):
    nd = len(shape)
    return pl.BlockSpec(shape, lambda *_: (0,) * nd, pipeline_mode=pl.Buffered(1))


def _layer_spec(stacked, l):
    nd = stacked.ndim - 1
    return pl.BlockSpec((None,) + stacked.shape[1:], lambda *_: (l,) + (0,) * nd, pipeline_mode=pl.Buffered(1))


def _per_row(m, rows):
    g, d = m.shape
    if g in (1, rows):
        return m
    return jnp.broadcast_to(m[:, None, :], (g, rows // g, d)).reshape(rows, d)


def _modnorm(x, gw, scale, shift):
    ms = jnp.mean(x * x, axis=-1, keepdims=True)
    y = x * lax.rsqrt(ms + EPS) * gw
    return y * (1.0 + _per_row(scale, x.shape[0])) + _per_row(shift, x.shape[0])


def _silu(x):
    return x * jax.nn.sigmoid(x)


def _gelu_tanh(x):
    c = 0.7978845608028654
    hx = 0.5 * x
    return hx + hx * jnp.tanh(x * (c + (c * 0.044715) * (x * x)))


def _ada_kernel(c_ref, w_ref, b_ref, o_ref):
    s = _silu(c_ref[...]).astype(BF16)
    o_ref[0] = jnp.dot(s, w_ref[0].astype(BF16), preferred_element_type=F32) + b_ref[0]


def _ada(c_all, w_ada, b_ada):
    depth, d, n6 = w_ada.shape
    rows = c_all.shape[0]
    tn = 1024
    return pl.pallas_call(
        _ada_kernel,
        grid=(depth, n6 // tn),
        in_specs=[
            pl.BlockSpec((rows, d), lambda l, n: (0, 0)),
            pl.BlockSpec((1, d, tn), lambda l, n: (l, 0, n)),
            pl.BlockSpec((1, 1, tn), lambda l, n: (l, 0, n)),
        ],
        out_specs=pl.BlockSpec((1, rows, tn), lambda l, n: (l, 0, n)),
        out_shape=jax.ShapeDtypeStruct((depth, rows, n6), F32),
        compiler_params=_params(2),
        name="ada",
    )(c_all, w_ada, b_ada.reshape(depth, 1, n6))


def _qkv_kernel(x_ref, sc_ref, sh_ref, gw_ref, w_ref, b_ref, cos_ref, sa_ref, sb_ref,
                q_ref, k_ref, v_ref, *tails):
    h = _modnorm(x_ref[...], gw_ref[...], sc_ref[0], sh_ref[0]).astype(BF16)
    y = jnp.dot(h, w_ref[...], preferred_element_type=F32) + b_ref[...]
    cos, sa, sb = cos_ref[...], sa_ref[...], sb_ref[...]
    nq = N_HEADS * HEAD_DIM
    nk = N_KV_HEADS * HEAD_DIM

    def rope(yb):
        return yb * cos + pltpu.roll(yb, LANES - ROT_DIM // 2, 1) * sa + pltpu.roll(yb, ROT_DIM // 2, 1) * sb

    for j in range(nq // LANES):
        q_ref[:, j * LANES:(j + 1) * LANES] = (rope(y[:, j * LANES:(j + 1) * LANES]) * Q_SCALE).astype(q_ref.dtype)
    k = [rope(y[:, nq + j * LANES:nq + (j + 1) * LANES]) for j in range(nk // LANES)]
    v = y[:, nq + nk:]
    for j in range(nk // LANES):
        k_ref[:, j * LANES:(j + 1) * LANES] = k[j].astype(k_ref.dtype)
    v_ref[...] = v.astype(v_ref.dtype)
    if tails:
        kt_ref, vt_ref = tails
        first = y.shape[0] - kt_ref.shape[1]
        for j in range(nk // LANES):
            kt_ref[0, :, j * LANES:(j + 1) * LANES] = k[j][first:]
        vt_ref[0] = v[first:]


def _rope_tables(pos):
    half = ROT_DIM // 2
    inv = ROPE_THETA ** (-jnp.arange(0, ROT_DIM, 2, dtype=F32) / ROT_DIM)
    ang = pos.astype(F32)[:, None] * inv[None, :]
    cos, sin = jnp.cos(ang), jnp.sin(ang)
    t = pos.shape[0]
    one = jnp.ones((t, HEAD_DIM - ROT_DIM), F32)
    zero = jnp.zeros((t, HEAD_DIM - ROT_DIM), F32)
    zh = jnp.zeros((t, half), F32)
    cos_t = jnp.concatenate([cos, cos, one], 1)
    sa_t = jnp.concatenate([-sin, zh, zero], 1)
    sb_t = jnp.concatenate([zh, sin, zero], 1)
    rep = LANES // HEAD_DIM
    return tuple(jnp.tile(a, (1, rep)) for a in (cos_t, sa_t, sb_t))


def _mod_spec(mod, tiles_per_group):
    return pl.BlockSpec((1,) + mod.shape[1:], lambda i: (i // tiles_per_group, 0, 0))


def _qkv(x, sc, sh, gw, w, l, b, tables, tm, tiles_per_group, table_tiles, tail_rows=0):
    m, d = x.shape
    n = w.shape[-1]
    nq = N_HEADS * HEAD_DIM
    nk = N_KV_HEADS * HEAD_DIM
    row = lambda width: pl.BlockSpec((tm, width), lambda i: (i, 0))
    tab = pl.BlockSpec((tm, LANES), lambda i: (i % table_tiles, 0))
    dt = BF16 if tail_rows else F32
    tail = pl.BlockSpec((1, tail_rows, nk), lambda i: (i // tiles_per_group, 0, 0))
    tail_shape = jax.ShapeDtypeStruct((m // tm // tiles_per_group, tail_rows, nk), F32)
    return pl.pallas_call(
        _qkv_kernel,
        grid=(m // tm,),
        in_specs=[row(d), _mod_spec(sc, tiles_per_group), _mod_spec(sh, tiles_per_group),
                  _resident((1, d)), _layer_spec(w, l), _resident((1, n)), tab, tab, tab],
        out_specs=[row(nq), row(nk), row(nk)] + [tail, tail] * bool(tail_rows),
        out_shape=[jax.ShapeDtypeStruct((m, nq), dt), jax.ShapeDtypeStruct((m, nk), dt),
                   jax.ShapeDtypeStruct((m, nk), dt)] + [tail_shape, tail_shape] * bool(tail_rows),
        compiler_params=_params(1),
        name="qkv",
    )(x, sc, sh, gw, w, b, *tables)


def _dup_half(x, kv):
    lo = lax.broadcasted_iota(jnp.int32, (1, LANES), 1) < HEAD_DIM
    x = x.astype(F32)
    xr = pltpu.roll(x, HEAD_DIM, 1)
    return jnp.where(lo, x, xr) if kv % 2 == 0 else jnp.where(lo, xr, x)


def _masked_queries(q, kv):
    lo = lax.broadcasted_iota(jnp.int32, (1, LANES), 1) < HEAD_DIM
    q0 = q[:, (2 * kv) * LANES:(2 * kv + 1) * LANES]
    q1 = q[:, (2 * kv + 1) * LANES:(2 * kv + 2) * LANES]
    return jnp.concatenate([jnp.where(lo, q0, 0.0), jnp.where(lo, 0.0, q0),
                            jnp.where(lo, q1, 0.0), jnp.where(lo, 0.0, q1)], axis=0)


def _band(tq, nwin):
    i = lax.broadcasted_iota(jnp.int32, (tq, nwin), 0)
    j = lax.broadcasted_iota(jnp.int32, (tq, nwin), 1)
    diff = WINDOW + i - j
    return jnp.concatenate([(diff >= 0) & (diff <= WINDOW)] * Q_PER_KV, axis=0)


def _band_t(tq, nwin, prev_valid):
    j = lax.broadcasted_iota(jnp.int32, (nwin, tq), 0)
    i = lax.broadcasted_iota(jnp.int32, (nwin, tq), 1)
    diff = WINDOW + i - j
    band = (diff >= 0) & (diff <= WINDOW)
    if prev_valid is not None:
        band = band & ((j >= WINDOW) | prev_valid)
    return jnp.concatenate([band] * Q_PER_KV, axis=1)


def _scores_t(q, kwin, kv):
    kd = _dup_half(kwin[:, (kv // 2) * LANES:(kv // 2 + 1) * LANES], kv)
    return lax.dot_general(kd.astype(BF16), _masked_queries(q, kv).astype(BF16), (((1,), (1,)), ((), ())),
                           preferred_element_type=F32)


def _softmax_pv_t(s, band, vwin, kv, sink_ref):
    tq = s.shape[1] // Q_PER_KV
    nwin = s.shape[0]
    blk = kv // 2
    vt = vwin[:, blk * LANES:(blk + 1) * LANES].astype(F32).T
    vt = jnp.concatenate([vt[(kv % 2) * HEAD_DIM:(kv % 2 + 1) * HEAD_DIM],
                          jnp.ones((2 * SUBLANES, nwin), F32)], axis=0)
    sink = jnp.concatenate([jnp.full((1, tq), sink_ref[Q_PER_KV * kv + g], F32)
                            for g in range(Q_PER_KV)], axis=1)
    s = jnp.where(band, s, NEG_BIG)
    mx = jnp.maximum(jnp.max(s, axis=0, keepdims=True), sink)
    p = jnp.exp2(s - mx)
    o = jnp.dot(vt.astype(BF16), p.astype(BF16), preferred_element_type=F32)
    den = o[HEAD_DIM:HEAD_DIM + 1] + jnp.exp2(sink - mx)
    o = o[:HEAD_DIM] / den
    return [o[:, g * tq:(g + 1) * tq] for g in range(Q_PER_KV)]


def _attn_prompt_kernel(sink_ref, q_ref, ko_ref, kp_ref, vo_ref, vp_ref, wo_ref, x_ref, g1_ref, o_ref):
    nblk = q_ref.shape[0] // WINDOW
    kall = jnp.concatenate([kp_ref[...], ko_ref[...]], axis=0)
    vall = jnp.concatenate([vp_ref[...], vo_ref[...]], axis=0)
    band_first = _band_t(WINDOW, 2 * WINDOW, pl.program_id(1) > 0)
    band_rest = _band_t(WINDOW, 2 * WINDOW, None) if nblk > 1 else None
    heads = {c: [] for c in range(nblk)}

    def finish(c, kv, s):
        win = slice(c * WINDOW, (c + 2) * WINDOW)
        heads[c].extend(_softmax_pv_t(s, band_first if c == 0 else band_rest, vall[win], kv, sink_ref))
        if kv == N_KV_HEADS - 1:
            rows = slice(c * WINDOW, (c + 1) * WINDOW)
            ot = jnp.concatenate(heads[c], axis=0)
            mix = lax.dot_general(ot.astype(BF16), wo_ref[...], (((0,), (0,)), ((), ())),
                                  preferred_element_type=F32)
            o_ref[rows, :] = x_ref[rows, :] + g1_ref[0] * mix

    pending = []
    for c in range(nblk):
        for kv in range(N_KV_HEADS):
            s = _scores_t(q_ref[c * WINDOW:(c + 1) * WINDOW, :], kall[c * WINDOW:(c + 2) * WINDOW], kv)
            pending.append((c, kv, s))
            if len(pending) > ATTN_SCORES_AHEAD:
                finish(*pending.pop(0))
    for unit in pending:
        finish(*unit)


def _attn_prompt(sink, q, k, v, wo, l, x, g1, batch, nblk):
    m, d = x.shape
    tq = nblk * WINDOW
    nb = m // batch // tq
    nk = k.shape[1]
    own = lambda width: pl.BlockSpec((tq, width), lambda b, n: (b * nb + n, 0))
    prev = lambda width: pl.BlockSpec(
        (WINDOW, width), lambda b, n: (b * nb * nblk + jnp.maximum(n * nblk - 1, 0), 0))
    return pl.pallas_call(
        _attn_prompt_kernel,
        grid=(batch, nb),
        in_specs=[pl.BlockSpec(memory_space=pltpu.SMEM), own(q.shape[1]), own(nk), prev(nk), own(nk), prev(nk),
                  _layer_spec(wo, l),
                  own(d), pl.BlockSpec((1, 1, d), lambda b, n: (b, 0, 0))],
        out_specs=own(d),
        out_shape=jax.ShapeDtypeStruct((m, d), F32),
        compiler_params=_params(2),
        name="attn_prompt",
    )(sink, q, k, k, v, v, wo, x, g1)


def _attn_sample_kernel(sink_ref, q_ref, kn_ref, vn_ref, kc_ref, vc_ref, wo_ref, x_ref, g1_ref,
                        o_ref, ko_ref, vo_ref, o_scr, *, t_new):
    nseq, _, _, w = kc_ref.shape
    lanes = lax.broadcasted_iota(jnp.int32, (1, w), 1)
    lo = lanes < HEAD_DIM
    is_new = lanes >= w - t_new
    zpad = jnp.zeros((w - t_new, LANES), F32)
    band = _band(t_new, 2 * w)

    def new_rows(x, blk, at_end):
        rows = x[:, blk * LANES:(blk + 1) * LANES]
        return jnp.concatenate([zpad, rows] if at_end else [rows, zpad], axis=0)

    def body(it, carry):
        units = []
        for u in range(SAMPLE_UNROLL):
            b = it * SAMPLE_UNROLL + u
            r = pl.multiple_of(b * t_new, t_new)
            kn, vn = kn_ref[pl.ds(r, t_new), :], vn_ref[pl.ds(r, t_new), :]
            q = q_ref[pl.ds(r, t_new), :]
            knt = [new_rows(kn, blk, True).T for blk in range(N_KV_HEADS // 2)]
            vnt = [new_rows(vn, blk, True).T for blk in range(N_KV_HEADS // 2)]
            scores, vals = [], []
            for kv in range(N_KV_HEADS):
                half = slice((kv % 2) * HEAD_DIM, (kv % 2 + 1) * HEAD_DIM)
                kt, vt = kc_ref[b, kv], vc_ref[b, kv]
                ko_ref[b, kv] = jnp.where(is_new, knt[kv // 2][half], pltpu.roll(kt, w - t_new, 1))
                vo_ref[b, kv] = jnp.where(is_new, vnt[kv // 2][half], pltpu.roll(vt, w - t_new, 1))
                lhs = _masked_queries(q, kv).astype(BF16)
                kd_new = _dup_half(new_rows(kn, kv // 2, False), kv).astype(BF16)
                s_old = jnp.dot(lhs, jnp.concatenate([kt, kt], axis=0).astype(BF16), preferred_element_type=F32)
                s_new = lax.dot_general(lhs, kd_new, (((1,), (1,)), ((), ())), preferred_element_type=F32)
                scores.append(jnp.concatenate([s_old, s_new], axis=1))
                vals.append((jnp.concatenate([vt, vt], axis=0).astype(BF16),
                             _dup_half(new_rows(vn, kv // 2, False), kv).astype(BF16)))
            units.append((r, scores, vals))
        for r, scores, vals in units:
            cols = []
            for kv in range(N_KV_HEADS):
                sink = jnp.concatenate([jnp.full((t_new, 1), sink_ref[Q_PER_KV * kv + g], F32)
                                        for g in range(Q_PER_KV)], axis=0)
                s = jnp.where(band, scores[kv], NEG_BIG)
                mx = jnp.maximum(jnp.max(s, axis=-1, keepdims=True), sink)
                p = jnp.exp2(s - mx)
                den = jnp.sum(p, axis=-1, keepdims=True) + jnp.exp2(sink - mx)
                p = p.astype(BF16)
                vd_old, vd_new = vals[kv]
                o = lax.dot_general(p[:, :w], vd_old, (((1,), (1,)), ((), ())), preferred_element_type=F32)
                o = (o + jnp.dot(p[:, w:], vd_new, preferred_element_type=F32)) / den
                cols.append(jnp.where(lo, o[0:t_new], o[t_new:2 * t_new]))
                cols.append(jnp.where(lo, o[2 * t_new:3 * t_new], o[3 * t_new:4 * t_new]))
            o_scr[pl.ds(r, t_new), :] = jnp.concatenate(cols, axis=1)
        return carry

    lax.fori_loop(0, nseq // SAMPLE_UNROLL, body, 0)
    mix = jnp.dot(o_scr[...].astype(BF16), wo_ref[...], preferred_element_type=F32)
    o_ref[...] = x_ref[...] + _per_row(g1_ref[...], mix.shape[0]) * mix


def _attn_sample(sink, q, kn, vn, kct, vct, wo, l, x, g1, t_new):
    m, d = x.shape
    _, db, nkv, hd, w = kct.shape
    nk = nkv * hd
    g = WINDOW // t_new
    tm = g * t_new
    row = lambda width: pl.BlockSpec((tm, width), lambda i: (i, 0))
    cache_in = pl.BlockSpec((None, g, nkv, hd, w), lambda i: (l, i, 0, 0, 0))
    cache_out = pl.BlockSpec((g, nkv, hd, w), lambda i: (i, 0, 0, 0))
    cache_shape = jax.ShapeDtypeStruct((db, nkv, hd, w), F32)
    return pl.pallas_call(
        functools.partial(_attn_sample_kernel, t_new=t_new),
        grid=(db // g,),
        in_specs=[pl.BlockSpec(memory_space=pltpu.SMEM), row(q.shape[1]), row(nk), row(nk), cache_in, cache_in,
                  _layer_spec(wo, l), row(d), pl.BlockSpec((g, d), lambda i: (i, 0))],
        out_specs=[row(d), cache_out, cache_out],
        out_shape=[jax.ShapeDtypeStruct((m, d), F32), cache_shape, cache_shape],
        scratch_shapes=[pltpu.VMEM((tm, q.shape[1]), F32)],
        compiler_params=_params(1),
        name="attn_sample",
    )(sink, q, kn, vn, kct, vct, wo, x, g1)


def _sgu_kernel(x_ref, sc_ref, sh_ref, g1_ref, gw_ref, win_ref, bin_ref, lng_ref, lnb_ref,
                wsp_ref, bsp_ref, wout_ref, o_ref, *rest, emit_v):
    v_out = rest[0] if emit_v else None
    h_scr, z_scr, vb_scr, gated_scr = rest[-4:]
    x = x_ref[...]
    tm = x.shape[0]
    d_sgu = lng_ref.shape[1]
    gdim = d_sgu // SGU_GROUPS
    h_scr[...] = _modnorm(x, gw_ref[...], sc_ref[0], sh_ref[0]).astype(BF16)

    def inproj(c0):
        z_scr[:, c0:c0 + gdim] = jnp.dot(h_scr[...], win_ref[:, c0:c0 + gdim], preferred_element_type=F32)

    def gelu_cols(c0):
        bias = jnp.broadcast_to(bin_ref[:, c0:c0 + gdim], (SUBLANES, gdim))
        for r0 in range(0, tm, SGU_ROWS):
            z = z_scr[r0:r0 + SGU_ROWS, c0:c0 + gdim].reshape(SGU_ROWS // SUBLANES, SUBLANES, gdim)
            z_scr[r0:r0 + SGU_ROWS, c0:c0 + gdim] = _gelu_tanh(z + bias).reshape(SGU_ROWS, gdim)

    def layernorm_v():
        g8 = jnp.broadcast_to(lng_ref[...], (SUBLANES, d_sgu))
        b8 = jnp.broadcast_to(lnb_ref[...], (SUBLANES, d_sgu))
        for r0 in range(0, tm, SGU_LN_ROWS):
            v = z_scr[r0:r0 + SGU_LN_ROWS, d_sgu:].reshape(SGU_LN_ROWS // SUBLANES, SUBLANES, d_sgu)
            vc = v - jnp.mean(v, axis=-1, keepdims=True)
            var = jnp.mean(vc * vc, axis=-1, keepdims=True)
            vn = (vc * lax.rsqrt(var + EPS) * g8 + b8).reshape(SGU_LN_ROWS, d_sgu)
            if emit_v:
                v_out[r0:r0 + SGU_LN_ROWS, :] = vn
            vb_scr[r0:r0 + SGU_LN_ROWS, :] = vn.astype(BF16)

    r = lax.broadcasted_iota(jnp.int32, (CHUNK, CHUNK), 0)
    c = lax.broadcasted_iota(jnp.int32, (CHUNK, CHUNK), 1)

    def mix(g):
        cols = slice(g * gdim, (g + 1) * gdim)
        wm = jnp.where(r >= c, wsp_ref[g], 0.0).astype(BF16)
        for ch in range(tm // CHUNK):
            rows = slice(ch * CHUNK, (ch + 1) * CHUNK)
            mixed = jnp.dot(wm, vb_scr[rows, cols], preferred_element_type=F32) + bsp_ref[g]
            gated_scr[rows, cols] = (z_scr[rows, cols] * mixed).astype(BF16)

    acc = []

    def outproj(g):
        cols = slice(g * gdim, (g + 1) * gdim)
        part = jnp.dot(gated_scr[:, cols], wout_ref[cols, :], preferred_element_type=F32)
        acc[:] = [part if not acc else acc[0] + part]

    ucols = [g * gdim for g in range(SGU_GROUPS)]
    vcols = [d_sgu + g * gdim for g in range(SGU_GROUPS)]
    order = vcols + ucols
    for i, c0 in enumerate(order):
        inproj(c0)
        if i >= 2:
            gelu_cols(order[i - 2])
        if i == len(vcols) + 1:
            layernorm_v()
    gelu_cols(order[-2])
    mix(0)
    gelu_cols(order[-1])
    for g in range(SGU_GROUPS):
        if g + 1 < SGU_GROUPS:
            mix(g + 1)
        outproj(g)
    o_ref[...] = x + _per_row(g1_ref[0], tm) * acc[0]


def _sgu(x, sc, sh, g1, gw, win, wout, l, b_in, lng, lnb, wsp, bsp, tm, tiles_per_group, emit_v):
    m, d = x.shape
    d_sgu = wout.shape[1]
    row = lambda width: pl.BlockSpec((tm, width), lambda i: (i, 0))
    out_specs = [row(d)]
    out_shape = [jax.ShapeDtypeStruct((m, d), F32)]
    if emit_v:
        out_specs.append(row(d_sgu))
        out_shape.append(jax.ShapeDtypeStruct((m, d_sgu), F32))
    return pl.pallas_call(
        functools.partial(_sgu_kernel, emit_v=emit_v),
        grid=(m // tm,),
        in_specs=[row(d), _mod_spec(sc, tiles_per_group), _mod_spec(sh, tiles_per_group),
                  _mod_spec(g1, tiles_per_group), _resident((1, d)), _layer_spec(win, l),
                  _resident(b_in.shape), _resident(lng.shape), _resident(lnb.shape),
                  _resident(wsp.shape), _resident(bsp.shape), _layer_spec(wout, l)],
        out_specs=out_specs,
        out_shape=out_shape,
        scratch_shapes=[pltpu.VMEM((tm, d), BF16), pltpu.VMEM((tm, 2 * d_sgu), F32),
                        pltpu.VMEM((tm, d_sgu), BF16), pltpu.VMEM((tm, d_sgu), BF16)],
        compiler_params=_params(1),
        name="sgu",
    )(x, sc, sh, g1, gw, win, b_in, lng, lnb, wsp, bsp, wout)


def _ffn_chunks(d_ff):
    return [(c, d_ff + c) for c in range(0, d_ff, FF_CHUNK)]


def _final(xn, gf_ref):
    if gf_ref is None:
        return xn
    ms = jnp.mean(xn * xn, axis=-1, keepdims=True)
    return xn * lax.rsqrt(ms + EPS) * gf_ref[...]


def _ffn_prompt_kernel(x_ref, sc_ref, sh_ref, g2_ref, gw_ref, wup_ref, cw_ref, cb_ref, wdn_ref, *rest,
                       tiles_per_seq, final):
    gf_ref = rest[0] if final else None
    o_ref, tail_ref, halo_scr, a_scr, act_scr = rest[-5:]
    d_ff = wdn_ref.shape[0]
    tm, d = x_ref.shape
    seg = tm // SUBLANES
    lead = (CONV_W - 1) * SUBLANES
    nbuf = a_scr.shape[0]

    @pl.when(pl.program_id(0) % tiles_per_seq == 0)
    def _():
        halo_scr[...] = jnp.zeros(halo_scr.shape, F32)

    hs = _modnorm(x_ref[...], gw_ref[...], sc_ref[0], sh_ref[0])
    h = jnp.swapaxes(hs.reshape(SUBLANES, seg, d), 0, 1).reshape(tm, d).astype(BF16)
    first = lax.broadcasted_iota(jnp.int32, (SUBLANES, FF_CHUNK), 0) == 0

    def up(buf, half, c0):
        cols = slice(c0, c0 + FF_CHUNK)
        a_scr[buf, half, lead:lead + tm, :] = jnp.dot(h, wup_ref[:, cols], preferred_element_type=F32)
        for k in range(CONV_W - 1):
            last = a_scr[buf, half, lead + tm - (2 - k) * SUBLANES:lead + tm - (1 - k) * SUBLANES, :]
            prev = halo_scr[k * SUBLANES:(k + 1) * SUBLANES, cols]
            a_scr[buf, half, k * SUBLANES:(k + 1) * SUBLANES, :] = jnp.where(
                first, pltpu.roll(prev, 1, 0), pltpu.roll(last, 1, 0))
            halo_scr[k * SUBLANES:(k + 1) * SUBLANES, cols] = last

    def conv_taps(c0):
        cols = slice(c0, c0 + FF_CHUNK)
        full = lambda row: jnp.broadcast_to(row, (SUBLANES, FF_CHUNK))
        return [full(cw_ref[k:k + 1, cols]) for k in range(CONV_W)] + [full(cb_ref[:, cols])]

    def conv_rows(buf, half, taps, r0):
        blk = lambda k: a_scr[buf, half, r0 + k * SUBLANES:r0 + k * SUBLANES + FF_ROWS, :].reshape(
            FF_ROWS // SUBLANES, SUBLANES, FF_CHUNK)
        w0, w1, w2, b = taps
        return w0 * blk(0) + w1 * blk(1) + w2 * blk(2) + b

    def gate(buf, cg, cu):
        taps_g, taps_u = conv_taps(cg), conv_taps(cu)
        for r0 in range(0, tm, FF_ROWS):
            act = _silu(conv_rows(buf, 0, taps_g, r0)) * conv_rows(buf, 1, taps_u, r0)
            act_scr[r0:r0 + FF_ROWS, cg:cg + FF_CHUNK] = act.reshape(FF_ROWS, FF_CHUNK).astype(BF16)

    chunks = _ffn_chunks(d_ff)
    acc = []

    def down(c0, c1):
        dn = jnp.dot(act_scr[:, c0:c1], wdn_ref[c0:c1, :], preferred_element_type=F32)
        acc[:] = [dn if not acc else acc[0] + dn]

    pending, gated = [], 0
    for j in range(len(chunks) + FFN_UP_AHEAD):
        if j < len(chunks):
            buf, (cg, cu) = j % nbuf, chunks[j]
            up(buf, 0, cg)
            up(buf, 1, cu)
            pending.append((buf, cg, cu))
        if j >= FFN_UP_AHEAD:
            gate(*pending.pop(0))
            gated += 1
            if gated % FFN_DOWN_GROUP == 0 or gated == len(chunks):
                lo = (gated - 1) // FFN_DOWN_GROUP * FFN_DOWN_GROUP
                down(lo * FF_CHUNK, gated * FF_CHUNK)
    tail_ref[0] = halo_scr[...]
    y = jnp.swapaxes(acc[0].reshape(seg, SUBLANES, d), 0, 1).reshape(tm, d)
    o_ref[...] = _final(x_ref[...] + g2_ref[0] * y, gf_ref)


def _ffn_prompt(x, sc, sh, g2, gw, wup, wdn, l, cw, cb, gf, batch, tm):
    m, d = x.shape
    c2 = wup.shape[-1]
    tiles_per_seq = m // batch // tm
    lead = (CONV_W - 1) * SUBLANES
    row = pl.BlockSpec((tm, d), lambda i: (i, 0))
    final = gf is not None
    in_specs = [row, _mod_spec(sc, tiles_per_seq), _mod_spec(sh, tiles_per_seq), _mod_spec(g2, tiles_per_seq),
                _resident((1, d)), _layer_spec(wup, l), _resident(cw.shape), _resident(cb.shape),
                _layer_spec(wdn, l)]
    args = [x, sc, sh, g2, gw, wup, cw, cb, wdn]
    if final:
        in_specs.append(_resident((1, d)))
        args.append(gf)
    return pl.pallas_call(
        functools.partial(_ffn_prompt_kernel, tiles_per_seq=tiles_per_seq, final=final),
        grid=(m // tm,),
        in_specs=in_specs,
        out_specs=[row, pl.BlockSpec((1, lead, c2), lambda i: (i // tiles_per_seq, 0, 0))],
        out_shape=[jax.ShapeDtypeStruct((m, d), F32), jax.ShapeDtypeStruct((batch, lead, c2), F32)],
        scratch_shapes=[pltpu.VMEM((lead, c2), F32),
                        pltpu.VMEM((FFN_UP_AHEAD + 1, 2, tm + lead, FF_CHUNK), F32),
                        pltpu.VMEM((tm, c2 // 2), BF16)],
        compiler_params=_params(1),
        name="ffn_prompt",
    )(*args)


def _ffn_sample_kernel(x_ref, sc_ref, sh_ref, g2_ref, gw_ref, st_ref, wup_ref, cw_ref, cb_ref, wdn_ref, *rest,
                       final):
    gf_ref = rest[0] if final else None
    o_ref, nst_ref, h_scr, a_scr, act_scr = rest[-5:]
    nb, d = sc_ref.shape
    rows = x_ref.shape[0]
    t_new = rows // nb
    keep = CONV_W - 1
    d_ff = wdn_ref.shape[0]
    c2 = 2 * d_ff
    nbuf = a_scr.shape[0]

    per_row = lambda m: jnp.concatenate([m] * t_new, axis=0)
    xt = jnp.swapaxes(x_ref[...].reshape(nb, t_new, d), 0, 1).reshape(rows, d)
    h_scr[...] = _modnorm(xt, gw_ref[...], per_row(sc_ref[...]), per_row(sh_ref[...])).astype(BF16)

    def up(buf, half, c0):
        cols = slice(c0, c0 + FF_CHUNK)
        for r in range(keep):
            a_scr[buf, half, r * nb:(r + 1) * nb, :] = st_ref[:, r * c2 + c0:r * c2 + c0 + FF_CHUNK]
        a_scr[buf, half, keep * nb:keep * nb + rows, :] = jnp.dot(h_scr[...], wup_ref[:, cols],
                                                                  preferred_element_type=F32)
        for r in range(keep):
            nst_ref[:, r * c2 + c0:r * c2 + c0 + FF_CHUNK] = a_scr[buf, half, rows + r * nb:rows + (r + 1) * nb, :]

    def conv_taps(c0):
        cols = slice(c0, c0 + FF_CHUNK)
        full = lambda row: jnp.broadcast_to(row, (SUBLANES, FF_CHUNK))
        return [full(cw_ref[k:k + 1, cols]) for k in range(CONV_W)] + [full(cb_ref[:, cols])]

    def conv_rows(buf, half, taps, r0):
        blk = lambda k: a_scr[buf, half, r0 + k * nb:r0 + k * nb + FF_ROWS, :].reshape(
            FF_ROWS // SUBLANES, SUBLANES, FF_CHUNK)
        w0, w1, w2, b = taps
        return w0 * blk(0) + w1 * blk(1) + w2 * blk(2) + b

    def gate(buf, cg, cu):
        taps_g, taps_u = conv_taps(cg), conv_taps(cu)
        for r0 in range(0, rows, FF_ROWS):
            act = _silu(conv_rows(buf, 0, taps_g, r0)) * conv_rows(buf, 1, taps_u, r0)
            act_scr[r0:r0 + FF_ROWS, cg:cg + FF_CHUNK] = act.reshape(FF_ROWS, FF_CHUNK).astype(BF16)

    chunks = _ffn_chunks(d_ff)
    acc = []

    def down(c0, c1):
        dn = jnp.dot(act_scr[:, c0:c1], wdn_ref[c0:c1, :], preferred_element_type=F32)
        acc[:] = [dn if not acc else acc[0] + dn]

    pending, gated = [], 0
    for j in range(len(chunks) + FFN_UP_AHEAD):
        if j < len(chunks):
            buf, (cg, cu) = j % nbuf, chunks[j]
            up(buf, 0, cg)
            up(buf, 1, cu)
            pending.append((buf, cg, cu))
        if j >= FFN_UP_AHEAD:
            gate(*pending.pop(0))
            gated += 1
            if gated % FFN_DOWN_GROUP == 0 or gated == len(chunks):
                first = (gated - 1) // FFN_DOWN_GROUP * FFN_DOWN_GROUP
                down(first * FF_CHUNK, gated * FF_CHUNK)
    out = _final(xt + per_row(g2_ref[...]) * acc[0], gf_ref)
    o_ref[...] = jnp.swapaxes(out.reshape(t_new, nb, d), 0, 1).reshape(rows, d)


def _ffn_sample(x, sc, sh, g2, gw, st2, wup, wdn, l, cw, cb, gf, nb):
    db, d = sc.shape
    t_new = x.shape[0] // db
    c2 = wup.shape[-1]
    keep = CONV_W - 1
    rows = t_new * nb
    final = gf is not None
    seqs = lambda width: pl.BlockSpec((nb, width), lambda i: (i, 0))
    xrows = pl.BlockSpec((rows, d), lambda i: (i, 0))
    in_specs = [xrows, seqs(d), seqs(d), seqs(d), _resident((1, d)), seqs(keep * c2),
                _layer_spec(wup, l), _resident(cw.shape), _resident(cb.shape), _layer_spec(wdn, l)]
    args = [x, sc, sh, g2, gw, st2, wup, cw, cb, wdn]
    if final:
        in_specs.append(_resident((1, d)))
        args.append(gf)
    return pl.pallas_call(
        functools.partial(_ffn_sample_kernel, final=final),
        grid=(db // nb,),
        in_specs=in_specs,
        out_specs=[xrows, seqs(keep * c2)],
        out_shape=[jax.ShapeDtypeStruct((db * t_new, d), F32), jax.ShapeDtypeStruct((db, keep * c2), F32)],
        scratch_shapes=[pltpu.VMEM((rows, d), BF16),
                        pltpu.VMEM((FFN_UP_AHEAD + 1, 2, rows + keep * nb, FF_CHUNK), F32),
                        pltpu.VMEM((rows, c2 // 2), BF16)],
        compiler_params=_params(1),
        name="ffn_sample",
    )(*args)


def _tile_rows(size, want):
    return want if size % want == 0 else size


def kernel(x_prompt, x_sample, c_prompt, c_sample, cache_k, cache_v, state_conv, w_ada, b_ada, norm_mix,
           norm_ffn, w_qkv, b_qkv, attn_sink, w_o, w_sgu_in, b_sgu_in, sgu_ln_g, sgu_ln_b, w_spatial,
           b_spatial, w_sgu_out, w_up, conv_w, conv_b, w_down, norm_final):
    batch, seq, d = x_prompt.shape
    db, t_new, _ = x_sample.shape
    depth = w_ada.shape[0]

    pad = (-(batch + db)) % SUBLANES
    c_all = jnp.concatenate([c_prompt, c_sample, jnp.zeros((pad, d), F32)], axis=0)
    mod = _ada(c_all, w_ada, b_ada)

    def mod_part(l, k, lo, hi):
        return mod[l, lo:hi, k * d:(k + 1) * d]

    mp, ms = x_prompt.shape[0] * seq, db * t_new
    xp = x_prompt.reshape(mp, d)
    xs = x_sample.reshape(ms, d)

    tm_qkv_p, tm_sgu_p, tm_ffn_p = _tile_rows(seq, 1024), _tile_rows(seq, 1024), _tile_rows(seq, 1024)
    tm_qkv_s, tm_sgu_s = _tile_rows(ms, 512), _tile_rows(ms, 256)
    nb_ffn_s = _tile_rows(db, 512 // t_new)

    tab_p = _rope_tables(jnp.arange(seq))
    tab_s = _rope_tables(PAST_LEN + (jnp.arange(ms) % t_new))

    wq_all, wo_all = w_qkv.astype(BF16), w_o.astype(BF16)
    win_all, wout_all = w_sgu_in.astype(BF16), w_sgu_out.astype(BF16)
    wup_all, wdn_all = w_up.astype(BF16), w_down.astype(BF16)

    kct = jnp.transpose(cache_k, (0, 1, 3, 4, 2))
    vct = jnp.transpose(cache_v, (0, 1, 3, 4, 2))

    new_k_p, new_v_p, new_conv_p = [], [], []
    new_k_s, new_v_s, new_conv_s, new_sgu_s = [], [], [], []

    for l in range(depth):
        idx = l // N_MIXERS
        gmix = norm_mix[l].reshape(1, d)
        gffn = norm_ffn[l].reshape(1, d)
        p_mod = [mod_part(l, k, 0, batch)[:, None, :] for k in range(6)]
        s_seq = [mod_part(l, k, batch, batch + db) for k in range(6)]
        seq_tiles = lambda a, tm: a.reshape(ms // tm, tm // t_new, d)

        if l % N_MIXERS == 0:
            bq = b_qkv[idx].reshape(1, -1)
            sink = attn_sink[idx] * LOG2E
            keep = min(WINDOW, seq)
            q, k, v, k_last, v_last = _qkv(xp, p_mod[1], p_mod[0], gmix, wq_all, idx, bq, tab_p, tm_qkv_p,
                                           seq // tm_qkv_p, seq // tm_qkv_p, tail_rows=keep)
            nblk = ATTN_BLOCKS_PER_STEP if seq % (ATTN_BLOCKS_PER_STEP * WINDOW) == 0 else 1
            xp = _attn_prompt(sink, q, k, v, wo_all, idx, xp, p_mod[2], batch, nblk)
            new_k_p.append(k_last.reshape(batch, keep, N_KV_HEADS, HEAD_DIM))
            new_v_p.append(v_last.reshape(batch, keep, N_KV_HEADS, HEAD_DIM))

            q, k, v = _qkv(xs, seq_tiles(s_seq[1], tm_qkv_s), seq_tiles(s_seq[0], tm_qkv_s), gmix, wq_all, idx, bq, tab_s,
                           tm_qkv_s, 1, ms // tm_qkv_s)
            xs, nk, nv = _attn_sample(sink, q, k, v, kct, vct, wo_all, idx, xs, s_seq[2], t_new)
            new_k_s.append(nk)
            new_v_s.append(nv)
        else:
            b_in = b_sgu_in[idx].reshape(1, -1)
            lng = sgu_ln_g[idx].reshape(1, -1)
            lnb = sgu_ln_b[idx].reshape(1, -1)
            tc = min(seq, CHUNK)
            (xp,) = _sgu(xp, p_mod[1], p_mod[0], p_mod[2], gmix, win_all, wout_all, idx, b_in, lng, lnb,
                         w_spatial[idx][:, :tc, :tc], b_spatial[idx][:, :tc, None],
                         tm_sgu_p, seq // tm_sgu_p, False)
            reps = CHUNK // t_new
            eye = jnp.eye(reps, dtype=F32)
            wsp_s = jnp.einsum("ab,gts->gatbs", eye, w_spatial[idx][:, :t_new, :t_new]).reshape(
                SGU_GROUPS, CHUNK, CHUNK)
            bsp_s = jnp.tile(b_spatial[idx][:, :t_new], (1, reps))[:, :, None]
            xs, vrows = _sgu(xs, seq_tiles(s_seq[1], tm_sgu_s), seq_tiles(s_seq[0], tm_sgu_s),
                             seq_tiles(s_seq[2], tm_sgu_s), gmix, win_all, wout_all, idx,
                             b_in, lng, lnb, wsp_s, bsp_s, tm_sgu_s, 1, True)
            new_sgu_s.append(vrows.reshape(db, t_new, -1))

        cw = conv_w[l]
        cb = conv_b[l].reshape(1, -1)
        gf = norm_final.reshape(1, d) if l == depth - 1 else None
        xp, tail = _ffn_prompt(xp, p_mod[4], p_mod[3], p_mod[5], gffn, wup_all, wdn_all, l, cw, cb, gf, batch,
                               tm_ffn_p)
        new_conv_p.append(tail[:, SUBLANES - 1::SUBLANES, :])

        xs, nst = _ffn_sample(xs, s_seq[4], s_seq[3], s_seq[5], gffn,
                              state_conv[l].reshape(db, -1), wup_all, wdn_all, l, cw, cb, gf, nb_ffn_s)
        new_conv_s.append(nst.reshape(db, CONV_W - 1, -1))

    return (xp.reshape(batch, seq, d), xs.reshape(db, t_new, d),
            jnp.stack(new_k_p), jnp.stack(new_v_p), jnp.stack(new_conv_p),
            jnp.transpose(jnp.stack(new_k_s), (0, 1, 4, 2, 3)), jnp.transpose(jnp.stack(new_v_s), (0, 1, 4, 2, 3)),
            jnp.stack(new_conv_s), jnp.stack(new_sgu_s))
```

```python
import functools

import jax
import jax.numpy as jnp
from jax import lax
from jax.experimental import pallas as pl
from jax.experimental.pallas import tpu as pltpu

N_HEADS = 16
N_KV_HEADS = 4
HEAD_DIM = 64
Q_PER_KV = N_HEADS // N_KV_HEADS
WINDOW = 128
ROT_DIM = HEAD_DIM // 4
ROPE_THETA = 500000.0
CHUNK = 128
SGU_GROUPS = 4
CONV_W = 3
EPS = 1e-6
N_MIXERS = 2
PAST_LEN = 8192

LANES = 128
SUBLANES = 8
FF_CHUNK = 256
ATTN_BLOCKS_PER_STEP = 8
SGU_ROWS = 64
SGU_LN_ROWS = 16
FF_ROWS = 64
FFN_UP_AHEAD = 3
FFN_DOWN_GROUP = 4
ATTN_SCORES_AHEAD = 5
SAMPLE_UNROLL = 4
LOG2E = 1.4426950408889634
Q_SCALE = HEAD_DIM ** -0.5 * LOG2E
VMEM_LIMIT = 56 * 1024 * 1024
NEG_BIG = -1e30

F32 = jnp.float32
BF16 = jnp.bfloat16


def _params(n_axes=1, vmem=VMEM_LIMIT):
    return pltpu.CompilerParams(dimension_semantics=("arbitrary",) * n_axes, vmem_limit_bytes=vmem)


def _resident(shape):
    nd = len(shape)
    return pl.BlockSpec(shape, lambda *_: (0,) * nd, pipeline_mode=pl.Buffered(1))


def _layer_spec(stacked, l):
    nd = stacked.ndim - 1
    return pl.BlockSpec((None,) + stacked.shape[1:], lambda *_: (l,) + (0,) * nd, pipeline_mode=pl.Buffered(1))


def _per_row(m, rows):
    g, d = m.shape
    if g in (1, rows):
        return m
    return jnp.broadcast_to(m[:, None, :], (g, rows // g, d)).reshape(rows, d)


def _modnorm(x, gw, scale, shift):
    ms = jnp.mean(x * x, axis=-1, keepdims=True)
    y = x * lax.rsqrt(ms + EPS) * gw
    return y * (1.0 + _per_row(scale, x.shape[0])) + _per_row(shift, x.shape[0])


def _silu(x):
    return x * jax.nn.sigmoid(x)


def _gelu_tanh(x):
    c = 0.7978845608028654
    hx = 0.5 * x
    return hx + hx * jnp.tanh(x * (c + (c * 0.044715) * (x * x)))


def _ada_kernel(c_ref, w_ref, b_ref, o_ref):
    s = _silu(c_ref[...]).astype(BF16)
    o_ref[0] = jnp.dot(s, w_ref[0].astype(BF16), preferred_element_type=F32) + b_ref[0]


def _ada(c_all, w_ada, b_ada):
    depth, d, n6 = w_ada.shape
    rows = c_all.shape[0]
    tn = 1024
    return pl.pallas_call(
        _ada_kernel,
        grid=(depth, n6 // tn),
        in_specs=[
            pl.BlockSpec((rows, d), lambda l, n: (0, 0)),
            pl.BlockSpec((1, d, tn), lambda l, n: (l, 0, n)),
            pl.BlockSpec((1, 1, tn), lambda l, n: (l, 0, n)),
        ],
        out_specs=pl.BlockSpec((1, rows, tn), lambda l, n: (l, 0, n)),
        out_shape=jax.ShapeDtypeStruct((depth, rows, n6), F32),
        compiler_params=_params(2),
        name="ada",
    )(c_all, w_ada, b_ada.reshape(depth, 1, n6))


def _qkv_kernel(x_ref, sc_ref, sh_ref, gw_ref, w_ref, b_ref, cos_ref, sa_ref, sb_ref,
                q_ref, k_ref, v_ref, *tails):
    h = _modnorm(x_ref[...], gw_ref[...], sc_ref[0], sh_ref[0]).astype(BF16)
    y = jnp.dot(h, w_ref[...], preferred_element_type=F32) + b_ref[...]
    cos, sa, sb = cos_ref[...], sa_ref[...], sb_ref[...]
    nq = N_HEADS * HEAD_DIM
    nk = N_KV_HEADS * HEAD_DIM

    def rope(yb):
        return yb * cos + pltpu.roll(yb, LANES - ROT_DIM // 2, 1) * sa + pltpu.roll(yb, ROT_DIM // 2, 1) * sb

    for j in range(nq // LANES):
        q_ref[:, j * LANES:(j + 1) * LANES] = (rope(y[:, j * LANES:(j + 1) * LANES]) * Q_SCALE).astype(q_ref.dtype)
    k = [rope(y[:, nq + j * LANES:nq + (j + 1) * LANES]) for j in range(nk // LANES)]
    v = y[:, nq + nk:]
    for j in range(nk // LANES):
        k_ref[:, j * LANES:(j + 1) * LANES] = k[j].astype(k_ref.dtype)
    v_ref[...] = v.astype(v_ref.dtype)
    if tails:
        kt_ref, vt_ref = tails
        first = y.shape[0] - kt_ref.shape[1]
        for j in range(nk // LANES):
            kt_ref[0, :, j * LANES:(j + 1) * LANES] = k[j][first:]
        vt_ref[0] = v[first:]


def _rope_tables(pos):
    half = ROT_DIM // 2
    inv = ROPE_THETA ** (-jnp.arange(0, ROT_DIM, 2, dtype=F32) / ROT_DIM)
    ang = pos.astype(F32)[:, None] * inv[None, :]
    cos, sin = jnp.cos(ang), jnp.sin(ang)
    t = pos.shape[0]
    one = jnp.ones((t, HEAD_DIM - ROT_DIM), F32)
    zero = jnp.zeros((t, HEAD_DIM - ROT_DIM), F32)
    zh = jnp.zeros((t, half), F32)
    cos_t = jnp.concatenate([cos, cos, one], 1)
    sa_t = jnp.concatenate([-sin, zh, zero], 1)
    sb_t = jnp.concatenate([zh, sin, zero], 1)
    rep = LANES // HEAD_DIM
    return tuple(jnp.tile(a, (1, rep)) for a in (cos_t, sa_t, sb_t))


def _mod_spec(mod, tiles_per_group):
    return pl.BlockSpec((1,) + mod.shape[1:], lambda i: (i // tiles_per_group, 0, 0))


def _qkv(x, sc, sh, gw, w, l, b, tables, tm, tiles_per_group, table_tiles, tail_rows=0):
    m, d = x.shape
    n = w.shape[-1]
    nq = N_HEADS * HEAD_DIM
    nk = N_KV_HEADS * HEAD_DIM
    row = lambda width: pl.BlockSpec((tm, width), lambda i: (i, 0))
    tab = pl.BlockSpec((tm, LANES), lambda i: (i % table_tiles, 0))
    dt = BF16 if tail_rows else F32
    tail = pl.BlockSpec((1, tail_rows, nk), lambda i: (i // tiles_per_group, 0, 0))
    tail_shape = jax.ShapeDtypeStruct((m // tm // tiles_per_group, tail_rows, nk), F32)
    return pl.pallas_call(
        _qkv_kernel,
        grid=(m // tm,),
        in_specs=[row(d), _mod_spec(sc, tiles_per_group), _mod_spec(sh, tiles_per_group),
                  _resident((1, d)), _layer_spec(w, l), _resident((1, n)), tab, tab, tab],
        out_specs=[row(nq), row(nk), row(nk)] + [tail, tail] * bool(tail_rows),
        out_shape=[jax.ShapeDtypeStruct((m, nq), dt), jax.ShapeDtypeStruct((m, nk), dt),
                   jax.ShapeDtypeStruct((m, nk), dt)] + [tail_shape, tail_shape] * bool(tail_rows),
        compiler_params=_params(1),
        name="qkv",
    )(x, sc, sh, gw, w, b, *tables)


def _dup_half(x, kv):
    lo = lax.broadcasted_iota(jnp.int32, (1, LANES), 1) < HEAD_DIM
    x = x.astype(F32)
    xr = pltpu.roll(x, HEAD_DIM, 1)
    return jnp.where(lo, x, xr) if kv % 2 == 0 else jnp.where(lo, xr, x)


def _masked_queries(q, kv):
    lo = lax.broadcasted_iota(jnp.int32, (1, LANES), 1) < HEAD_DIM
    q0 = q[:, (2 * kv) * LANES:(2 * kv + 1) * LANES]
    q1 = q[:, (2 * kv + 1) * LANES:(2 * kv + 2) * LANES]
    return jnp.concatenate([jnp.where(lo, q0, 0.0), jnp.where(lo, 0.0, q0),
                            jnp.where(lo, q1, 0.0), jnp.where(lo, 0.0, q1)], axis=0)


def _band(tq, nwin):
    i = lax.broadcasted_iota(jnp.int32, (tq, nwin), 0)
    j = lax.broadcasted_iota(jnp.int32, (tq, nwin), 1)
    diff = WINDOW + i - j
    return jnp.concatenate([(diff >= 0) & (diff <= WINDOW)] * Q_PER_KV, axis=0)


def _band_t(tq, nwin, prev_valid):
    j = lax.broadcasted_iota(jnp.int32, (nwin, tq), 0)
    i = lax.broadcasted_iota(jnp.int32, (nwin, tq), 1)
    diff = WINDOW + i - j
    band = (diff >= 0) & (diff <= WINDOW)
    if prev_valid is not None:
        band = band & ((j >= WINDOW) | prev_valid)
    return jnp.concatenate([band] * Q_PER_KV, axis=1)


def _scores_t(q, kwin, kv):
    kd = _dup_half(kwin[:, (kv // 2) * LANES:(kv // 2 + 1) * LANES], kv)
    return lax.dot_general(kd.astype(BF16), _masked_queries(q, kv).astype(BF16), (((1,), (1,)), ((), ())),
                           preferred_element_type=F32)


def _softmax_pv_t(s, band, vwin, kv, sink_ref):
    tq = s.shape[1] // Q_PER_KV
    nwin = s.shape[0]
    blk = kv // 2
    vt = vwin[:, blk * LANES:(blk + 1) * LANES].astype(F32).T
    vt = jnp.concatenate([vt[(kv % 2) * HEAD_DIM:(kv % 2 + 1) * HEAD_DIM],
                          jnp.ones((2 * SUBLANES, nwin), F32)], axis=0)
    sink = jnp.concatenate([jnp.full((1, tq), sink_ref[Q_PER_KV * kv + g], F32)
                            for g in range(Q_PER_KV)], axis=1)
    s = jnp.where(band, s, NEG_BIG)
    mx = jnp.maximum(jnp.max(s, axis=0, keepdims=True), sink)
    p = jnp.exp2(s - mx)
    o = jnp.dot(vt.astype(BF16), p.astype(BF16), preferred_element_type=F32)
    den = o[HEAD_DIM:HEAD_DIM + 1] + jnp.exp2(sink - mx)
    o = o[:HEAD_DIM] / den
    return [o[:, g * tq:(g + 1) * tq] for g in range(Q_PER_KV)]


def _attn_prompt_kernel(sink_ref, q_ref, ko_ref, kp_ref, vo_ref, vp_ref, wo_ref, x_ref, g1_ref, o_ref):
    nblk = q_ref.shape[0] // WINDOW
    kall = jnp.concatenate([kp_ref[...], ko_ref[...]], axis=0)
    vall = jnp.concatenate([vp_ref[...], vo_ref[...]], axis=0)
    band_first = _band_t(WINDOW, 2 * WINDOW, pl.program_id(1) > 0)
    band_rest = _band_t(WINDOW, 2 * WINDOW, None) if nblk > 1 else None
    heads = {c: [] for c in range(nblk)}

    def finish(c, kv, s):
        win = slice(c * WINDOW, (c + 2) * WINDOW)
        heads[c].extend(_softmax_pv_t(s, band_first if c == 0 else band_rest, vall[win], kv, sink_ref))
        if kv == N_KV_HEADS - 1:
            rows = slice(c * WINDOW, (c + 1) * WINDOW)
            ot = jnp.concatenate(heads[c], axis=0)
            mix = lax.dot_general(ot.astype(BF16), wo_ref[...], (((0,), (0,)), ((), ())),
                                  preferred_element_type=F32)
            o_ref[rows, :] = x_ref[rows, :] + g1_ref[0] * mix

    pending = []
    for c in range(nblk):
        for kv in range(N_KV_HEADS):
            s = _scores_t(q_ref[c * WINDOW:(c + 1) * WINDOW, :], kall[c * WINDOW:(c + 2) * WINDOW], kv)
            pending.append((c, kv, s))
            if len(pending) > ATTN_SCORES_AHEAD:
                finish(*pending.pop(0))
    for unit in pending:
        finish(*unit)


def _attn_prompt(sink, q, k, v, wo, l, x, g1, batch, nblk):
    m, d = x.shape
    tq = nblk * WINDOW
    nb = m // batch // tq
    nk = k.shape[1]
    own = lambda width: pl.BlockSpec((tq, width), lambda b, n: (b * nb + n, 0))
    prev = lambda width: pl.BlockSpec(
        (WINDOW, width), lambda b, n: (b * nb * nblk + jnp.maximum(n * nblk - 1, 0), 0))
    return pl.pallas_call(
        _attn_prompt_kernel,
        grid=(batch, nb),
        in_specs=[pl.BlockSpec(memory_space=pltpu.SMEM), own(q.shape[1]), own(nk), prev(nk), own(nk), prev(nk),
                  _layer_spec(wo, l),
                  own(d), pl.BlockSpec((1, 1, d), lambda b, n: (b, 0, 0))],
        out_specs=own(d),
        out_shape=jax.ShapeDtypeStruct((m, d), F32),
        compiler_params=_params(2),
        name="attn_prompt",
    )(sink, q, k, k, v, v, wo, x, g1)


def _attn_sample_kernel(sink_ref, q_ref, kn_ref, vn_ref, kc_ref, vc_ref, wo_ref, x_ref, g1_ref,
                        o_ref, ko_ref, vo_ref, o_scr, *, t_new):
    nseq, _, _, w = kc_ref.shape
    lanes = lax.broadcasted_iota(jnp.int32, (1, w), 1)
    lo = lanes < HEAD_DIM
    is_new = lanes >= w - t_new
    zpad = jnp.zeros((w - t_new, LANES), F32)
    band = _band(t_new, 2 * w)

    def new_rows(x, blk, at_end, dup_kv=None):
        rows = x[:, blk * LANES:(blk + 1) * LANES]
        if dup_kv is not None:
            rows = _dup_half(rows, dup_kv)
        return jnp.concatenate([zpad, rows] if at_end else [rows, zpad], axis=0)

    def body(it, carry):
        units = []
        for u in range(SAMPLE_UNROLL):
            b = it * SAMPLE_UNROLL + u
            r = pl.multiple_of(b * t_new, t_new)
            kn, vn = kn_ref[pl.ds(r, t_new), :], vn_ref[pl.ds(r, t_new), :]
            q = q_ref[pl.ds(r, t_new), :]
            knt = [new_rows(kn, blk, True).T for blk in range(N_KV_HEADS // 2)]
            vnt = [new_rows(vn, blk, True).T for blk in range(N_KV_HEADS // 2)]
            scores, vals = [], []
            for kv in range(N_KV_HEADS):
                half = slice((kv % 2) * HEAD_DIM, (kv % 2 + 1) * HEAD_DIM)
                kt, vt = kc_ref[b, kv], vc_ref[b, kv]
                ko_ref[b, kv] = jnp.where(is_new, knt[kv // 2][half], pltpu.roll(kt, w - t_new, 1))
                vo_ref[b, kv] = jnp.where(is_new, vnt[kv // 2][half], pltpu.roll(vt, w - t_new, 1))
                lhs = _masked_queries(q, kv).astype(BF16)
                kd_new = new_rows(kn, kv // 2, False, dup_kv=kv).astype(BF16)
                s_old = jnp.dot(lhs, jnp.concatenate([kt, kt], axis=0).astype(BF16), preferred_element_type=F32)
                s_new = lax.dot_general(lhs, kd_new, (((1,), (1,)), ((), ())), preferred_element_type=F32)
                scores.append(jnp.concatenate([s_old, s_new], axis=1))
                vals.append((jnp.concatenate([vt, vt], axis=0).astype(BF16),
                             new_rows(vn, kv // 2, False, dup_kv=kv).astype(BF16)))
            units.append((r, scores, vals))
        for r, scores, vals in units:
            cols = []
            for kv in range(N_KV_HEADS):
                sink = jnp.concatenate([jnp.full((t_new, 1), sink_ref[Q_PER_KV * kv + g], F32)
                                        for g in range(Q_PER_KV)], axis=0)
                s = jnp.where(band, scores[kv], NEG_BIG)
                mx = jnp.maximum(jnp.max(s, axis=-1, keepdims=True), sink)
                p = jnp.exp2(s - mx)
                den = jnp.sum(p, axis=-1, keepdims=True) + jnp.exp2(sink - mx)
                p = p.astype(BF16)
                vd_old, vd_new = vals[kv]
                o = lax.dot_general(p[:, :w], vd_old, (((1,), (1,)), ((), ())), preferred_element_type=F32)
                o = (o + jnp.dot(p[:, w:], vd_new, preferred_element_type=F32)) / den
                cols.append(jnp.where(lo, o[0:t_new], o[t_new:2 * t_new]))
                cols.append(jnp.where(lo, o[2 * t_new:3 * t_new], o[3 * t_new:4 * t_new]))
            o_scr[pl.ds(r, t_new), :] = jnp.concatenate(cols, axis=1)
        return carry

    lax.fori_loop(0, nseq // SAMPLE_UNROLL, body, 0)
    mix = jnp.dot(o_scr[...].astype(BF16), wo_ref[...], preferred_element_type=F32)
    o_ref[...] = x_ref[...] + _per_row(g1_ref[...], mix.shape[0]) * mix


def _attn_sample(sink, q, kn, vn, kct, vct, wo, l, x, g1, t_new):
    m, d = x.shape
    _, db, nkv, hd, w = kct.shape
    nk = nkv * hd
    g = WINDOW // t_new
    tm = g * t_new
    row = lambda width: pl.BlockSpec((tm, width), lambda i: (i, 0))
    cache_in = pl.BlockSpec((None, g, nkv, hd, w), lambda i: (l, i, 0, 0, 0))
    cache_out = pl.BlockSpec((g, nkv, hd, w), lambda i: (i, 0, 0, 0))
    cache_shape = jax.ShapeDtypeStruct((db, nkv, hd, w), F32)
    return pl.pallas_call(
        functools.partial(_attn_sample_kernel, t_new=t_new),
        grid=(db // g,),
        in_specs=[pl.BlockSpec(memory_space=pltpu.SMEM), row(q.shape[1]), row(nk), row(nk), cache_in, cache_in,
                  _layer_spec(wo, l), row(d), pl.BlockSpec((g, d), lambda i: (i, 0))],
        out_specs=[row(d), cache_out, cache_out],
        out_shape=[jax.ShapeDtypeStruct((m, d), F32), cache_shape, cache_shape],
        scratch_shapes=[pltpu.VMEM((tm, q.shape[1]), F32)],
        compiler_params=_params(1),
        name="attn_sample",
    )(sink, q, kn, vn, kct, vct, wo, x, g1)


def _sgu_kernel(x_ref, sc_ref, sh_ref, g1_ref, gw_ref, win_ref, bin_ref, lng_ref, lnb_ref,
                wsp_ref, bsp_ref, wout_ref, o_ref, *rest, emit_v):
    v_out = rest[0] if emit_v else None
    h_scr, z_scr, vb_scr, gated_scr = rest[-4:]
    x = x_ref[...]
    tm = x.shape[0]
    d_sgu = lng_ref.shape[1]
    gdim = d_sgu // SGU_GROUPS
    h_scr[...] = _modnorm(x, gw_ref[...], sc_ref[0], sh_ref[0]).astype(BF16)

    def inproj(c0):
        z_scr[:, c0:c0 + gdim] = jnp.dot(h_scr[...], win_ref[:, c0:c0 + gdim], preferred_element_type=F32)

    def gelu_cols(c0):
        bias = jnp.broadcast_to(bin_ref[:, c0:c0 + gdim], (SUBLANES, gdim))
        for r0 in range(0, tm, SGU_ROWS):
            z = z_scr[r0:r0 + SGU_ROWS, c0:c0 + gdim].reshape(SGU_ROWS // SUBLANES, SUBLANES, gdim)
            z_scr[r0:r0 + SGU_ROWS, c0:c0 + gdim] = _gelu_tanh(z + bias).reshape(SGU_ROWS, gdim)

    def layernorm_v():
        g8 = jnp.broadcast_to(lng_ref[...], (SUBLANES, d_sgu))
        b8 = jnp.broadcast_to(lnb_ref[...], (SUBLANES, d_sgu))
        for r0 in range(0, tm, SGU_LN_ROWS):
            v = z_scr[r0:r0 + SGU_LN_ROWS, d_sgu:].reshape(SGU_LN_ROWS // SUBLANES, SUBLANES, d_sgu)
            vc = v - jnp.mean(v, axis=-1, keepdims=True)
            var = jnp.mean(vc * vc, axis=-1, keepdims=True)
            vn = (vc * lax.rsqrt(var + EPS) * g8 + b8).reshape(SGU_LN_ROWS, d_sgu)
            if emit_v:
                v_out[r0:r0 + SGU_LN_ROWS, :] = vn
            vb_scr[r0:r0 + SGU_LN_ROWS, :] = vn.astype(BF16)

    r = lax.broadcasted_iota(jnp.int32, (CHUNK, CHUNK), 0)
    c = lax.broadcasted_iota(jnp.int32, (CHUNK, CHUNK), 1)

    def mix(g):
        cols = slice(g * gdim, (g + 1) * gdim)
        wm = jnp.where(r >= c, wsp_ref[g], 0.0).astype(BF16)
        for ch in range(tm // CHUNK):
            rows = slice(ch * CHUNK, (ch + 1) * CHUNK)
            mixed = jnp.dot(wm, vb_scr[rows, cols], preferred_element_type=F32) + bsp_ref[g]
            gated_scr[rows, cols] = (z_scr[rows, cols] * mixed).astype(BF16)

    acc = []

    def outproj(g):
        cols = slice(g * gdim, (g + 1) * gdim)
        part = jnp.dot(gated_scr[:, cols], wout_ref[cols, :], preferred_element_type=F32)
        acc[:] = [part if not acc else acc[0] + part]

    ucols = [g * gdim for g in range(SGU_GROUPS)]
    vcols = [d_sgu + g * gdim for g in range(SGU_GROUPS)]
    order = vcols + ucols
    for i, c0 in enumerate(order):
        inproj(c0)
        if i >= 2:
            gelu_cols(order[i - 2])
        if i == len(vcols) + 1:
            layernorm_v()
    gelu_cols(order[-2])
    mix(0)
    gelu_cols(order[-1])
    for g in range(SGU_GROUPS):
        if g + 1 < SGU_GROUPS:
            mix(g + 1)
        outproj(g)
    o_ref[...] = x + _per_row(g1_ref[0], tm) * acc[0]


def _sgu(x, sc, sh, g1, gw, win, wout, l, b_in, lng, lnb, wsp, bsp, tm, tiles_per_group, emit_v):
    m, d = x.shape
    d_sgu = wout.shape[1]
    row = lambda width: pl.BlockSpec((tm, width), lambda i: (i, 0))
    out_specs = [row(d)]
    out_shape = [jax.ShapeDtypeStruct((m, d), F32)]
    if emit_v:
        out_specs.append(row(d_sgu))
        out_shape.append(jax.ShapeDtypeStruct((m, d_sgu), F32))
    return pl.pallas_call(
        functools.partial(_sgu_kernel, emit_v=emit_v),
        grid=(m // tm,),
        in_specs=[row(d), _mod_spec(sc, tiles_per_group), _mod_spec(sh, tiles_per_group),
                  _mod_spec(g1, tiles_per_group), _resident((1, d)), _layer_spec(win, l),
                  _resident(b_in.shape), _resident(lng.shape), _resident(lnb.shape),
                  _resident(wsp.shape), _resident(bsp.shape), _layer_spec(wout, l)],
        out_specs=out_specs,
        out_shape=out_shape,
        scratch_shapes=[pltpu.VMEM((tm, d), BF16), pltpu.VMEM((tm, 2 * d_sgu), F32),
                        pltpu.VMEM((tm, d_sgu), BF16), pltpu.VMEM((tm, d_sgu), BF16)],
        compiler_params=_params(1),
        name="sgu",
    )(x, sc, sh, g1, gw, win, b_in, lng, lnb, wsp, bsp, wout)


def _ffn_chunks(d_ff):
    return [(c, d_ff + c) for c in range(0, d_ff, FF_CHUNK)]


def _final(xn, gf_ref):
    if gf_ref is None:
        return xn
    ms = jnp.mean(xn * xn, axis=-1, keepdims=True)
    return xn * lax.rsqrt(ms + EPS) * gf_ref[...]


def _ffn_prompt_kernel(x_ref, sc_ref, sh_ref, g2_ref, gw_ref, wup_ref, cw_ref, cb_ref, wdn_ref, *rest,
                       tiles_per_seq, final):
    gf_ref = rest[0] if final else None
    o_ref, tail_ref, halo_scr, a_scr, act_scr = rest[-5:]
    d_ff = wdn_ref.shape[0]
    tm, d = x_ref.shape
    seg = tm // SUBLANES
    lead = (CONV_W - 1) * SUBLANES
    nbuf = a_scr.shape[0]

    @pl.when(pl.program_id(0) % tiles_per_seq == 0)
    def _():
        halo_scr[...] = jnp.zeros(halo_scr.shape, F32)

    hs = _modnorm(x_ref[...], gw_ref[...], sc_ref[0], sh_ref[0])
    h = jnp.swapaxes(hs.reshape(SUBLANES, seg, d), 0, 1).reshape(tm, d).astype(BF16)
    first = lax.broadcasted_iota(jnp.int32, (SUBLANES, FF_CHUNK), 0) == 0

    def up(buf, half, c0):
        cols = slice(c0, c0 + FF_CHUNK)
        a_scr[buf, half, lead:lead + tm, :] = jnp.dot(h, wup_ref[:, cols], preferred_element_type=F32)
        for k in range(CONV_W - 1):
            last = a_scr[buf, half, lead + tm - (2 - k) * SUBLANES:lead + tm - (1 - k) * SUBLANES, :]
            prev = halo_scr[k * SUBLANES:(k + 1) * SUBLANES, cols]
            a_scr[buf, half, k * SUBLANES:(k + 1) * SUBLANES, :] = jnp.where(
                first, pltpu.roll(prev, 1, 0), pltpu.roll(last, 1, 0))
            halo_scr[k * SUBLANES:(k + 1) * SUBLANES, cols] = last

    def conv_taps(c0):
        cols = slice(c0, c0 + FF_CHUNK)
        full = lambda row: jnp.broadcast_to(row, (SUBLANES, FF_CHUNK))
        return [full(cw_ref[k:k + 1, cols]) for k in range(CONV_W)] + [full(cb_ref[:, cols])]

    def conv_rows(buf, half, taps, r0):
        blk = lambda k: a_scr[buf, half, r0 + k * SUBLANES:r0 + k * SUBLANES + FF_ROWS, :].reshape(
            FF_ROWS // SUBLANES, SUBLANES, FF_CHUNK)
        w0, w1, w2, b = taps
        return w0 * blk(0) + w1 * blk(1) + w2 * blk(2) + b

    def gate(buf, cg, cu):
        taps_g, taps_u = conv_taps(cg), conv_taps(cu)
        for r0 in range(0, tm, FF_ROWS):
            act = _silu(conv_rows(buf, 0, taps_g, r0)) * conv_rows(buf, 1, taps_u, r0)
            act_scr[r0:r0 + FF_ROWS, cg:cg + FF_CHUNK] = act.reshape(FF_ROWS, FF_CHUNK).astype(BF16)

    chunks = _ffn_chunks(d_ff)
    acc = []

    def down(c0, c1):
        dn = jnp.dot(act_scr[:, c0:c1], wdn_ref[c0:c1, :], preferred_element_type=F32)
        acc[:] = [dn if not acc else acc[0] + dn]

    pending, gated = [], 0
    for j in range(len(chunks) + FFN_UP_AHEAD):
        if j < len(chunks):
            buf, (cg, cu) = j % nbuf, chunks[j]
            up(buf, 0, cg)
            up(buf, 1, cu)
            pending.append((buf, cg, cu))
        if j >= FFN_UP_AHEAD:
            gate(*pending.pop(0))
            gated += 1
            if gated % FFN_DOWN_GROUP == 0 or gated == len(chunks):
                lo = (gated - 1) // FFN_DOWN_GROUP * FFN_DOWN_GROUP
                down(lo * FF_CHUNK, gated * FF_CHUNK)
    tail_ref[0] = halo_scr[...]
    y = jnp.swapaxes(acc[0].reshape(seg, SUBLANES, d), 0, 1).reshape(tm, d)
    o_ref[...] = _final(x_ref[...] + g2_ref[0] * y, gf_ref)


def _ffn_prompt(x, sc, sh, g2, gw, wup, wdn, l, cw, cb, gf, batch, tm):
    m, d = x.shape
    c2 = wup.shape[-1]
    tiles_per_seq = m // batch // tm
    lead = (CONV_W - 1) * SUBLANES
    row = pl.BlockSpec((tm, d), lambda i: (i, 0))
    final = gf is not None
    in_specs = [row, _mod_spec(sc, tiles_per_seq), _mod_spec(sh, tiles_per_seq), _mod_spec(g2, tiles_per_seq),
                _resident((1, d)), _layer_spec(wup, l), _resident(cw.shape), _resident(cb.shape),
                _layer_spec(wdn, l)]
    args = [x, sc, sh, g2, gw, wup, cw, cb, wdn]
    if final:
        in_specs.append(_resident((1, d)))
        args.append(gf)
    return pl.pallas_call(
        functools.partial(_ffn_prompt_kernel, tiles_per_seq=tiles_per_seq, final=final),
        grid=(m // tm,),
        in_specs=in_specs,
        out_specs=[row, pl.BlockSpec((1, lead, c2), lambda i: (i // tiles_per_seq, 0, 0))],
        out_shape=[jax.ShapeDtypeStruct((m, d), F32), jax.ShapeDtypeStruct((batch, lead, c2), F32)],
        scratch_shapes=[pltpu.VMEM((lead, c2), F32),
                        pltpu.VMEM((FFN_UP_AHEAD + 1, 2, tm + lead, FF_CHUNK), F32),
                        pltpu.VMEM((tm, c2 // 2), BF16)],
        compiler_params=_params(1),
        name="ffn_prompt",
    )(*args)


def _ffn_sample_kernel(x_ref, sc_ref, sh_ref, g2_ref, gw_ref, st_ref, wup_ref, cw_ref, cb_ref, wdn_ref, *rest,
                       final):
    gf_ref = rest[0] if final else None
    o_ref, nst_ref, h_scr, a_scr, act_scr = rest[-5:]
    nb, d = sc_ref.shape
    rows = x_ref.shape[0]
    t_new = rows // nb
    keep = CONV_W - 1
    d_ff = wdn_ref.shape[0]
    c2 = 2 * d_ff
    nbuf = a_scr.shape[0]

    per_row = lambda m: jnp.concatenate([m] * t_new, axis=0)
    xt = jnp.swapaxes(x_ref[...].reshape(nb, t_new, d), 0, 1).reshape(rows, d)
    h_scr[...] = _modnorm(xt, gw_ref[...], per_row(sc_ref[...]), per_row(sh_ref[...])).astype(BF16)

    def up(buf, half, c0):
        cols = slice(c0, c0 + FF_CHUNK)
        for r in range(keep):
            a_scr[buf, half, r * nb:(r + 1) * nb, :] = st_ref[:, r, cols]
        a_scr[buf, half, keep * nb:keep * nb + rows, :] = jnp.dot(h_scr[...], wup_ref[:, cols],
                                                                  preferred_element_type=F32)
        for r in range(keep):
            nst_ref[:, r, cols] = a_scr[buf, half, rows + r * nb:rows + (r + 1) * nb, :]

    def conv_taps(c0):
        cols = slice(c0, c0 + FF_CHUNK)
        full = lambda row: jnp.broadcast_to(row, (SUBLANES, FF_CHUNK))
        return [full(cw_ref[k:k + 1, cols]) for k in range(CONV_W)] + [full(cb_ref[:, cols])]

    def conv_rows(buf, half, taps, r0):
        blk = lambda k: a_scr[buf, half, r0 + k * nb:r0 + k * nb + FF_ROWS, :].reshape(
            FF_ROWS // SUBLANES, SUBLANES, FF_CHUNK)
        w0, w1, w2, b = taps
        return w0 * blk(0) + w1 * blk(1) + w2 * blk(2) + b

    def gate(buf, cg, cu):
        taps_g, taps_u = conv_taps(cg), conv_taps(cu)
        for r0 in range(0, rows, FF_ROWS):
            act = _silu(conv_rows(buf, 0, taps_g, r0)) * conv_rows(buf, 1, taps_u, r0)
            act_scr[r0:r0 + FF_ROWS, cg:cg + FF_CHUNK] = act.reshape(FF_ROWS, FF_CHUNK).astype(BF16)

    chunks = _ffn_chunks(d_ff)
    acc = []

    def down(c0, c1):
        dn = jnp.dot(act_scr[:, c0:c1], wdn_ref[c0:c1, :], preferred_element_type=F32)
        acc[:] = [dn if not acc else acc[0] + dn]

    pending, gated = [], 0
    for j in range(len(chunks) + FFN_UP_AHEAD):
        if j < len(chunks):
            buf, (cg, cu) = j % nbuf, chunks[j]
            up(buf, 0, cg)
            up(buf, 1, cu)
            pending.append((buf, cg, cu))
        if j >= FFN_UP_AHEAD:
            gate(*pending.pop(0))
            gated += 1
            if gated % FFN_DOWN_GROUP == 0 or gated == len(chunks):
                first = (gated - 1) // FFN_DOWN_GROUP * FFN_DOWN_GROUP
                down(first * FF_CHUNK, gated * FF_CHUNK)
    out = _final(xt + per_row(g2_ref[...]) * acc[0], gf_ref)
    o_ref[...] = jnp.swapaxes(out.reshape(t_new, nb, d), 0, 1).reshape(rows, d)


def _ffn_sample(x, sc, sh, g2, gw, st2, wup, wdn, l, cw, cb, gf, nb):
    db, d = sc.shape
    t_new = x.shape[0] // db
    c2 = wup.shape[-1]
    keep = CONV_W - 1
    rows = t_new * nb
    final = gf is not None
    seqs = lambda width: pl.BlockSpec((nb, width), lambda i: (i, 0))
    xrows = pl.BlockSpec((rows, d), lambda i: (i, 0))
    state = pl.BlockSpec((nb, keep, c2), lambda i: (i, 0, 0))
    state_in = pl.BlockSpec((None, nb, keep, c2), lambda i: (l, i, 0, 0))
    in_specs = [xrows, seqs(d), seqs(d), seqs(d), _resident((1, d)), state_in,
                _layer_spec(wup, l), _resident(cw.shape), _resident(cb.shape), _layer_spec(wdn, l)]
    args = [x, sc, sh, g2, gw, st2, wup, cw, cb, wdn]
    if final:
        in_specs.append(_resident((1, d)))
        args.append(gf)
    return pl.pallas_call(
        functools.partial(_ffn_sample_kernel, final=final),
        grid=(db // nb,),
        in_specs=in_specs,
        out_specs=[xrows, state],
        out_shape=[jax.ShapeDtypeStruct((db * t_new, d), F32), jax.ShapeDtypeStruct((db, keep, c2), F32)],
        scratch_shapes=[pltpu.VMEM((rows, d), BF16),
                        pltpu.VMEM((FFN_UP_AHEAD + 1, 2, rows + keep * nb, FF_CHUNK), F32),
                        pltpu.VMEM((rows, c2 // 2), BF16)],
        compiler_params=_params(1),
        name="ffn_sample",
    )(*args)


def _tile_rows(size, want):
    return want if size % want == 0 else size


def kernel(x_prompt, x_sample, c_prompt, c_sample, cache_k, cache_v, state_conv, w_ada, b_ada, norm_mix,
           norm_ffn, w_qkv, b_qkv, attn_sink, w_o, w_sgu_in, b_sgu_in, sgu_ln_g, sgu_ln_b, w_spatial,
           b_spatial, w_sgu_out, w_up, conv_w, conv_b, w_down, norm_final):
    batch, seq, d = x_prompt.shape
    db, t_new, _ = x_sample.shape
    depth = w_ada.shape[0]

    pad = (-(batch + db)) % SUBLANES
    c_all = jnp.concatenate([c_prompt, c_sample, jnp.zeros((pad, d), F32)], axis=0)
    mod = _ada(c_all, w_ada, b_ada)

    def mod_part(l, k, lo, hi):
        return mod[l, lo:hi, k * d:(k + 1) * d]

    mp, ms = x_prompt.shape[0] * seq, db * t_new
    xp = x_prompt.reshape(mp, d)
    xs = x_sample.reshape(ms, d)

    tm_qkv_p, tm_sgu_p, tm_ffn_p = _tile_rows(seq, 1024), _tile_rows(seq, 1024), _tile_rows(seq, 1024)
    tm_qkv_s, tm_sgu_s = _tile_rows(ms, 512), _tile_rows(ms, 256)
    nb_ffn_s = _tile_rows(db, 512 // t_new)

    tab_p = _rope_tables(jnp.arange(seq))
    tab_s = _rope_tables(PAST_LEN + (jnp.arange(ms) % t_new))

    wq_all, wo_all = w_qkv.astype(BF16), w_o.astype(BF16)
    win_all, wout_all = w_sgu_in.astype(BF16), w_sgu_out.astype(BF16)
    wup_all, wdn_all = w_up.astype(BF16), w_down.astype(BF16)

    kct = jnp.transpose(cache_k, (0, 1, 3, 4, 2))
    vct = jnp.transpose(cache_v, (0, 1, 3, 4, 2))

    new_k_p, new_v_p, new_conv_p = [], [], []
    new_k_s, new_v_s, new_conv_s, new_sgu_s = [], [], [], []

    for l in range(depth):
        idx = l // N_MIXERS
        gmix = norm_mix[l].reshape(1, d)
        gffn = norm_ffn[l].reshape(1, d)
        p_mod = [mod_part(l, k, 0, batch)[:, None, :] for k in range(6)]
        s_seq = [mod_part(l, k, batch, batch + db) for k in range(6)]
        seq_tiles = lambda a, tm: a.reshape(ms // tm, tm // t_new, d)

        if l % N_MIXERS == 0:
            bq = b_qkv[idx].reshape(1, -1)
            sink = attn_sink[idx] * LOG2E
            keep = min(WINDOW, seq)
            q, k, v, k_last, v_last = _qkv(xp, p_mod[1], p_mod[0], gmix, wq_all, idx, bq, tab_p, tm_qkv_p,
                                           seq // tm_qkv_p, seq // tm_qkv_p, tail_rows=keep)
            nblk = ATTN_BLOCKS_PER_STEP if seq % (ATTN_BLOCKS_PER_STEP * WINDOW) == 0 else 1
            xp = _attn_prompt(sink, q, k, v, wo_all, idx, xp, p_mod[2], batch, nblk)
            new_k_p.append(k_last.reshape(batch, keep, N_KV_HEADS, HEAD_DIM))
            new_v_p.append(v_last.reshape(batch, keep, N_KV_HEADS, HEAD_DIM))

            q, k, v = _qkv(xs, seq_tiles(s_seq[1], tm_qkv_s), seq_tiles(s_seq[0], tm_qkv_s), gmix, wq_all, idx, bq, tab_s,
                           tm_qkv_s, 1, ms // tm_qkv_s)
            xs, nk, nv = _attn_sample(sink, q, k, v, kct, vct, wo_all, idx, xs, s_seq[2], t_new)
            new_k_s.append(nk)
            new_v_s.append(nv)
        else:
            b_in = b_sgu_in[idx].reshape(1, -1)
            lng = sgu_ln_g[idx].reshape(1, -1)
            lnb = sgu_ln_b[idx].reshape(1, -1)
            tc = min(seq, CHUNK)
            (xp,) = _sgu(xp, p_mod[1], p_mod[0], p_mod[2], gmix, win_all, wout_all, idx, b_in, lng, lnb,
                         w_spatial[idx][:, :tc, :tc], b_spatial[idx][:, :tc, None],
                         tm_sgu_p, seq // tm_sgu_p, False)
            reps = CHUNK // t_new
            eye = jnp.eye(reps, dtype=F32)
            wsp_s = jnp.einsum("ab,gts->gatbs", eye, w_spatial[idx][:, :t_new, :t_new]).reshape(
                SGU_GROUPS, CHUNK, CHUNK)
            bsp_s = jnp.tile(b_spatial[idx][:, :t_new], (1, reps))[:, :, None]
            xs, vrows = _sgu(xs, seq_tiles(s_seq[1], tm_sgu_s), seq_tiles(s_seq[0], tm_sgu_s),
                             seq_tiles(s_seq[2], tm_sgu_s), gmix, win_all, wout_all, idx,
                             b_in, lng, lnb, wsp_s, bsp_s, tm_sgu_s, 1, True)
            new_sgu_s.append(vrows.reshape(db, t_new, -1))

        cw = conv_w[l]
        cb = conv_b[l].reshape(1, -1)
        gf = norm_final.reshape(1, d) if l == depth - 1 else None
        xp, tail = _ffn_prompt(xp, p_mod[4], p_mod[3], p_mod[5], gffn, wup_all, wdn_all, l, cw, cb, gf, batch,
                               tm_ffn_p)
        new_conv_p.append(tail[:, SUBLANES - 1::SUBLANES, :])

        xs, nst = _ffn_sample(xs, s_seq[4], s_seq[3], s_seq[5], gffn,
                              state_conv, wup_all, wdn_all, l, cw, cb, gf, nb_ffn_s)
        new_conv_s.append(nst)

    return (xp.reshape(batch, seq, d), xs.reshape(db, t_new, d),
            jnp.stack(new_k_p), jnp.stack(new_v_p), jnp.stack(new_conv_p),
            jnp.transpose(jnp.stack(new_k_s), (0, 1, 4, 2, 3)), jnp.transpose(jnp.stack(new_v_s), (0, 1, 4, 2, 3)),
            jnp.stack(new_conv_s), jnp.stack(new_sgu_s))
```

```python
import functools

import jax
import jax.numpy as jnp
from jax import lax
from jax.experimental import pallas as pl
from jax.experimental.pallas import tpu as pltpu

N_HEADS = 16
N_KV_HEADS = 4
HEAD_DIM = 64
Q_PER_KV = N_HEADS // N_KV_HEADS
WINDOW = 128
ROT_DIM = HEAD_DIM // 4
ROPE_THETA = 500000.0
CHUNK = 128
SGU_GROUPS = 4
CONV_W = 3
EPS = 1e-6
N_MIXERS = 2
PAST_LEN = 8192

LANES = 128
SUBLANES = 8
FF_CHUNK = 256
ATTN_BLOCKS_PER_STEP = 8
SGU_ROWS = 64
SGU_LN_ROWS = 16
FF_ROWS = 64
FFN_UP_AHEAD = 3
FFN_DOWN_GROUP = 4
ATTN_SCORES_AHEAD = 5
SAMPLE_UNROLL = 4
LOG2E = 1.4426950408889634
Q_SCALE = HEAD_DIM ** -0.5 * LOG2E
VMEM_LIMIT = 56 * 1024 * 1024
NEG_BIG = -1e30

F32 = jnp.float32
BF16 = jnp.bfloat16


def _params(n_axes=1, vmem=VMEM_LIMIT):
    return pltpu.CompilerParams(dimension_semantics=("arbitrary",) * n_axes, vmem_limit_bytes=vmem)


def _resident(shape):
    nd = len(shape)
    return pl.BlockSpec(shape, lambda *_: (0,) * nd, pipeline_mode=pl.Buffered(1))


def _layer_spec(w, l):
    if l is None:
        return _resident(w.shape)
    nd = w.ndim - 1
    return pl.BlockSpec((None,) + w.shape[1:], lambda *_: (l,) + (0,) * nd, pipeline_mode=pl.Buffered(1))


def _per_row(m, rows):
    g, d = m.shape
    if g in (1, rows):
        return m
    return jnp.broadcast_to(m[:, None, :], (g, rows // g, d)).reshape(rows, d)


def _modnorm(x, gw, scale, shift):
    ms = jnp.mean(x * x, axis=-1, keepdims=True)
    y = x * lax.rsqrt(ms + EPS) * gw
    return y * (1.0 + _per_row(scale, x.shape[0])) + _per_row(shift, x.shape[0])


def _silu(x):
    return x * jax.nn.sigmoid(x)


def _gelu_tanh(x):
    c = 0.7978845608028654
    hx = 0.5 * x
    return hx + hx * jnp.tanh(x * (c + (c * 0.044715) * (x * x)))


def _ada_kernel(c_ref, w_ref, b_ref, o_ref):
    s = _silu(c_ref[...]).astype(BF16)
    o_ref[0] = jnp.dot(s, w_ref[0].astype(BF16), preferred_element_type=F32) + b_ref[0]


def _ada(c_all, w_ada, b_ada):
    depth, d, n6 = w_ada.shape
    rows = c_all.shape[0]
    tn = 1024
    return pl.pallas_call(
        _ada_kernel,
        grid=(depth, n6 // tn),
        in_specs=[
            pl.BlockSpec((rows, d), lambda l, n: (0, 0)),
            pl.BlockSpec((1, d, tn), lambda l, n: (l, 0, n)),
            pl.BlockSpec((1, 1, tn), lambda l, n: (l, 0, n)),
        ],
        out_specs=pl.BlockSpec((1, rows, tn), lambda l, n: (l, 0, n)),
        out_shape=jax.ShapeDtypeStruct((depth, rows, n6), F32),
        compiler_params=_params(2),
        name="ada",
    )(c_all, w_ada, b_ada.reshape(depth, 1, n6))


def _qkv_kernel(x_ref, sc_ref, sh_ref, gw_ref, w_ref, b_ref, cos_ref, sa_ref, sb_ref,
                q_ref, k_ref, v_ref, *tails):
    h = _modnorm(x_ref[...], gw_ref[...], sc_ref[0], sh_ref[0]).astype(BF16)
    y = jnp.dot(h, w_ref[...], preferred_element_type=F32) + b_ref[...]
    cos, sa, sb = cos_ref[...], sa_ref[...], sb_ref[...]
    nq = N_HEADS * HEAD_DIM
    nk = N_KV_HEADS * HEAD_DIM

    def rope(yb):
        return yb * cos + pltpu.roll(yb, LANES - ROT_DIM // 2, 1) * sa + pltpu.roll(yb, ROT_DIM // 2, 1) * sb

    for j in range(nq // LANES):
        q_ref[:, j * LANES:(j + 1) * LANES] = (rope(y[:, j * LANES:(j + 1) * LANES]) * Q_SCALE).astype(q_ref.dtype)
    k = [rope(y[:, nq + j * LANES:nq + (j + 1) * LANES]) for j in range(nk // LANES)]
    v = y[:, nq + nk:]
    for j in range(nk // LANES):
        k_ref[:, j * LANES:(j + 1) * LANES] = k[j].astype(k_ref.dtype)
    v_ref[...] = v.astype(v_ref.dtype)
    if tails:
        kt_ref, vt_ref = tails
        first = y.shape[0] - kt_ref.shape[1]
        for j in range(nk // LANES):
            kt_ref[0, :, j * LANES:(j + 1) * LANES] = k[j][first:]
        vt_ref[0] = v[first:]


def _rope_tables(pos):
    half = ROT_DIM // 2
    inv = ROPE_THETA ** (-jnp.arange(0, ROT_DIM, 2, dtype=F32) / ROT_DIM)
    ang = pos.astype(F32)[:, None] * inv[None, :]
    cos, sin = jnp.cos(ang), jnp.sin(ang)
    t = pos.shape[0]
    one = jnp.ones((t, HEAD_DIM - ROT_DIM), F32)
    zero = jnp.zeros((t, HEAD_DIM - ROT_DIM), F32)
    zh = jnp.zeros((t, half), F32)
    cos_t = jnp.concatenate([cos, cos, one], 1)
    sa_t = jnp.concatenate([-sin, zh, zero], 1)
    sb_t = jnp.concatenate([zh, sin, zero], 1)
    rep = LANES // HEAD_DIM
    return tuple(jnp.tile(a, (1, rep)) for a in (cos_t, sa_t, sb_t))


def _mod_spec(mod, tiles_per_group):
    return pl.BlockSpec((1,) + mod.shape[1:], lambda i: (i // tiles_per_group, 0, 0))


def _qkv(x, sc, sh, gw, w, l, b, tables, tm, tiles_per_group, table_tiles, tail_rows=0):
    m, d = x.shape
    n = w.shape[-1]
    nq = N_HEADS * HEAD_DIM
    nk = N_KV_HEADS * HEAD_DIM
    row = lambda width: pl.BlockSpec((tm, width), lambda i: (i, 0))
    tab = pl.BlockSpec((tm, LANES), lambda i: (i % table_tiles, 0))
    dt = BF16 if tail_rows else F32
    tail = pl.BlockSpec((1, tail_rows, nk), lambda i: (i // tiles_per_group, 0, 0))
    tail_shape = jax.ShapeDtypeStruct((m // tm // tiles_per_group, tail_rows, nk), F32)
    return pl.pallas_call(
        _qkv_kernel,
        grid=(m // tm,),
        in_specs=[row(d), _mod_spec(sc, tiles_per_group), _mod_spec(sh, tiles_per_group),
                  _resident((1, d)), _layer_spec(w, l), _resident((1, n)), tab, tab, tab],
        out_specs=[row(nq), row(nk), row(nk)] + [tail, tail] * bool(tail_rows),
        out_shape=[jax.ShapeDtypeStruct((m, nq), dt), jax.ShapeDtypeStruct((m, nk), dt),
                   jax.ShapeDtypeStruct((m, nk), dt)] + [tail_shape, tail_shape] * bool(tail_rows),
        compiler_params=_params(1),
        name="qkv",
    )(x, sc, sh, gw, w, b, *tables)


def _dup_half(x, kv):
    lo = lax.broadcasted_iota(jnp.int32, (1, LANES), 1) < HEAD_DIM
    x = x.astype(F32)
    xr = pltpu.roll(x, HEAD_DIM, 1)
    return jnp.where(lo, x, xr) if kv % 2 == 0 else jnp.where(lo, xr, x)


def _masked_queries(q, kv):
    lo = lax.broadcasted_iota(jnp.int32, (1, LANES), 1) < HEAD_DIM
    q0 = q[:, (2 * kv) * LANES:(2 * kv + 1) * LANES]
    q1 = q[:, (2 * kv + 1) * LANES:(2 * kv + 2) * LANES]
    return jnp.concatenate([jnp.where(lo, q0, 0.0), jnp.where(lo, 0.0, q0),
                            jnp.where(lo, q1, 0.0), jnp.where(lo, 0.0, q1)], axis=0)


def _band(tq, nwin):
    i = lax.broadcasted_iota(jnp.int32, (tq, nwin), 0)
    j = lax.broadcasted_iota(jnp.int32, (tq, nwin), 1)
    diff = WINDOW + i - j
    return jnp.concatenate([(diff >= 0) & (diff <= WINDOW)] * Q_PER_KV, axis=0)


def _band_t(tq, nwin, prev_valid):
    j = lax.broadcasted_iota(jnp.int32, (nwin, tq), 0)
    i = lax.broadcasted_iota(jnp.int32, (nwin, tq), 1)
    diff = WINDOW + i - j
    band = (diff >= 0) & (diff <= WINDOW)
    if prev_valid is not None:
        band = band & ((j >= WINDOW) | prev_valid)
    return jnp.concatenate([band] * Q_PER_KV, axis=1)


def _scores_t(q, kwin, kv):
    kd = _dup_half(kwin[:, (kv // 2) * LANES:(kv // 2 + 1) * LANES], kv)
    return lax.dot_general(kd.astype(BF16), _masked_queries(q, kv).astype(BF16), (((1,), (1,)), ((), ())),
                           preferred_element_type=F32)


def _softmax_pv_t(s, band, vwin, kv, sink_ref):
    tq = s.shape[1] // Q_PER_KV
    nwin = s.shape[0]
    blk = kv // 2
    vt = vwin[:, blk * LANES:(blk + 1) * LANES].astype(F32).T
    vt = jnp.concatenate([vt[(kv % 2) * HEAD_DIM:(kv % 2 + 1) * HEAD_DIM],
                          jnp.ones((2 * SUBLANES, nwin), F32)], axis=0)
    sink = jnp.concatenate([jnp.full((1, tq), sink_ref[Q_PER_KV * kv + g], F32)
                            for g in range(Q_PER_KV)], axis=1)
    s = jnp.where(band, s, NEG_BIG)
    mx = jnp.maximum(jnp.max(s, axis=0, keepdims=True), sink)
    p = jnp.exp2(s - mx)
    o = jnp.dot(vt.astype(BF16), p.astype(BF16), preferred_element_type=F32)
    den = o[HEAD_DIM:HEAD_DIM + 1] + jnp.exp2(sink - mx)
    o = o[:HEAD_DIM] / den
    return [o[:, g * tq:(g + 1) * tq] for g in range(Q_PER_KV)]


def _attn_prompt_kernel(sink_ref, q_ref, ko_ref, kp_ref, vo_ref, vp_ref, wo_ref, x_ref, g1_ref, o_ref):
    nblk = q_ref.shape[0] // WINDOW
    kall = jnp.concatenate([kp_ref[...], ko_ref[...]], axis=0)
    vall = jnp.concatenate([vp_ref[...], vo_ref[...]], axis=0)
    band_first = _band_t(WINDOW, 2 * WINDOW, pl.program_id(1) > 0)
    band_rest = _band_t(WINDOW, 2 * WINDOW, None) if nblk > 1 else None
    heads = {c: [] for c in range(nblk)}

    def finish(c, kv, s):
        win = slice(c * WINDOW, (c + 2) * WINDOW)
        heads[c].extend(_softmax_pv_t(s, band_first if c == 0 else band_rest, vall[win], kv, sink_ref))
        if kv == N_KV_HEADS - 1:
            rows = slice(c * WINDOW, (c + 1) * WINDOW)
            ot = jnp.concatenate(heads[c], axis=0)
            mix = lax.dot_general(ot.astype(BF16), wo_ref[...], (((0,), (0,)), ((), ())),
                                  preferred_element_type=F32)
            o_ref[rows, :] = x_ref[rows, :] + g1_ref[0] * mix

    pending = []
    for c in range(nblk):
        for kv in range(N_KV_HEADS):
            s = _scores_t(q_ref[c * WINDOW:(c + 1) * WINDOW, :], kall[c * WINDOW:(c + 2) * WINDOW], kv)
            pending.append((c, kv, s))
            if len(pending) > ATTN_SCORES_AHEAD:
                finish(*pending.pop(0))
    for unit in pending:
        finish(*unit)


def _attn_prompt(sink, q, k, v, wo, l, x, g1, batch, nblk):
    m, d = x.shape
    tq = nblk * WINDOW
    nb = m // batch // tq
    nk = k.shape[1]
    own = lambda width: pl.BlockSpec((tq, width), lambda b, n: (b * nb + n, 0))
    prev = lambda width: pl.BlockSpec(
        (WINDOW, width), lambda b, n: (b * nb * nblk + jnp.maximum(n * nblk - 1, 0), 0))
    return pl.pallas_call(
        _attn_prompt_kernel,
        grid=(batch, nb),
        in_specs=[pl.BlockSpec(memory_space=pltpu.SMEM), own(q.shape[1]), own(nk), prev(nk), own(nk), prev(nk),
                  _layer_spec(wo, l),
                  own(d), pl.BlockSpec((1, 1, d), lambda b, n: (b, 0, 0))],
        out_specs=own(d),
        out_shape=jax.ShapeDtypeStruct((m, d), F32),
        compiler_params=_params(2),
        name="attn_prompt",
    )(sink, q, k, k, v, v, wo, x, g1)


def _attn_sample_kernel(sink_ref, q_ref, kn_ref, vn_ref, kc_ref, vc_ref, wo_ref, x_ref, g1_ref,
                        o_ref, ko_ref, vo_ref, o_scr, *, t_new):
    nseq, _, _, w = kc_ref.shape
    lanes = lax.broadcasted_iota(jnp.int32, (1, w), 1)
    lo = lanes < HEAD_DIM
    is_new = lanes >= w - t_new
    zpad = jnp.zeros((w - t_new, LANES), F32)
    band = _band(t_new, 2 * w)

    def new_rows(x, blk, at_end, dup_kv=None):
        rows = x[:, blk * LANES:(blk + 1) * LANES]
        if dup_kv is not None:
            rows = _dup_half(rows, dup_kv)
        return jnp.concatenate([zpad, rows] if at_end else [rows, zpad], axis=0)

    def body(it, carry):
        units = []
        for u in range(SAMPLE_UNROLL):
            b = it * SAMPLE_UNROLL + u
            r = pl.multiple_of(b * t_new, t_new)
            kn, vn = kn_ref[pl.ds(r, t_new), :], vn_ref[pl.ds(r, t_new), :]
            q = q_ref[pl.ds(r, t_new), :]
            knt = [new_rows(kn, blk, True).T for blk in range(N_KV_HEADS // 2)]
            vnt = [new_rows(vn, blk, True).T for blk in range(N_KV_HEADS // 2)]
            scores, vals = [], []
            for kv in range(N_KV_HEADS):
                half = slice((kv % 2) * HEAD_DIM, (kv % 2 + 1) * HEAD_DIM)
                kt, vt = kc_ref[b, kv], vc_ref[b, kv]
                ko_ref[b, kv] = jnp.where(is_new, knt[kv // 2][half], pltpu.roll(kt, w - t_new, 1))
                vo_ref[b, kv] = jnp.where(is_new, vnt[kv // 2][half], pltpu.roll(vt, w - t_new, 1))
                lhs = _masked_queries(q, kv).astype(BF16)
                kd_new = new_rows(kn, kv // 2, False, dup_kv=kv).astype(BF16)
                s_old = jnp.dot(lhs, jnp.concatenate([kt, kt], axis=0).astype(BF16), preferred_element_type=F32)
                s_new = lax.dot_general(lhs, kd_new, (((1,), (1,)), ((), ())), preferred_element_type=F32)
                scores.append(jnp.concatenate([s_old, s_new], axis=1))
                vals.append((jnp.concatenate([vt, vt], axis=0).astype(BF16),
                             new_rows(vn, kv // 2, False, dup_kv=kv).astype(BF16)))
            units.append((r, scores, vals))
        for r, scores, vals in units:
            cols = []
            for kv in range(N_KV_HEADS):
                sink = jnp.concatenate([jnp.full((t_new, 1), sink_ref[Q_PER_KV * kv + g], F32)
                                        for g in range(Q_PER_KV)], axis=0)
                s = jnp.where(band, scores[kv], NEG_BIG)
                mx = jnp.maximum(jnp.max(s, axis=-1, keepdims=True), sink)
                p = jnp.exp2(s - mx)
                den = jnp.sum(p, axis=-1, keepdims=True) + jnp.exp2(sink - mx)
                p = p.astype(BF16)
                vd_old, vd_new = vals[kv]
                o = lax.dot_general(p[:, :w], vd_old, (((1,), (1,)), ((), ())), preferred_element_type=F32)
                o = (o + jnp.dot(p[:, w:], vd_new, preferred_element_type=F32)) / den
                cols.append(jnp.where(lo, o[0:t_new], o[t_new:2 * t_new]))
                cols.append(jnp.where(lo, o[2 * t_new:3 * t_new], o[3 * t_new:4 * t_new]))
            o_scr[pl.ds(r, t_new), :] = jnp.concatenate(cols, axis=1)
        return carry

    lax.fori_loop(0, nseq // SAMPLE_UNROLL, body, 0)
    mix = jnp.dot(o_scr[...].astype(BF16), wo_ref[...], preferred_element_type=F32)
    o_ref[...] = x_ref[...] + _per_row(g1_ref[...], mix.shape[0]) * mix


def _attn_sample(sink, q, kn, vn, kct, vct, cl, wo, l, x, g1, t_new):
    m, d = x.shape
    _, db, nkv, hd, w = kct.shape
    nk = nkv * hd
    g = WINDOW // t_new
    tm = g * t_new
    row = lambda width: pl.BlockSpec((tm, width), lambda i: (i, 0))
    cache_in = pl.BlockSpec((None, g, nkv, hd, w), lambda i: (cl, i, 0, 0, 0))
    cache_out = pl.BlockSpec((g, nkv, hd, w), lambda i: (i, 0, 0, 0))
    cache_shape = jax.ShapeDtypeStruct((db, nkv, hd, w), F32)
    return pl.pallas_call(
        functools.partial(_attn_sample_kernel, t_new=t_new),
        grid=(db // g,),
        in_specs=[pl.BlockSpec(memory_space=pltpu.SMEM), row(q.shape[1]), row(nk), row(nk), cache_in, cache_in,
                  _layer_spec(wo, l), row(d), pl.BlockSpec((g, d), lambda i: (i, 0))],
        out_specs=[row(d), cache_out, cache_out],
        out_shape=[jax.ShapeDtypeStruct((m, d), F32), cache_shape, cache_shape],
        scratch_shapes=[pltpu.VMEM((tm, q.shape[1]), F32)],
        compiler_params=_params(1),
        name="attn_sample",
    )(sink, q, kn, vn, kct, vct, wo, x, g1)


def _sgu_kernel(x_ref, sc_ref, sh_ref, g1_ref, gw_ref, win_ref, bin_ref, lng_ref, lnb_ref,
                wsp_ref, bsp_ref, wout_ref, o_ref, *rest, emit_v):
    v_out = rest[0] if emit_v else None
    h_scr, z_scr, vb_scr, gated_scr = rest[-4:]
    x = x_ref[...]
    tm = x.shape[0]
    d_sgu = lng_ref.shape[1]
    gdim = d_sgu // SGU_GROUPS
    h_scr[...] = _modnorm(x, gw_ref[...], sc_ref[0], sh_ref[0]).astype(BF16)

    def inproj(c0):
        z_scr[:, c0:c0 + gdim] = jnp.dot(h_scr[...], win_ref[:, c0:c0 + gdim], preferred_element_type=F32)

    def gelu_cols(c0):
        bias = jnp.broadcast_to(bin_ref[:, c0:c0 + gdim], (SUBLANES, gdim))
        for r0 in range(0, tm, SGU_ROWS):
            z = z_scr[r0:r0 + SGU_ROWS, c0:c0 + gdim].reshape(SGU_ROWS // SUBLANES, SUBLANES, gdim)
            z_scr[r0:r0 + SGU_ROWS, c0:c0 + gdim] = _gelu_tanh(z + bias).reshape(SGU_ROWS, gdim)

    def layernorm_v():
        g8 = jnp.broadcast_to(lng_ref[...], (SUBLANES, d_sgu))
        b8 = jnp.broadcast_to(lnb_ref[...], (SUBLANES, d_sgu))
        for r0 in range(0, tm, SGU_LN_ROWS):
            v = z_scr[r0:r0 + SGU_LN_ROWS, d_sgu:].reshape(SGU_LN_ROWS // SUBLANES, SUBLANES, d_sgu)
            vc = v - jnp.mean(v, axis=-1, keepdims=True)
            var = jnp.mean(vc * vc, axis=-1, keepdims=True)
            vn = (vc * lax.rsqrt(var + EPS) * g8 + b8).reshape(SGU_LN_ROWS, d_sgu)
            if emit_v:
                v_out[r0:r0 + SGU_LN_ROWS, :] = vn
            vb_scr[r0:r0 + SGU_LN_ROWS, :] = vn.astype(BF16)

    r = lax.broadcasted_iota(jnp.int32, (CHUNK, CHUNK), 0)
    c = lax.broadcasted_iota(jnp.int32, (CHUNK, CHUNK), 1)

    def mix(g):
        cols = slice(g * gdim, (g + 1) * gdim)
        wm = jnp.where(r >= c, wsp_ref[g], 0.0).astype(BF16)
        for ch in range(tm // CHUNK):
            rows = slice(ch * CHUNK, (ch + 1) * CHUNK)
            mixed = jnp.dot(wm, vb_scr[rows, cols], preferred_element_type=F32) + bsp_ref[g]
            gated_scr[rows, cols] = (z_scr[rows, cols] * mixed).astype(BF16)

    acc = []

    def outproj(g):
        cols = slice(g * gdim, (g + 1) * gdim)
        part = jnp.dot(gated_scr[:, cols], wout_ref[cols, :], preferred_element_type=F32)
        acc[:] = [part if not acc else acc[0] + part]

    ucols = [g * gdim for g in range(SGU_GROUPS)]
    vcols = [d_sgu + g * gdim for g in range(SGU_GROUPS)]
    order = vcols + ucols
    for i, c0 in enumerate(order):
        inproj(c0)
        if i >= 2:
            gelu_cols(order[i - 2])
        if i == len(vcols) + 1:
            layernorm_v()
    gelu_cols(order[-2])
    mix(0)
    gelu_cols(order[-1])
    for g in range(SGU_GROUPS):
        if g + 1 < SGU_GROUPS:
            mix(g + 1)
        outproj(g)
    o_ref[...] = x + _per_row(g1_ref[0], tm) * acc[0]


def _sgu(x, sc, sh, g1, gw, win, wout, l, b_in, lng, lnb, wsp, bsp, tm, tiles_per_group, emit_v):
    m, d = x.shape
    d_sgu = wout.shape[-2]
    row = lambda width: pl.BlockSpec((tm, width), lambda i: (i, 0))
    out_specs = [row(d)]
    out_shape = [jax.ShapeDtypeStruct((m, d), F32)]
    if emit_v:
        out_specs.append(row(d_sgu))
        out_shape.append(jax.ShapeDtypeStruct((m, d_sgu), F32))
    return pl.pallas_call(
        functools.partial(_sgu_kernel, emit_v=emit_v),
        grid=(m // tm,),
        in_specs=[row(d), _mod_spec(sc, tiles_per_group), _mod_spec(sh, tiles_per_group),
                  _mod_spec(g1, tiles_per_group), _resident((1, d)), _layer_spec(win, l),
                  _resident(b_in.shape), _resident(lng.shape), _resident(lnb.shape),
                  _resident(wsp.shape), _resident(bsp.shape), _layer_spec(wout, l)],
        out_specs=out_specs,
        out_shape=out_shape,
        scratch_shapes=[pltpu.VMEM((tm, d), BF16), pltpu.VMEM((tm, 2 * d_sgu), F32),
                        pltpu.VMEM((tm, d_sgu), BF16), pltpu.VMEM((tm, d_sgu), BF16)],
        compiler_params=_params(1),
        name="sgu",
    )(x, sc, sh, g1, gw, win, b_in, lng, lnb, wsp, bsp, wout)


def _ffn_chunks(d_ff):
    return [(c, d_ff + c) for c in range(0, d_ff, FF_CHUNK)]


def _final(xn, gf_ref):
    if gf_ref is None:
        return xn
    ms = jnp.mean(xn * xn, axis=-1, keepdims=True)
    return xn * lax.rsqrt(ms + EPS) * gf_ref[...]


def _ffn_prompt_kernel(x_ref, sc_ref, sh_ref, g2_ref, gw_ref, wup_ref, cw_ref, cb_ref, wdn_ref, *rest,
                       tiles_per_seq, final, n_cast):
    gf_ref = rest[0] if final else None
    cast_in = rest[int(final):int(final) + n_cast]
    o_ref, tail_ref = rest[int(final) + n_cast:int(final) + n_cast + 2]
    cast_out = rest[int(final) + n_cast + 2:int(final) + 2 * n_cast + 2]
    halo_scr, a_scr, act_scr = rest[-3:]
    for src, dst in zip(cast_in, cast_out):
        dst[...] = src[...].astype(BF16)
    d_ff = wdn_ref.shape[0]
    tm, d = x_ref.shape
    seg = tm // SUBLANES
    lead = (CONV_W - 1) * SUBLANES
    nbuf = a_scr.shape[0]

    @pl.when(pl.program_id(0) % tiles_per_seq == 0)
    def _():
        halo_scr[...] = jnp.zeros(halo_scr.shape, F32)

    hs = _modnorm(x_ref[...], gw_ref[...], sc_ref[0], sh_ref[0])
    h = jnp.swapaxes(hs.reshape(SUBLANES, seg, d), 0, 1).reshape(tm, d).astype(BF16)
    first = lax.broadcasted_iota(jnp.int32, (SUBLANES, FF_CHUNK), 0) == 0

    def up(buf, half, c0):
        cols = slice(c0, c0 + FF_CHUNK)
        a_scr[buf, half, lead:lead + tm, :] = jnp.dot(h, wup_ref[:, cols], preferred_element_type=F32)
        for k in range(CONV_W - 1):
            last = a_scr[buf, half, lead + tm - (2 - k) * SUBLANES:lead + tm - (1 - k) * SUBLANES, :]
            prev = halo_scr[k * SUBLANES:(k + 1) * SUBLANES, cols]
            a_scr[buf, half, k * SUBLANES:(k + 1) * SUBLANES, :] = jnp.where(
                first, pltpu.roll(prev, 1, 0), pltpu.roll(last, 1, 0))
            halo_scr[k * SUBLANES:(k + 1) * SUBLANES, cols] = last

    def conv_taps(c0):
        cols = slice(c0, c0 + FF_CHUNK)
        full = lambda row: jnp.broadcast_to(row, (SUBLANES, FF_CHUNK))
        return [full(cw_ref[k:k + 1, cols]) for k in range(CONV_W)] + [full(cb_ref[:, cols])]

    def conv_rows(buf, half, taps, r0):
        blk = lambda k: a_scr[buf, half, r0 + k * SUBLANES:r0 + k * SUBLANES + FF_ROWS, :].reshape(
            FF_ROWS // SUBLANES, SUBLANES, FF_CHUNK)
        w0, w1, w2, b = taps
        return w0 * blk(0) + w1 * blk(1) + w2 * blk(2) + b

    def gate(buf, cg, cu):
        taps_g, taps_u = conv_taps(cg), conv_taps(cu)
        for r0 in range(0, tm, FF_ROWS):
            act = _silu(conv_rows(buf, 0, taps_g, r0)) * conv_rows(buf, 1, taps_u, r0)
            act_scr[r0:r0 + FF_ROWS, cg:cg + FF_CHUNK] = act.reshape(FF_ROWS, FF_CHUNK).astype(BF16)

    chunks = _ffn_chunks(d_ff)
    acc = []

    def down(c0, c1):
        dn = jnp.dot(act_scr[:, c0:c1], wdn_ref[c0:c1, :], preferred_element_type=F32)
        acc[:] = [dn if not acc else acc[0] + dn]

    pending, gated = [], 0
    for j in range(len(chunks) + FFN_UP_AHEAD):
        if j < len(chunks):
            buf, (cg, cu) = j % nbuf, chunks[j]
            up(buf, 0, cg)
            up(buf, 1, cu)
            pending.append((buf, cg, cu))
        if j >= FFN_UP_AHEAD:
            gate(*pending.pop(0))
            gated += 1
            if gated % FFN_DOWN_GROUP == 0 or gated == len(chunks):
                lo = (gated - 1) // FFN_DOWN_GROUP * FFN_DOWN_GROUP
                down(lo * FF_CHUNK, gated * FF_CHUNK)
    tail_ref[0] = halo_scr[...]
    y = jnp.swapaxes(acc[0].reshape(seg, SUBLANES, d), 0, 1).reshape(tm, d)
    o_ref[...] = _final(x_ref[...] + g2_ref[0] * y, gf_ref)


def _ffn_prompt(x, sc, sh, g2, gw, wup, wdn, l, cw, cb, gf, batch, tm, casts=()):
    m, d = x.shape
    c2 = wup.shape[-1]
    tiles_per_seq = m // batch // tm
    lead = (CONV_W - 1) * SUBLANES
    row = pl.BlockSpec((tm, d), lambda i: (i, 0))
    final = gf is not None
    in_specs = [row, _mod_spec(sc, tiles_per_seq), _mod_spec(sh, tiles_per_seq), _mod_spec(g2, tiles_per_seq),
                _resident((1, d)), _layer_spec(wup, l), _resident(cw.shape), _resident(cb.shape),
                _layer_spec(wdn, l)]
    args = [x, sc, sh, g2, gw, wup, cw, cb, wdn]
    if final:
        in_specs.append(_resident((1, d)))
        args.append(gf)
    steps = m // tm
    cast_specs, cast_shapes = [], []
    for stack, li in casts:
        _, r, c = stack.shape
        in_specs.append(pl.BlockSpec((None, r // steps, c), lambda i, li=li: (li, i, 0)))
        args.append(stack)
        cast_specs.append(pl.BlockSpec((r // steps, c), lambda i: (i, 0)))
        cast_shapes.append(jax.ShapeDtypeStruct((r, c), BF16))
    return pl.pallas_call(
        functools.partial(_ffn_prompt_kernel, tiles_per_seq=tiles_per_seq, final=final, n_cast=len(casts)),
        grid=(steps,),
        in_specs=in_specs,
        out_specs=[row, pl.BlockSpec((1, lead, c2), lambda i: (i // tiles_per_seq, 0, 0))] + cast_specs,
        out_shape=[jax.ShapeDtypeStruct((m, d), F32), jax.ShapeDtypeStruct((batch, lead, c2), F32)] + cast_shapes,
        scratch_shapes=[pltpu.VMEM((lead, c2), F32),
                        pltpu.VMEM((FFN_UP_AHEAD + 1, 2, tm + lead, FF_CHUNK), F32),
                        pltpu.VMEM((tm, c2 // 2), BF16)],
        compiler_params=_params(1),
        name="ffn_prompt",
    )(*args)


def _ffn_sample_kernel(x_ref, sc_ref, sh_ref, g2_ref, gw_ref, st_ref, wup_ref, cw_ref, cb_ref, wdn_ref, *rest,
                       final):
    gf_ref = rest[0] if final else None
    o_ref, nst_ref, h_scr, a_scr, act_scr = rest[-5:]
    nb, d = sc_ref.shape
    rows = x_ref.shape[0]
    t_new = rows // nb
    keep = CONV_W - 1
    d_ff = wdn_ref.shape[0]
    c2 = 2 * d_ff
    nbuf = a_scr.shape[0]

    per_row = lambda m: jnp.concatenate([m] * t_new, axis=0)
    xt = jnp.swapaxes(x_ref[...].reshape(nb, t_new, d), 0, 1).reshape(rows, d)
    h_scr[...] = _modnorm(xt, gw_ref[...], per_row(sc_ref[...]), per_row(sh_ref[...])).astype(BF16)

    def up(buf, half, c0):
        cols = slice(c0, c0 + FF_CHUNK)
        for r in range(keep):
            a_scr[buf, half, r * nb:(r + 1) * nb, :] = st_ref[:, r, cols]
        a_scr[buf, half, keep * nb:keep * nb + rows, :] = jnp.dot(h_scr[...], wup_ref[:, cols],
                                                                  preferred_element_type=F32)
        for r in range(keep):
            nst_ref[:, r, cols] = a_scr[buf, half, rows + r * nb:rows + (r + 1) * nb, :]

    def conv_taps(c0):
        cols = slice(c0, c0 + FF_CHUNK)
        full = lambda row: jnp.broadcast_to(row, (SUBLANES, FF_CHUNK))
        return [full(cw_ref[k:k + 1, cols]) for k in range(CONV_W)] + [full(cb_ref[:, cols])]

    def conv_rows(buf, half, taps, r0):
        blk = lambda k: a_scr[buf, half, r0 + k * nb:r0 + k * nb + FF_ROWS, :].reshape(
            FF_ROWS // SUBLANES, SUBLANES, FF_CHUNK)
        w0, w1, w2, b = taps
        return w0 * blk(0) + w1 * blk(1) + w2 * blk(2) + b

    def gate(buf, cg, cu):
        taps_g, taps_u = conv_taps(cg), conv_taps(cu)
        for r0 in range(0, rows, FF_ROWS):
            act = _silu(conv_rows(buf, 0, taps_g, r0)) * conv_rows(buf, 1, taps_u, r0)
            act_scr[r0:r0 + FF_ROWS, cg:cg + FF_CHUNK] = act.reshape(FF_ROWS, FF_CHUNK).astype(BF16)

    chunks = _ffn_chunks(d_ff)
    acc = []

    def down(c0, c1):
        dn = jnp.dot(act_scr[:, c0:c1], wdn_ref[c0:c1, :], preferred_element_type=F32)
        acc[:] = [dn if not acc else acc[0] + dn]

    pending, gated = [], 0
    for j in range(len(chunks) + FFN_UP_AHEAD):
        if j < len(chunks):
            buf, (cg, cu) = j % nbuf, chunks[j]
            up(buf, 0, cg)
            up(buf, 1, cu)
            pending.append((buf, cg, cu))
        if j >= FFN_UP_AHEAD:
            gate(*pending.pop(0))
            gated += 1
            if gated % FFN_DOWN_GROUP == 0 or gated == len(chunks):
                first = (gated - 1) // FFN_DOWN_GROUP * FFN_DOWN_GROUP
                down(first * FF_CHUNK, gated * FF_CHUNK)
    out = _final(xt + per_row(g2_ref[...]) * acc[0], gf_ref)
    o_ref[...] = jnp.swapaxes(out.reshape(t_new, nb, d), 0, 1).reshape(rows, d)


def _ffn_sample(x, sc, sh, g2, gw, st2, sl, wup, wdn, l, cw, cb, gf, nb):
    db, d = sc.shape
    t_new = x.shape[0] // db
    c2 = wup.shape[-1]
    keep = CONV_W - 1
    rows = t_new * nb
    final = gf is not None
    seqs = lambda width: pl.BlockSpec((nb, width), lambda i: (i, 0))
    xrows = pl.BlockSpec((rows, d), lambda i: (i, 0))
    state = pl.BlockSpec((nb, keep, c2), lambda i: (i, 0, 0))
    state_in = pl.BlockSpec((None, nb, keep, c2), lambda i: (sl, i, 0, 0))
    in_specs = [xrows, seqs(d), seqs(d), seqs(d), _resident((1, d)), state_in,
                _layer_spec(wup, l), _resident(cw.shape), _resident(cb.shape), _layer_spec(wdn, l)]
    args = [x, sc, sh, g2, gw, st2, wup, cw, cb, wdn]
    if final:
        in_specs.append(_resident((1, d)))
        args.append(gf)
    return pl.pallas_call(
        functools.partial(_ffn_sample_kernel, final=final),
        grid=(db // nb,),
        in_specs=in_specs,
        out_specs=[xrows, state],
        out_shape=[jax.ShapeDtypeStruct((db * t_new, d), F32), jax.ShapeDtypeStruct((db, keep, c2), F32)],
        scratch_shapes=[pltpu.VMEM((rows, d), BF16),
                        pltpu.VMEM((FFN_UP_AHEAD + 1, 2, rows + keep * nb, FF_CHUNK), F32),
                        pltpu.VMEM((rows, c2 // 2), BF16)],
        compiler_params=_params(1),
        name="ffn_sample",
    )(*args)


def _tile_rows(size, want):
    return want if size % want == 0 else size


def kernel(x_prompt, x_sample, c_prompt, c_sample, cache_k, cache_v, state_conv, w_ada, b_ada, norm_mix,
           norm_ffn, w_qkv, b_qkv, attn_sink, w_o, w_sgu_in, b_sgu_in, sgu_ln_g, sgu_ln_b, w_spatial,
           b_spatial, w_sgu_out, w_up, conv_w, conv_b, w_down, norm_final):
    batch, seq, d = x_prompt.shape
    db, t_new, _ = x_sample.shape
    depth = w_ada.shape[0]

    pad = (-(batch + db)) % SUBLANES
    c_all = jnp.concatenate([c_prompt, c_sample, jnp.zeros((pad, d), F32)], axis=0)
    mod = _ada(c_all, w_ada, b_ada)

    def mod_part(l, k, lo, hi):
        return mod[l, lo:hi, k * d:(k + 1) * d]

    mp, ms = x_prompt.shape[0] * seq, db * t_new
    xp = x_prompt.reshape(mp, d)
    xs = x_sample.reshape(ms, d)

    tm_qkv_p, tm_sgu_p, tm_ffn_p = _tile_rows(seq, 1024), _tile_rows(seq, 1024), _tile_rows(seq, 1024)
    tm_qkv_s, tm_sgu_s = _tile_rows(ms, 512), _tile_rows(ms, 256)
    nb_ffn_s = _tile_rows(db, 512 // t_new)

    tab_p = _rope_tables(jnp.arange(seq))
    tab_s = _rope_tables(PAST_LEN + (jnp.arange(ms) % t_new))

    bf = {"up": w_up[0].astype(BF16), "down": w_down[0].astype(BF16),
          "a": w_qkv[0].astype(BF16), "b": w_o[0].astype(BF16)}

    kct = jnp.transpose(cache_k, (0, 1, 3, 4, 2))
    vct = jnp.transpose(cache_v, (0, 1, 3, 4, 2))

    new_k_p, new_v_p, new_conv_p = [], [], []
    new_k_s, new_v_s, new_conv_s, new_sgu_s = [], [], [], []

    for l in range(depth):
        idx = l // N_MIXERS
        gmix = norm_mix[l].reshape(1, d)
        gffn = norm_ffn[l].reshape(1, d)
        p_mod = [mod_part(l, k, 0, batch)[:, None, :] for k in range(6)]
        s_seq = [mod_part(l, k, batch, batch + db) for k in range(6)]
        seq_tiles = lambda a, tm: a.reshape(ms // tm, tm // t_new, d)

        if l % N_MIXERS == 0:
            bq = b_qkv[idx].reshape(1, -1)
            sink = attn_sink[idx] * LOG2E
            keep = min(WINDOW, seq)
            q, k, v, k_last, v_last = _qkv(xp, p_mod[1], p_mod[0], gmix, bf["a"], None, bq, tab_p, tm_qkv_p,
                                           seq // tm_qkv_p, seq // tm_qkv_p, tail_rows=keep)
            nblk = ATTN_BLOCKS_PER_STEP if seq % (ATTN_BLOCKS_PER_STEP * WINDOW) == 0 else 1
            xp = _attn_prompt(sink, q, k, v, bf["b"], None, xp, p_mod[2], batch, nblk)
            new_k_p.append(k_last.reshape(batch, keep, N_KV_HEADS, HEAD_DIM))
            new_v_p.append(v_last.reshape(batch, keep, N_KV_HEADS, HEAD_DIM))

            q, k, v = _qkv(xs, seq_tiles(s_seq[1], tm_qkv_s), seq_tiles(s_seq[0], tm_qkv_s), gmix, bf["a"], None, bq, tab_s,
                           tm_qkv_s, 1, ms // tm_qkv_s)
            xs, nk, nv = _attn_sample(sink, q, k, v, kct, vct, idx, bf["b"], None, xs, s_seq[2], t_new)
            new_k_s.append(nk)
            new_v_s.append(nv)
        else:
            b_in = b_sgu_in[idx].reshape(1, -1)
            lng = sgu_ln_g[idx].reshape(1, -1)
            lnb = sgu_ln_b[idx].reshape(1, -1)
            tc = min(seq, CHUNK)
            (xp,) = _sgu(xp, p_mod[1], p_mod[0], p_mod[2], gmix, bf["a"], bf["b"], None, b_in, lng, lnb,
                         w_spatial[idx][:, :tc, :tc], b_spatial[idx][:, :tc, None],
                         tm_sgu_p, seq // tm_sgu_p, False)
            reps = CHUNK // t_new
            eye = jnp.eye(reps, dtype=F32)
            wsp_s = jnp.einsum("ab,gts->gatbs", eye, w_spatial[idx][:, :t_new, :t_new]).reshape(
                SGU_GROUPS, CHUNK, CHUNK)
            bsp_s = jnp.tile(b_spatial[idx][:, :t_new], (1, reps))[:, :, None]
            xs, vrows = _sgu(xs, seq_tiles(s_seq[1], tm_sgu_s), seq_tiles(s_seq[0], tm_sgu_s),
                             seq_tiles(s_seq[2], tm_sgu_s), gmix, bf["a"], bf["b"], None,
                             b_in, lng, lnb, wsp_s, bsp_s, tm_sgu_s, 1, True)
            new_sgu_s.append(vrows.reshape(db, t_new, -1))

        cw = conv_w[l]
        cb = conv_b[l].reshape(1, -1)
        gf = norm_final.reshape(1, d) if l == depth - 1 else None
        casts = []
        if l + 1 < depth:
            nxt = (l + 1) // N_MIXERS
            mixer = [(w_qkv, nxt), (w_o, nxt)] if (l + 1) % N_MIXERS == 0 else [(w_sgu_in, nxt), (w_sgu_out, nxt)]
            casts = [(w_up, l + 1), (w_down, l + 1)] + mixer
        xp, tail, *cast = _ffn_prompt(xp, p_mod[4], p_mod[3], p_mod[5], gffn, bf["up"], bf["down"], None, cw, cb, gf,
                                      batch, tm_ffn_p, casts)
        new_conv_p.append(tail[:, SUBLANES - 1::SUBLANES, :])

        xs, nst = _ffn_sample(xs, s_seq[4], s_seq[3], s_seq[5], gffn,
                              state_conv, l, bf["up"], bf["down"], None, cw, cb, gf, nb_ffn_s)
        new_conv_s.append(nst)
        if cast:
            bf = dict(zip(("up", "down", "a", "b"), cast))

    return (xp.reshape(batch, seq, d), xs.reshape(db, t_new, d),
            jnp.stack(new_k_p), jnp.stack(new_v_p), jnp.stack(new_conv_p),
            jnp.transpose(jnp.stack(new_k_s), (0, 1, 4, 2, 3)), jnp.transpose(jnp.stack(new_v_s), (0, 1, 4, 2, 3)),
            jnp.stack(new_conv_s), jnp.stack(new_sgu_s))
```

```python
import functools

import jax
import jax.numpy as jnp
from jax import lax
from jax.experimental import pallas as pl
from jax.experimental.pallas import tpu as pltpu

N_HEADS = 16
N_KV_HEADS = 4
HEAD_DIM = 64
Q_PER_KV = N_HEADS // N_KV_HEADS
WINDOW = 128
ROT_DIM = HEAD_DIM // 4
ROPE_THETA = 500000.0
CHUNK = 128
SGU_GROUPS = 4
CONV_W = 3
EPS = 1e-6
N_MIXERS = 2
PAST_LEN = 8192

LANES = 128
SUBLANES = 8
FF_CHUNK = 256
ATTN_BLOCKS_PER_STEP = 8
SGU_ROWS = 64
SGU_LN_ROWS = 16
FF_ROWS = 64
FFN_UP_AHEAD = 3
FFN_DOWN_GROUP = 4
ATTN_SCORES_AHEAD = 5
ATTN_OPROJ_GROUP = 2
SAMPLE_UNROLL = 4
LOG2E = 1.4426950408889634
Q_SCALE = HEAD_DIM ** -0.5 * LOG2E
VMEM_LIMIT = 56 * 1024 * 1024
NEG_BIG = -1e30

F32 = jnp.float32
BF16 = jnp.bfloat16


def _params(n_axes=1, vmem=VMEM_LIMIT):
    return pltpu.CompilerParams(dimension_semantics=("arbitrary",) * n_axes, vmem_limit_bytes=vmem)


def _resident(shape):
    nd = len(shape)
    return pl.BlockSpec(shape, lambda *_: (0,) * nd, pipeline_mode=pl.Buffered(1))


def _layer_spec(w, l):
    if l is None:
        return _resident(w.shape)
    nd = w.ndim - 1
    return pl.BlockSpec((None,) + w.shape[1:], lambda *_: (l,) + (0,) * nd, pipeline_mode=pl.Buffered(1))


def _per_row(m, rows):
    g, d = m.shape
    if g in (1, rows):
        return m
    return jnp.broadcast_to(m[:, None, :], (g, rows // g, d)).reshape(rows, d)


def _modnorm(x, gw, scale, shift):
    ms = jnp.mean(x * x, axis=-1, keepdims=True)
    y = x * lax.rsqrt(ms + EPS) * gw
    return y * (1.0 + _per_row(scale, x.shape[0])) + _per_row(shift, x.shape[0])


def _silu(x):
    return x * jax.nn.sigmoid(x)


def _gelu_tanh(x):
    c = 0.7978845608028654
    hx = 0.5 * x
    return hx + hx * jnp.tanh(x * (c + (c * 0.044715) * (x * x)))


def _ada_kernel(c_ref, w_ref, b_ref, o_ref):
    s = _silu(c_ref[...]).astype(BF16)
    o_ref[0] = jnp.dot(s, w_ref[0].astype(BF16), preferred_element_type=F32) + b_ref[0]


def _ada(c_all, w_ada, b_ada):
    depth, d, n6 = w_ada.shape
    rows = c_all.shape[0]
    tn = 1024
    return pl.pallas_call(
        _ada_kernel,
        grid=(depth, n6 // tn),
        in_specs=[
            pl.BlockSpec((rows, d), lambda l, n: (0, 0)),
            pl.BlockSpec((1, d, tn), lambda l, n: (l, 0, n)),
            pl.BlockSpec((1, 1, tn), lambda l, n: (l, 0, n)),
        ],
        out_specs=pl.BlockSpec((1, rows, tn), lambda l, n: (l, 0, n)),
        out_shape=jax.ShapeDtypeStruct((depth, rows, n6), F32),
        compiler_params=_params(2),
        name="ada",
    )(c_all, w_ada, b_ada.reshape(depth, 1, n6))


def _qkv_kernel(x_ref, sc_ref, sh_ref, gw_ref, w_ref, b_ref, cos_ref, sa_ref, sb_ref,
                q_ref, k_ref, v_ref, *tails):
    h = _modnorm(x_ref[...], gw_ref[...], sc_ref[0], sh_ref[0]).astype(BF16)
    y = jnp.dot(h, w_ref[...], preferred_element_type=F32) + b_ref[...]
    cos, sa, sb = cos_ref[...], sa_ref[...], sb_ref[...]
    nq = N_HEADS * HEAD_DIM
    nk = N_KV_HEADS * HEAD_DIM

    def rope(yb):
        return yb * cos + pltpu.roll(yb, LANES - ROT_DIM // 2, 1) * sa + pltpu.roll(yb, ROT_DIM // 2, 1) * sb

    for j in range(nq // LANES):
        q_ref[:, j * LANES:(j + 1) * LANES] = (rope(y[:, j * LANES:(j + 1) * LANES]) * Q_SCALE).astype(q_ref.dtype)
    k = [rope(y[:, nq + j * LANES:nq + (j + 1) * LANES]) for j in range(nk // LANES)]
    v = y[:, nq + nk:]
    for j in range(nk // LANES):
        k_ref[:, j * LANES:(j + 1) * LANES] = k[j].astype(k_ref.dtype)
    v_ref[...] = v.astype(v_ref.dtype)
    if tails:
        kt_ref, vt_ref = tails
        first = y.shape[0] - kt_ref.shape[1]
        for j in range(nk // LANES):
            kt_ref[0, :, j * LANES:(j + 1) * LANES] = k[j][first:]
        vt_ref[0] = v[first:]


def _rope_tables(pos):
    half = ROT_DIM // 2
    inv = ROPE_THETA ** (-jnp.arange(0, ROT_DIM, 2, dtype=F32) / ROT_DIM)
    ang = pos.astype(F32)[:, None] * inv[None, :]
    cos, sin = jnp.cos(ang), jnp.sin(ang)
    t = pos.shape[0]
    one = jnp.ones((t, HEAD_DIM - ROT_DIM), F32)
    zero = jnp.zeros((t, HEAD_DIM - ROT_DIM), F32)
    zh = jnp.zeros((t, half), F32)
    cos_t = jnp.concatenate([cos, cos, one], 1)
    sa_t = jnp.concatenate([-sin, zh, zero], 1)
    sb_t = jnp.concatenate([zh, sin, zero], 1)
    rep = LANES // HEAD_DIM
    return tuple(jnp.tile(a, (1, rep)) for a in (cos_t, sa_t, sb_t))


def _mod_spec(mod, tiles_per_group):
    return pl.BlockSpec((1,) + mod.shape[1:], lambda i: (i // tiles_per_group, 0, 0))


def _qkv(x, sc, sh, gw, w, l, b, tables, tm, tiles_per_group, table_tiles, tail_rows=0):
    m, d = x.shape
    n = w.shape[-1]
    nq = N_HEADS * HEAD_DIM
    nk = N_KV_HEADS * HEAD_DIM
    row = lambda width: pl.BlockSpec((tm, width), lambda i: (i, 0))
    tab = pl.BlockSpec((tm, LANES), lambda i: (i % table_tiles, 0))
    dt = BF16 if tail_rows else F32
    tail = pl.BlockSpec((1, tail_rows, nk), lambda i: (i // tiles_per_group, 0, 0))
    tail_shape = jax.ShapeDtypeStruct((m // tm // tiles_per_group, tail_rows, nk), F32)
    return pl.pallas_call(
        _qkv_kernel,
        grid=(m // tm,),
        in_specs=[row(d), _mod_spec(sc, tiles_per_group), _mod_spec(sh, tiles_per_group),
                  _resident((1, d)), _layer_spec(w, l), _resident((1, n)), tab, tab, tab],
        out_specs=[row(nq), row(nk), row(nk)] + [tail, tail] * bool(tail_rows),
        out_shape=[jax.ShapeDtypeStruct((m, nq), dt), jax.ShapeDtypeStruct((m, nk), dt),
                   jax.ShapeDtypeStruct((m, nk), dt)] + [tail_shape, tail_shape] * bool(tail_rows),
        compiler_params=_params(1),
        name="qkv",
    )(x, sc, sh, gw, w, b, *tables)


def _dup_half(x, kv):
    lo = lax.broadcasted_iota(jnp.int32, (1, LANES), 1) < HEAD_DIM
    x = x.astype(F32)
    xr = pltpu.roll(x, HEAD_DIM, 1)
    return jnp.where(lo, x, xr) if kv % 2 == 0 else jnp.where(lo, xr, x)


def _masked_queries(q, kv):
    lo = lax.broadcasted_iota(jnp.int32, (1, LANES), 1) < HEAD_DIM
    q0 = q[:, (2 * kv) * LANES:(2 * kv + 1) * LANES]
    q1 = q[:, (2 * kv + 1) * LANES:(2 * kv + 2) * LANES]
    return jnp.concatenate([jnp.where(lo, q0, 0.0), jnp.where(lo, 0.0, q0),
                            jnp.where(lo, q1, 0.0), jnp.where(lo, 0.0, q1)], axis=0)


def _band(tq, nwin):
    i = lax.broadcasted_iota(jnp.int32, (tq, nwin), 0)
    j = lax.broadcasted_iota(jnp.int32, (tq, nwin), 1)
    diff = WINDOW + i - j
    return jnp.concatenate([(diff >= 0) & (diff <= WINDOW)] * Q_PER_KV, axis=0)


def _band_t(tq, nwin, prev_valid):
    j = lax.broadcasted_iota(jnp.int32, (nwin, tq), 0)
    i = lax.broadcasted_iota(jnp.int32, (nwin, tq), 1)
    diff = WINDOW + i - j
    band = (diff >= 0) & (diff <= WINDOW)
    if prev_valid is not None:
        band = band & ((j >= WINDOW) | prev_valid)
    return jnp.concatenate([band] * Q_PER_KV, axis=1)


def _scores_t(q, kwin, kv):
    kd = _dup_half(kwin[:, (kv // 2) * LANES:(kv // 2 + 1) * LANES], kv)
    return lax.dot_general(kd.astype(BF16), _masked_queries(q, kv).astype(BF16), (((1,), (1,)), ((), ())),
                           preferred_element_type=F32)


def _softmax_pv_t(s, band, vwin, kv, sink_ref):
    tq = s.shape[1] // Q_PER_KV
    nwin = s.shape[0]
    blk = kv // 2
    vt = vwin[:, blk * LANES:(blk + 1) * LANES].astype(F32).T
    vt = jnp.concatenate([vt[(kv % 2) * HEAD_DIM:(kv % 2 + 1) * HEAD_DIM],
                          jnp.ones((2 * SUBLANES, nwin), F32)], axis=0)
    sink = jnp.concatenate([jnp.full((1, tq), sink_ref[Q_PER_KV * kv + g], F32)
                            for g in range(Q_PER_KV)], axis=1)
    s = jnp.where(band, s, NEG_BIG)
    mx = jnp.maximum(jnp.max(s, axis=0, keepdims=True), sink)
    p = jnp.exp2(s - mx)
    o = jnp.dot(vt.astype(BF16), p.astype(BF16), preferred_element_type=F32)
    den = o[HEAD_DIM:HEAD_DIM + 1] + jnp.exp2(sink - mx)
    o = o[:HEAD_DIM] / den
    return [o[:, g * tq:(g + 1) * tq] for g in range(Q_PER_KV)]


def _attn_prompt_kernel(sink_ref, q_ref, ko_ref, kp_ref, vo_ref, vp_ref, wo_ref, x_ref, g1_ref, o_ref):
    nblk = q_ref.shape[0] // WINDOW
    kall = jnp.concatenate([kp_ref[...], ko_ref[...]], axis=0)
    vall = jnp.concatenate([vp_ref[...], vo_ref[...]], axis=0)
    band_first = _band_t(WINDOW, 2 * WINDOW, pl.program_id(1) > 0)
    band_rest = _band_t(WINDOW, 2 * WINDOW, None) if nblk > 1 else None
    heads = {c: [] for c in range(nblk)}

    def finish(c, kv, s):
        win = slice(c * WINDOW, (c + 2) * WINDOW)
        heads[c].extend(_softmax_pv_t(s, band_first if c == 0 else band_rest, vall[win], kv, sink_ref))
        group = min(ATTN_OPROJ_GROUP, nblk)
        if kv == N_KV_HEADS - 1 and c % group == group - 1:
            rows = slice((c + 1 - group) * WINDOW, (c + 1) * WINDOW)
            ot = jnp.concatenate([jnp.concatenate(heads[b], axis=0) for b in range(c + 1 - group, c + 1)], axis=1)
            mix = lax.dot_general(ot.astype(BF16), wo_ref[...], (((0,), (0,)), ((), ())),
                                  preferred_element_type=F32)
            o_ref[rows, :] = x_ref[rows, :] + g1_ref[0] * mix

    pending = []
    for c in range(nblk):
        for kv in range(N_KV_HEADS):
            s = _scores_t(q_ref[c * WINDOW:(c + 1) * WINDOW, :], kall[c * WINDOW:(c + 2) * WINDOW], kv)
            pending.append((c, kv, s))
            if len(pending) > ATTN_SCORES_AHEAD:
                finish(*pending.pop(0))
    for unit in pending:
        finish(*unit)


def _attn_prompt(sink, q, k, v, wo, l, x, g1, batch, nblk):
    m, d = x.shape
    tq = nblk * WINDOW
    nb = m // batch // tq
    nk = k.shape[1]
    own = lambda width: pl.BlockSpec((tq, width), lambda b, n: (b * nb + n, 0))
    prev = lambda width: pl.BlockSpec(
        (WINDOW, width), lambda b, n: (b * nb * nblk + jnp.maximum(n * nblk - 1, 0), 0))
    return pl.pallas_call(
        _attn_prompt_kernel,
        grid=(batch, nb),
        in_specs=[pl.BlockSpec(memory_space=pltpu.SMEM), own(q.shape[1]), own(nk), prev(nk), own(nk), prev(nk),
                  _layer_spec(wo, l),
                  own(d), pl.BlockSpec((1, 1, d), lambda b, n: (b, 0, 0))],
        out_specs=own(d),
        out_shape=jax.ShapeDtypeStruct((m, d), F32),
        compiler_params=_params(2),
        name="attn_prompt",
    )(sink, q, k, k, v, v, wo, x, g1)


def _attn_sample_kernel(sink_ref, q_ref, kn_ref, vn_ref, kc_ref, vc_ref, wo_ref, x_ref, g1_ref,
                        o_ref, ko_ref, vo_ref, o_scr, *, t_new):
    nseq, _, _, w = kc_ref.shape
    lanes = lax.broadcasted_iota(jnp.int32, (1, w), 1)
    lo = lanes < HEAD_DIM
    is_new = lanes >= w - t_new
    zpad = jnp.zeros((w - t_new, LANES), F32)
    band = _band(t_new, 2 * w)

    def new_rows(x, blk, at_end, dup_kv=None):
        rows = x[:, blk * LANES:(blk + 1) * LANES]
        if dup_kv is not None:
            rows = _dup_half(rows, dup_kv)
        return jnp.concatenate([zpad, rows] if at_end else [rows, zpad], axis=0)

    def body(it, carry):
        units = []
        for u in range(SAMPLE_UNROLL):
            b = it * SAMPLE_UNROLL + u
            r = pl.multiple_of(b * t_new, t_new)
            kn, vn = kn_ref[pl.ds(r, t_new), :], vn_ref[pl.ds(r, t_new), :]
            q = q_ref[pl.ds(r, t_new), :]
            knt = [new_rows(kn, blk, True).T for blk in range(N_KV_HEADS // 2)]
            vnt = [new_rows(vn, blk, True).T for blk in range(N_KV_HEADS // 2)]
            scores, vals = [], []
            for kv in range(N_KV_HEADS):
                half = slice((kv % 2) * HEAD_DIM, (kv % 2 + 1) * HEAD_DIM)
                kt, vt = kc_ref[b, kv], vc_ref[b, kv]
                ko_ref[b, kv] = jnp.where(is_new, knt[kv // 2][half], pltpu.roll(kt, w - t_new, 1))
                vo_ref[b, kv] = jnp.where(is_new, vnt[kv // 2][half], pltpu.roll(vt, w - t_new, 1))
                lhs = _masked_queries(q, kv).astype(BF16)
                kd_new = new_rows(kn, kv // 2, False, dup_kv=kv).astype(BF16)
                s_old = jnp.dot(lhs, jnp.concatenate([kt, kt], axis=0).astype(BF16), preferred_element_type=F32)
                s_new = lax.dot_general(lhs, kd_new, (((1,), (1,)), ((), ())), preferred_element_type=F32)
                scores.append(jnp.concatenate([s_old, s_new], axis=1))
                vals.append((jnp.concatenate([vt, vt], axis=0).astype(BF16),
                             new_rows(vn, kv // 2, False, dup_kv=kv).astype(BF16)))
            units.append((r, scores, vals))
        for r, scores, vals in units:
            cols = []
            for kv in range(N_KV_HEADS):
                sink = jnp.concatenate([jnp.full((t_new, 1), sink_ref[Q_PER_KV * kv + g], F32)
                                        for g in range(Q_PER_KV)], axis=0)
                s = jnp.where(band, scores[kv], NEG_BIG)
                mx = jnp.maximum(jnp.max(s, axis=-1, keepdims=True), sink)
                p = jnp.exp2(s - mx)
                den = jnp.sum(p, axis=-1, keepdims=True) + jnp.exp2(sink - mx)
                p = p.astype(BF16)
                vd_old, vd_new = vals[kv]
                o = lax.dot_general(p[:, :w], vd_old, (((1,), (1,)), ((), ())), preferred_element_type=F32)
                o = (o + jnp.dot(p[:, w:], vd_new, preferred_element_type=F32)) / den
                cols.append(jnp.where(lo, o[0:t_new], o[t_new:2 * t_new]))
                cols.append(jnp.where(lo, o[2 * t_new:3 * t_new], o[3 * t_new:4 * t_new]))
            o_scr[pl.ds(r, t_new), :] = jnp.concatenate(cols, axis=1)
        return carry

    lax.fori_loop(0, nseq // SAMPLE_UNROLL, body, 0)
    mix = jnp.dot(o_scr[...].astype(BF16), wo_ref[...], preferred_element_type=F32)
    o_ref[...] = x_ref[...] + _per_row(g1_ref[...], mix.shape[0]) * mix


def _attn_sample(sink, q, kn, vn, kct, vct, cl, wo, l, x, g1, t_new):
    m, d = x.shape
    _, db, nkv, hd, w = kct.shape
    nk = nkv * hd
    g = WINDOW // t_new
    tm = g * t_new
    row = lambda width: pl.BlockSpec((tm, width), lambda i: (i, 0))
    cache_in = pl.BlockSpec((None, g, nkv, hd, w), lambda i: (cl, i, 0, 0, 0))
    cache_out = pl.BlockSpec((g, nkv, hd, w), lambda i: (i, 0, 0, 0))
    cache_shape = jax.ShapeDtypeStruct((db, nkv, hd, w), F32)
    return pl.pallas_call(
        functools.partial(_attn_sample_kernel, t_new=t_new),
        grid=(db // g,),
        in_specs=[pl.BlockSpec(memory_space=pltpu.SMEM), row(q.shape[1]), row(nk), row(nk), cache_in, cache_in,
                  _layer_spec(wo, l), row(d), pl.BlockSpec((g, d), lambda i: (i, 0))],
        out_specs=[row(d), cache_out, cache_out],
        out_shape=[jax.ShapeDtypeStruct((m, d), F32), cache_shape, cache_shape],
        scratch_shapes=[pltpu.VMEM((tm, q.shape[1]), F32)],
        compiler_params=_params(1),
        name="attn_sample",
    )(sink, q, kn, vn, kct, vct, wo, x, g1)


def _sgu_kernel(x_ref, sc_ref, sh_ref, g1_ref, gw_ref, win_ref, bin_ref, lng_ref, lnb_ref,
                wsp_ref, bsp_ref, wout_ref, o_ref, *rest, emit_v):
    v_out = rest[0] if emit_v else None
    h_scr, z_scr, vb_scr, gated_scr = rest[-4:]
    x = x_ref[...]
    tm = x.shape[0]
    d_sgu = lng_ref.shape[1]
    gdim = d_sgu // SGU_GROUPS
    h_scr[...] = _modnorm(x, gw_ref[...], sc_ref[0], sh_ref[0]).astype(BF16)

    def inproj(c0):
        z_scr[:, c0:c0 + gdim] = jnp.dot(h_scr[...], win_ref[:, c0:c0 + gdim], preferred_element_type=F32)

    def gelu_cols(c0):
        bias = jnp.broadcast_to(bin_ref[:, c0:c0 + gdim], (SUBLANES, gdim))
        for r0 in range(0, tm, SGU_ROWS):
            z = z_scr[r0:r0 + SGU_ROWS, c0:c0 + gdim].reshape(SGU_ROWS // SUBLANES, SUBLANES, gdim)
            z_scr[r0:r0 + SGU_ROWS, c0:c0 + gdim] = _gelu_tanh(z + bias).reshape(SGU_ROWS, gdim)

    def layernorm_v():
        g8 = jnp.broadcast_to(lng_ref[...], (SUBLANES, d_sgu))
        b8 = jnp.broadcast_to(lnb_ref[...], (SUBLANES, d_sgu))
        for r0 in range(0, tm, SGU_LN_ROWS):
            v = z_scr[r0:r0 + SGU_LN_ROWS, d_sgu:].reshape(SGU_LN_ROWS // SUBLANES, SUBLANES, d_sgu)
            vc = v - jnp.mean(v, axis=-1, keepdims=True)
            var = jnp.mean(vc * vc, axis=-1, keepdims=True)
            vn = (vc * lax.rsqrt(var + EPS) * g8 + b8).reshape(SGU_LN_ROWS, d_sgu)
            if emit_v:
                v_out[r0:r0 + SGU_LN_ROWS, :] = vn
            vb_scr[r0:r0 + SGU_LN_ROWS, :] = vn.astype(BF16)

    r = lax.broadcasted_iota(jnp.int32, (CHUNK, CHUNK), 0)
    c = lax.broadcasted_iota(jnp.int32, (CHUNK, CHUNK), 1)

    def mix(g):
        cols = slice(g * gdim, (g + 1) * gdim)
        wm = jnp.where(r >= c, wsp_ref[g], 0.0).astype(BF16)
        for ch in range(tm // CHUNK):
            rows = slice(ch * CHUNK, (ch + 1) * CHUNK)
            mixed = jnp.dot(wm, vb_scr[rows, cols], preferred_element_type=F32) + bsp_ref[g]
            gated_scr[rows, cols] = (z_scr[rows, cols] * mixed).astype(BF16)

    acc = []

    def outproj(g):
        cols = slice(g * gdim, (g + 1) * gdim)
        part = jnp.dot(gated_scr[:, cols], wout_ref[cols, :], preferred_element_type=F32)
        acc[:] = [part if not acc else acc[0] + part]

    ucols = [g * gdim for g in range(SGU_GROUPS)]
    vcols = [d_sgu + g * gdim for g in range(SGU_GROUPS)]
    order = vcols + ucols
    for i, c0 in enumerate(order):
        inproj(c0)
        if i >= 2:
            gelu_cols(order[i - 2])
        if i == len(vcols) + 1:
            layernorm_v()
    gelu_cols(order[-2])
    mix(0)
    gelu_cols(order[-1])
    for g in range(SGU_GROUPS):
        if g + 1 < SGU_GROUPS:
            mix(g + 1)
        outproj(g)
    o_ref[...] = x + _per_row(g1_ref[0], tm) * acc[0]


def _sgu(x, sc, sh, g1, gw, win, wout, l, b_in, lng, lnb, wsp, bsp, tm, tiles_per_group, emit_v):
    m, d = x.shape
    d_sgu = wout.shape[-2]
    row = lambda width: pl.BlockSpec((tm, width), lambda i: (i, 0))
    out_specs = [row(d)]
    out_shape = [jax.ShapeDtypeStruct((m, d), F32)]
    if emit_v:
        out_specs.append(row(d_sgu))
        out_shape.append(jax.ShapeDtypeStruct((m, d_sgu), F32))
    return pl.pallas_call(
        functools.partial(_sgu_kernel, emit_v=emit_v),
        grid=(m // tm,),
        in_specs=[row(d), _mod_spec(sc, tiles_per_group), _mod_spec(sh, tiles_per_group),
                  _mod_spec(g1, tiles_per_group), _resident((1, d)), _layer_spec(win, l),
                  _resident(b_in.shape), _resident(lng.shape), _resident(lnb.shape),
                  _resident(wsp.shape), _resident(bsp.shape), _layer_spec(wout, l)],
        out_specs=out_specs,
        out_shape=out_shape,
        scratch_shapes=[pltpu.VMEM((tm, d), BF16), pltpu.VMEM((tm, 2 * d_sgu), F32),
                        pltpu.VMEM((tm, d_sgu), BF16), pltpu.VMEM((tm, d_sgu), BF16)],
        compiler_params=_params(1),
        name="sgu",
    )(x, sc, sh, g1, gw, win, b_in, lng, lnb, wsp, bsp, wout)


def _ffn_chunks(d_ff):
    return [(c, d_ff + c) for c in range(0, d_ff, FF_CHUNK)]


def _final(xn, gf_ref):
    if gf_ref is None:
        return xn
    ms = jnp.mean(xn * xn, axis=-1, keepdims=True)
    return xn * lax.rsqrt(ms + EPS) * gf_ref[...]


def _ffn_prompt_kernel(x_ref, sc_ref, sh_ref, g2_ref, gw_ref, wup_ref, cw_ref, cb_ref, wdn_ref, *rest,
                       tiles_per_seq, final, n_cast):
    gf_ref = rest[0] if final else None
    cast_in = rest[int(final):int(final) + n_cast]
    o_ref, tail_ref = rest[int(final) + n_cast:int(final) + n_cast + 2]
    cast_out = rest[int(final) + n_cast + 2:int(final) + 2 * n_cast + 2]
    halo_scr, a_scr, act_scr = rest[-3:]
    for src, dst in zip(cast_in, cast_out):
        dst[...] = src[...].astype(BF16)
    d_ff = wdn_ref.shape[0]
    tm, d = x_ref.shape
    seg = tm // SUBLANES
    lead = (CONV_W - 1) * SUBLANES
    nbuf = a_scr.shape[0]

    @pl.when(pl.program_id(0) % tiles_per_seq == 0)
    def _():
        halo_scr[...] = jnp.zeros(halo_scr.shape, F32)

    hs = _modnorm(x_ref[...], gw_ref[...], sc_ref[0], sh_ref[0])
    h = jnp.swapaxes(hs.reshape(SUBLANES, seg, d), 0, 1).reshape(tm, d).astype(BF16)
    first = lax.broadcasted_iota(jnp.int32, (SUBLANES, FF_CHUNK), 0) == 0

    def up(buf, half, c0):
        cols = slice(c0, c0 + FF_CHUNK)
        a_scr[buf, half, lead:lead + tm, :] = jnp.dot(h, wup_ref[:, cols], preferred_element_type=F32)
        for k in range(CONV_W - 1):
            last = a_scr[buf, half, lead + tm - (2 - k) * SUBLANES:lead + tm - (1 - k) * SUBLANES, :]
            prev = halo_scr[k * SUBLANES:(k + 1) * SUBLANES, cols]
            a_scr[buf, half, k * SUBLANES:(k + 1) * SUBLANES, :] = jnp.where(
                first, pltpu.roll(prev, 1, 0), pltpu.roll(last, 1, 0))
            halo_scr[k * SUBLANES:(k + 1) * SUBLANES, cols] = last

    def conv_taps(c0):
        cols = slice(c0, c0 + FF_CHUNK)
        full = lambda row: jnp.broadcast_to(row, (SUBLANES, FF_CHUNK))
        return [full(cw_ref[k:k + 1, cols]) for k in range(CONV_W)] + [full(cb_ref[:, cols])]

    def conv_rows(buf, half, taps, r0):
        blk = lambda k: a_scr[buf, half, r0 + k * SUBLANES:r0 + k * SUBLANES + FF_ROWS, :].reshape(
            FF_ROWS // SUBLANES, SUBLANES, FF_CHUNK)
        w0, w1, w2, b = taps
        return w0 * blk(0) + w1 * blk(1) + w2 * blk(2) + b

    def gate(buf, cg, cu):
        taps_g, taps_u = conv_taps(cg), conv_taps(cu)
        for r0 in range(0, tm, FF_ROWS):
            act = _silu(conv_rows(buf, 0, taps_g, r0)) * conv_rows(buf, 1, taps_u, r0)
            act_scr[r0:r0 + FF_ROWS, cg:cg + FF_CHUNK] = act.reshape(FF_ROWS, FF_CHUNK).astype(BF16)

    chunks = _ffn_chunks(d_ff)
    acc = []

    def down(c0, c1):
        dn = jnp.dot(act_scr[:, c0:c1], wdn_ref[c0:c1, :], preferred_element_type=F32)
        acc[:] = [dn if not acc else acc[0] + dn]

    pending, gated = [], 0
    for j in range(len(chunks) + FFN_UP_AHEAD):
        if j < len(chunks):
            buf, (cg, cu) = j % nbuf, chunks[j]
            up(buf, 0, cg)
            up(buf, 1, cu)
            pending.append((buf, cg, cu))
        if j >= FFN_UP_AHEAD:
            gate(*pending.pop(0))
            gated += 1
            if gated % FFN_DOWN_GROUP == 0 or gated == len(chunks):
                lo = (gated - 1) // FFN_DOWN_GROUP * FFN_DOWN_GROUP
                down(lo * FF_CHUNK, gated * FF_CHUNK)
    tail_ref[0] = halo_scr[...]
    y = jnp.swapaxes(acc[0].reshape(seg, SUBLANES, d), 0, 1).reshape(tm, d)
    o_ref[...] = _final(x_ref[...] + g2_ref[0] * y, gf_ref)


def _ffn_prompt(x, sc, sh, g2, gw, wup, wdn, l, cw, cb, gf, batch, tm, casts=()):
    m, d = x.shape
    c2 = wup.shape[-1]
    tiles_per_seq = m // batch // tm
    lead = (CONV_W - 1) * SUBLANES
    row = pl.BlockSpec((tm, d), lambda i: (i, 0))
    final = gf is not None
    in_specs = [row, _mod_spec(sc, tiles_per_seq), _mod_spec(sh, tiles_per_seq), _mod_spec(g2, tiles_per_seq),
                _resident((1, d)), _layer_spec(wup, l), _resident(cw.shape), _resident(cb.shape),
                _layer_spec(wdn, l)]
    args = [x, sc, sh, g2, gw, wup, cw, cb, wdn]
    if final:
        in_specs.append(_resident((1, d)))
        args.append(gf)
    steps = m // tm
    cast_specs, cast_shapes = [], []
    for stack, li in casts:
        _, r, c = stack.shape
        in_specs.append(pl.BlockSpec((None, r // steps, c), lambda i, li=li: (li, i, 0)))
        args.append(stack)
        cast_specs.append(pl.BlockSpec((r // steps, c), lambda i: (i, 0)))
        cast_shapes.append(jax.ShapeDtypeStruct((r, c), BF16))
    return pl.pallas_call(
        functools.partial(_ffn_prompt_kernel, tiles_per_seq=tiles_per_seq, final=final, n_cast=len(casts)),
        grid=(steps,),
        in_specs=in_specs,
        out_specs=[row, pl.BlockSpec((1, lead, c2), lambda i: (i // tiles_per_seq, 0, 0))] + cast_specs,
        out_shape=[jax.ShapeDtypeStruct((m, d), F32), jax.ShapeDtypeStruct((batch, lead, c2), F32)] + cast_shapes,
        scratch_shapes=[pltpu.VMEM((lead, c2), F32),
                        pltpu.VMEM((FFN_UP_AHEAD + 1, 2, tm + lead, FF_CHUNK), F32),
                        pltpu.VMEM((tm, c2 // 2), BF16)],
        compiler_params=_params(1),
        name="ffn_prompt",
    )(*args)


def _ffn_sample_kernel(x_ref, sc_ref, sh_ref, g2_ref, gw_ref, st_ref, wup_ref, cw_ref, cb_ref, wdn_ref, *rest,
                       final):
    gf_ref = rest[0] if final else None
    o_ref, nst_ref, h_scr, a_scr, act_scr = rest[-5:]
    nb, d = sc_ref.shape
    rows = x_ref.shape[0]
    t_new = rows // nb
    keep = CONV_W - 1
    d_ff = wdn_ref.shape[0]
    c2 = 2 * d_ff
    nbuf = a_scr.shape[0]

    per_row = lambda m: jnp.concatenate([m] * t_new, axis=0)
    xt = jnp.swapaxes(x_ref[...].reshape(nb, t_new, d), 0, 1).reshape(rows, d)
    h_scr[...] = _modnorm(xt, gw_ref[...], per_row(sc_ref[...]), per_row(sh_ref[...])).astype(BF16)

    def up(buf, half, c0):
        cols = slice(c0, c0 + FF_CHUNK)
        for r in range(keep):
            a_scr[buf, half, r * nb:(r + 1) * nb, :] = st_ref[:, r, cols]
        a_scr[buf, half, keep * nb:keep * nb + rows, :] = jnp.dot(h_scr[...], wup_ref[:, cols],
                                                                  preferred_element_type=F32)
        for r in range(keep):
            nst_ref[:, r, cols] = a_scr[buf, half, rows + r * nb:rows + (r + 1) * nb, :]

    def conv_taps(c0):
        cols = slice(c0, c0 + FF_CHUNK)
        full = lambda row: jnp.broadcast_to(row, (SUBLANES, FF_CHUNK))
        return [full(cw_ref[k:k + 1, cols]) for k in range(CONV_W)] + [full(cb_ref[:, cols])]

    def conv_rows(buf, half, taps, r0):
        blk = lambda k: a_scr[buf, half, r0 + k * nb:r0 + k * nb + FF_ROWS, :].reshape(
            FF_ROWS // SUBLANES, SUBLANES, FF_CHUNK)
        w0, w1, w2, b = taps
        return w0 * blk(0) + w1 * blk(1) + w2 * blk(2) + b

    def gate(buf, cg, cu):
        taps_g, taps_u = conv_taps(cg), conv_taps(cu)
        for r0 in range(0, rows, FF_ROWS):
            act = _silu(conv_rows(buf, 0, taps_g, r0)) * conv_rows(buf, 1, taps_u, r0)
            act_scr[r0:r0 + FF_ROWS, cg:cg + FF_CHUNK] = act.reshape(FF_ROWS, FF_CHUNK).astype(BF16)

    chunks = _ffn_chunks(d_ff)
    acc = []

    def down(c0, c1):
        dn = jnp.dot(act_scr[:, c0:c1], wdn_ref[c0:c1, :], preferred_element_type=F32)
        acc[:] = [dn if not acc else acc[0] + dn]

    pending, gated = [], 0
    for j in range(len(chunks) + FFN_UP_AHEAD):
        if j < len(chunks):
            buf, (cg, cu) = j % nbuf, chunks[j]
            up(buf, 0, cg)
            up(buf, 1, cu)
            pending.append((buf, cg, cu))
        if j >= FFN_UP_AHEAD:
            gate(*pending.pop(0))
            gated += 1
            if gated % FFN_DOWN_GROUP == 0 or gated == len(chunks):
                first = (gated - 1) // FFN_DOWN_GROUP * FFN_DOWN_GROUP
                down(first * FF_CHUNK, gated * FF_CHUNK)
    out = _final(xt + per_row(g2_ref[...]) * acc[0], gf_ref)
    o_ref[...] = jnp.swapaxes(out.reshape(t_new, nb, d), 0, 1).reshape(rows, d)


def _ffn_sample(x, sc, sh, g2, gw, st2, sl, wup, wdn, l, cw, cb, gf, nb):
    db, d = sc.shape
    t_new = x.shape[0] // db
    c2 = wup.shape[-1]
    keep = CONV_W - 1
    rows = t_new * nb
    final = gf is not None
    seqs = lambda width: pl.BlockSpec((nb, width), lambda i: (i, 0))
    xrows = pl.BlockSpec((rows, d), lambda i: (i, 0))
    state = pl.BlockSpec((nb, keep, c2), lambda i: (i, 0, 0))
    state_in = pl.BlockSpec((None, nb, keep, c2), lambda i: (sl, i, 0, 0))
    in_specs = [xrows, seqs(d), seqs(d), seqs(d), _resident((1, d)), state_in,
                _layer_spec(wup, l), _resident(cw.shape), _resident(cb.shape), _layer_spec(wdn, l)]
    args = [x, sc, sh, g2, gw, st2, wup, cw, cb, wdn]
    if final:
        in_specs.append(_resident((1, d)))
        args.append(gf)
    return pl.pallas_call(
        functools.partial(_ffn_sample_kernel, final=final),
        grid=(db // nb,),
        in_specs=in_specs,
        out_specs=[xrows, state],
        out_shape=[jax.ShapeDtypeStruct((db * t_new, d), F32), jax.ShapeDtypeStruct((db, keep, c2), F32)],
        scratch_shapes=[pltpu.VMEM((rows, d), BF16),
                        pltpu.VMEM((FFN_UP_AHEAD + 1, 2, rows + keep * nb, FF_CHUNK), F32),
                        pltpu.VMEM((rows, c2 // 2), BF16)],
        compiler_params=_params(1),
        name="ffn_sample",
    )(*args)


def _tile_rows(size, want):
    return want if size % want == 0 else size


def kernel(x_prompt, x_sample, c_prompt, c_sample, cache_k, cache_v, state_conv, w_ada, b_ada, norm_mix,
           norm_ffn, w_qkv, b_qkv, attn_sink, w_o, w_sgu_in, b_sgu_in, sgu_ln_g, sgu_ln_b, w_spatial,
           b_spatial, w_sgu_out, w_up, conv_w, conv_b, w_down, norm_final):
    batch, seq, d = x_prompt.shape
    db, t_new, _ = x_sample.shape
    depth = w_ada.shape[0]

    pad = (-(batch + db)) % SUBLANES
    c_all = jnp.concatenate([c_prompt, c_sample, jnp.zeros((pad, d), F32)], axis=0)
    mod = _ada(c_all, w_ada, b_ada)

    def mod_part(l, k, lo, hi):
        return mod[l, lo:hi, k * d:(k + 1) * d]

    mp, ms = x_prompt.shape[0] * seq, db * t_new
    xp = x_prompt.reshape(mp, d)
    xs = x_sample.reshape(ms, d)

    tm_qkv_p, tm_sgu_p, tm_ffn_p = _tile_rows(seq, 1024), _tile_rows(seq, 1024), _tile_rows(seq, 1024)
    tm_qkv_s, tm_sgu_s = _tile_rows(ms, 512), _tile_rows(ms, 256)
    nb_ffn_s = _tile_rows(db, 512 // t_new)

    tab_p = _rope_tables(jnp.arange(seq))
    tab_s = _rope_tables(PAST_LEN + (jnp.arange(ms) % t_new))

    bf = {"up": w_up[0].astype(BF16), "down": w_down[0].astype(BF16),
          "a": w_qkv[0].astype(BF16), "b": w_o[0].astype(BF16)}

    kct = jnp.transpose(cache_k, (0, 1, 3, 4, 2))
    vct = jnp.transpose(cache_v, (0, 1, 3, 4, 2))

    new_k_p, new_v_p, new_conv_p = [], [], []
    new_k_s, new_v_s, new_conv_s, new_sgu_s = [], [], [], []

    for l in range(depth):
        idx = l // N_MIXERS
        gmix = norm_mix[l].reshape(1, d)
        gffn = norm_ffn[l].reshape(1, d)
        p_mod = [mod_part(l, k, 0, batch)[:, None, :] for k in range(6)]
        s_seq = [mod_part(l, k, batch, batch + db) for k in range(6)]
        seq_tiles = lambda a, tm: a.reshape(ms // tm, tm // t_new, d)

        if l % N_MIXERS == 0:
            bq = b_qkv[idx].reshape(1, -1)
            sink = attn_sink[idx] * LOG2E
            keep = min(WINDOW, seq)
            q, k, v, k_last, v_last = _qkv(xp, p_mod[1], p_mod[0], gmix, bf["a"], None, bq, tab_p, tm_qkv_p,
                                           seq // tm_qkv_p, seq // tm_qkv_p, tail_rows=keep)
            nblk = ATTN_BLOCKS_PER_STEP if seq % (ATTN_BLOCKS_PER_STEP * WINDOW) == 0 else 1
            xp = _attn_prompt(sink, q, k, v, bf["b"], None, xp, p_mod[2], batch, nblk)
            new_k_p.append(k_last.reshape(batch, keep, N_KV_HEADS, HEAD_DIM))
            new_v_p.append(v_last.reshape(batch, keep, N_KV_HEADS, HEAD_DIM))

            q, k, v = _qkv(xs, seq_tiles(s_seq[1], tm_qkv_s), seq_tiles(s_seq[0], tm_qkv_s), gmix, bf["a"], None, bq, tab_s,
                           tm_qkv_s, 1, ms // tm_qkv_s)
            xs, nk, nv = _attn_sample(sink, q, k, v, kct, vct, idx, bf["b"], None, xs, s_seq[2], t_new)
            new_k_s.append(nk)
            new_v_s.append(nv)
        else:
            b_in = b_sgu_in[idx].reshape(1, -1)
            lng = sgu_ln_g[idx].reshape(1, -1)
            lnb = sgu_ln_b[idx].reshape(1, -1)
            tc = min(seq, CHUNK)
            (xp,) = _sgu(xp, p_mod[1], p_mod[0], p_mod[2], gmix, bf["a"], bf["b"], None, b_in, lng, lnb,
                         w_spatial[idx][:, :tc, :tc], b_spatial[idx][:, :tc, None],
                         tm_sgu_p, seq // tm_sgu_p, False)
            reps = CHUNK // t_new
            eye = jnp.eye(reps, dtype=F32)
            wsp_s = jnp.einsum("ab,gts->gatbs", eye, w_spatial[idx][:, :t_new, :t_new]).reshape(
                SGU_GROUPS, CHUNK, CHUNK)
            bsp_s = jnp.tile(b_spatial[idx][:, :t_new], (1, reps))[:, :, None]
            xs, vrows = _sgu(xs, seq_tiles(s_seq[1], tm_sgu_s), seq_tiles(s_seq[0], tm_sgu_s),
                             seq_tiles(s_seq[2], tm_sgu_s), gmix, bf["a"], bf["b"], None,
                             b_in, lng, lnb, wsp_s, bsp_s, tm_sgu_s, 1, True)
            new_sgu_s.append(vrows.reshape(db, t_new, -1))

        cw = conv_w[l]
        cb = conv_b[l].reshape(1, -1)
        gf = norm_final.reshape(1, d) if l == depth - 1 else None
        casts = []
        if l + 1 < depth:
            nxt = (l + 1) // N_MIXERS
            mixer = [(w_qkv, nxt), (w_o, nxt)] if (l + 1) % N_MIXERS == 0 else [(w_sgu_in, nxt), (w_sgu_out, nxt)]
            casts = [(w_up, l + 1), (w_down, l + 1)] + mixer
        xp, tail, *cast = _ffn_prompt(xp, p_mod[4], p_mod[3], p_mod[5], gffn, bf["up"], bf["down"], None, cw, cb, gf,
                                      batch, tm_ffn_p, casts)
        new_conv_p.append(tail[:, SUBLANES - 1::SUBLANES, :])

        xs, nst = _ffn_sample(xs, s_seq[4], s_seq[3], s_seq[5], gffn,
                              state_conv, l, bf["up"], bf["down"], None, cw, cb, gf, nb_ffn_s)
        new_conv_s.append(nst)
        if cast:
            bf = dict(zip(("up", "down", "a", "b"), cast))

    return (xp.reshape(batch, seq, d), xs.reshape(db, t_new, d),
            jnp.stack(new_k_p), jnp.stack(new_v_p), jnp.stack(new_conv_p),
            jnp.transpose(jnp.stack(new_k_s), (0, 1, 4, 2, 3)), jnp.transpose(jnp.stack(new_v_s), (0, 1, 4, 2, 3)),
            jnp.stack(new_conv_s), jnp.stack(new_sgu_s))
```
